```python
import jax
import jax.numpy as jnp
from jax import lax
import numpy as np

D_MODEL = 4096
BATCH = 32
SEQ = 256
DEPTH = 2
DEC_BATCH = 2
DEC_SEQ = 4096
PAST_LEN = 512

GRID_W = 64
D_MIX = D_MODEL
GROUP_W = D_MIX // 4

SGU_CHUNK = 128
SGU_GROUPS = 4
SGU_DIM = GROUP_W
SGU_GCH = SGU_DIM // SGU_GROUPS

HG_HEADS = 8
HG_DK = 128
HG_DV = GROUP_W // HG_HEADS
HG_K = HG_HEADS * HG_DK

N_HEADS = 8
N_KV_HEADS = 2
HEAD_DIM = GROUP_W // N_HEADS
WINDOW = 128
ATT_BLOCK = 128
ROPE_THETA = 10000.0

GLA_HEADS = 4
GLA_DK = 128
GLA_DV = GROUP_W // GLA_HEADS
GLA_RANK = 16
GLA_NORMALIZER = 16.0

SCAN_CHUNK = 32
D_FF = 11008
CONV_W = 3
EPS = 1e-6
F32 = jnp.float32

IN_SIZES = (2 * SGU_DIM,
            HG_K, HG_K, HG_K, HG_HEADS * HG_DV, HG_HEADS * HG_DV,
            N_HEADS * HEAD_DIM, N_KV_HEADS * HEAD_DIM, N_KV_HEADS * HEAD_DIM,
            GLA_HEADS * GLA_DK, GLA_HEADS * GLA_DK, GLA_HEADS * GLA_DV, GLA_HEADS * GLA_DV,
            GLA_RANK, GLA_RANK)
D_IN = sum(IN_SIZES)

kernel_name = 'hybrid_diffusion_prefix_step'


def rmsnorm(x, g):
    xf = x.astype(F32)
    y = xf * lax.rsqrt(jnp.mean(xf * xf, axis=-1, keepdims=True) + EPS)
    return (y * g.astype(F32)).astype(x.dtype)


def layernorm(x, g, b):
    xf = x.astype(F32)
    xc = xf - jnp.mean(xf, axis=-1, keepdims=True)
    y = xc * lax.rsqrt(jnp.mean(xc * xc, axis=-1, keepdims=True) + EPS)
    return (y * g.astype(F32) + b.astype(F32)).astype(x.dtype)


def split_proj(z):
    cuts = [int(c) for c in np.cumsum(IN_SIZES)[:-1]]
    return jnp.split(z, cuts, axis=-1)


def sgu_mixer(z, ln_g, ln_b, w_s, b_s):
    B, L, _ = z.shape
    n = L // SGU_CHUNK
    z = jax.nn.gelu(z, approximate=False)
    u, v = jnp.split(z, 2, axis=-1)
    v = layernorm(v, ln_g, ln_b).reshape(B, n, SGU_CHUNK, SGU_GROUPS, SGU_GCH)
    v = jnp.einsum('gts,bnsgc->bntgc', w_s, v) + b_s.T[None, None, :, :, None]
    return u * v.reshape(B, L, SGU_DIM)


def gated_linear_scan(q, k, v, log_a, s0):
    B, L, H, _ = q.shape
    dv = v.shape[-1]
    C = SCAN_CHUNK
    n = L // C

    def chunks(t):
        return jnp.moveaxis(t.astype(F32).reshape(B, n, C, H, t.shape[-1]), 1, 0)

    causal = jnp.tril(jnp.ones((C, C), dtype=bool))[None, :, :, None, None]

    def step(S, inp):
        qi, ki, vi, ai = inp
        b = jnp.cumsum(ai, axis=1)
        o_inter = jnp.einsum('bchk,bhkv->bchv', qi * jnp.exp(b), S)
        diff = b[:, :, None] - b[:, None, :]
        decay = jnp.exp(jnp.where(causal, diff, -jnp.inf))
        scores = jnp.einsum('bthk,bshk,btshk->bhts', qi, ki, decay)
        o_intra = jnp.einsum('bhts,bshv->bthv', scores, vi)
        b_last = b[:, -1]
        k_dec = ki * jnp.exp(b_last[:, None] - b)
        S_new = jnp.exp(b_last)[..., None] * S + jnp.einsum('bchk,bchv->bhkv', k_dec, vi)
        return S_new, o_inter + o_intra

    S_fin, o = lax.scan(step, s0.astype(F32), (chunks(q), chunks(k), chunks(v), chunks(log_a)))
    o = jnp.moveaxis(o, 0, 1).reshape(B, L, H, dv)
    return o.astype(v.dtype), S_fin.astype(s0.dtype)


def bidir_scan(q, k_f, k_b, v, la_f, la_b, s0_f, s0_b):
    o_f, s_f = gated_linear_scan(q, k_f, v, la_f, s0_f)
    rev = lambda t: jnp.flip(t, axis=1)
    o_b, s_b = gated_linear_scan(rev(q), rev(k_b), rev(v), rev(la_b), s0_b)
    return o_f + rev(o_b), s_f, s_b


def hgrn_gate(f_raw, lb):
    f = lb + (1.0 - lb) * jax.nn.sigmoid(f_raw.astype(F32))
    return 1.0 - f, jnp.log(f)


def gla_log_decay(low, w_up, b):
    g = jnp.einsum('blr,rk->blk', low, w_up).astype(F32) + b.astype(F32)
    return jax.nn.log_sigmoid(g) / GLA_NORMALIZER


def head_norm_gate(o, g_raw, gain):
    B, L, H, dv = o.shape
    return (rmsnorm(o, gain) * jax.nn.silu(g_raw.reshape(B, L, H, dv))).reshape(B, L, H * dv)


def axial_rope_tables(L):
    rows = L // GRID_W
    row = jnp.repeat(jnp.arange(rows), GRID_W).astype(F32)
    col = jnp.tile(jnp.arange(GRID_W), rows).astype(F32)
    n_freq = HEAD_DIM // 4
    inv = ROPE_THETA ** (-jnp.arange(n_freq, dtype=F32) / n_freq)
    ang = jnp.concatenate([row[:, None] * inv, col[:, None] * inv], axis=-1)
    return jnp.cos(ang), jnp.sin(ang)


def apply_rope(x, cos, sin):
    half = x.shape[-1] // 2
    x1, x2 = x[..., :half].astype(F32), x[..., half:].astype(F32)
    c, s = cos[None, :, None], sin[None, :, None]
    return jnp.concatenate([x1 * c - x2 * s, x1 * s + x2 * c], axis=-1).astype(x.dtype)


def context_attention(q, k, v, sink):
    B, S, H, hd = q.shape
    G = H // N_KV_HEADS
    nb = S // ATT_BLOCK
    scale = hd ** -0.5
    qb = jnp.moveaxis(q.reshape(B, nb, ATT_BLOCK, N_KV_HEADS, G, hd), 1, 0)
    sink_s = jnp.broadcast_to(sink.astype(F32).reshape(1, N_KV_HEADS, G, 1, 1), (B, N_KV_HEADS, G, ATT_BLOCK, 1))

    def block(qi):
        s = jnp.einsum('bqkgd,bskd->bkgqs', qi, k, preferred_element_type=F32) * scale
        p = jax.nn.softmax(jnp.concatenate([sink_s, s], axis=-1), axis=-1)[..., 1:]
        return jnp.einsum('bkgqs,bskd->bqkgd', p.astype(v.dtype), v)

    o = lax.map(block, qb)
    return jnp.moveaxis(o, 0, 1).reshape(B, S, H * hd)


def latent_window_attention(q, k, v, ck, cv, sink):
    B, L, H, hd = q.shape
    G = H // N_KV_HEADS
    nb = L // ATT_BLOCK
    P = ck.shape[1]
    scale = hd ** -0.5
    qb = q.reshape(B, nb, ATT_BLOCK, N_KV_HEADS, G, hd)

    def band(t):
        tb = jnp.pad(t.reshape(B, nb, ATT_BLOCK, N_KV_HEADS, hd), ((0, 0), (1, 1), (0, 0), (0, 0), (0, 0)))
        return jnp.concatenate([tb[:, :-2], tb[:, 1:-1], tb[:, 2:]], axis=2)

    kb, vb = band(k), band(v)
    qi = jnp.arange(ATT_BLOCK)[:, None] + ATT_BLOCK
    kj = jnp.arange(3 * ATT_BLOCK)[None, :]
    key_pos = (jnp.arange(nb)[:, None] - 1) * ATT_BLOCK + kj
    valid = (jnp.abs(qi - kj) <= WINDOW)[None] & ((key_pos >= 0) & (key_pos < L))[:, None, :]
    s_loc = jnp.einsum('bnqkgd,bnskd->bnkgqs', qb, kb, preferred_element_type=F32) * scale
    s_loc = jnp.where(valid[None, :, None, None], s_loc, -jnp.inf)
    s_ctx = jnp.einsum('bnqkgd,bpkd->bnkgqp', qb, ck, preferred_element_type=F32) * scale
    sink_s = jnp.broadcast_to(sink.astype(F32).reshape(1, 1, N_KV_HEADS, G, 1, 1), (B, nb, N_KV_HEADS, G, ATT_BLOCK, 1))
    p = jax.nn.softmax(jnp.concatenate([sink_s, s_ctx, s_loc], axis=-1), axis=-1).astype(v.dtype)
    o = (jnp.einsum('bnkgqp,bpkd->bnqkgd', p[..., 1:1 + P], cv)
         + jnp.einsum('bnkgqs,bnskd->bnqkgd', p[..., 1 + P:], vb))
    return o.reshape(B, L, H * hd)


def conv_ffn(h, w_up, conv_w, conv_b, w_down):
    L = h.shape[1]
    gate, val = jnp.split(h @ w_up, 2, axis=-1)
    pad = CONV_W // 2
    gp = jnp.pad(gate, ((0, 0), (pad, pad), (0, 0)))
    conv = gp[:, 0:L] * conv_w[0]
    for j in range(1, CONV_W):
        conv = conv + gp[:, j:j + L] * conv_w[j]
    return (jax.nn.gelu(conv + conv_b, approximate=False) * val) @ w_down


def token_mixers(h, lp, lb, ctx):
    B, L, _ = h.shape
    z = h @ lp['w_in']
    (z_sgu, hq, hff, hfb, hi, hg, aq, ak, av, gq, gk, gv, gg, glf, glb) = split_proj(z)
    heads = lambda t, n: t.reshape(B, L, n, -1)
    o_sgu = sgu_mixer(z_sgu, lp['sgu_ln_g'], lp['sgu_ln_b'], lp['sgu_w'], lp['sgu_b'])
    q_hg = heads(jax.nn.silu(hq), HG_HEADS) * HG_DK ** -0.5
    k_hf, la_hf = hgrn_gate(hff, lb[0])
    k_hb, la_hb = hgrn_gate(hfb, lb[1])
    k_hf, la_hf, k_hb, la_hb = (heads(t, HG_HEADS) for t in (k_hf, la_hf, k_hb, la_hb))
    v_hg = heads(hi, HG_HEADS)
    q_a, k_a, v_a = heads(aq, N_HEADS), heads(ak, N_KV_HEADS), heads(av, N_KV_HEADS)
    q_g = heads(gq, GLA_HEADS) * GLA_DK ** -0.5
    k_g, v_g = heads(gk, GLA_HEADS), heads(gv, GLA_HEADS)
    la_gf = heads(gla_log_decay(glf, lp['gla_w_gk'][0], lp['gla_b_gk'][0]), GLA_HEADS)
    la_gb = heads(gla_log_decay(glb, lp['gla_w_gk'][1], lp['gla_b_gk'][1]), GLA_HEADS)
    if ctx is None:
        s_hg = jnp.zeros((B, 2, HG_HEADS, HG_DK, HG_DV), h.dtype)
        s_gl = jnp.zeros((B, 2, GLA_HEADS, GLA_DK, GLA_DV), h.dtype)
        o_att = context_attention(q_a, k_a, v_a, lp['attn_sink'])
    else:
        ck, cv, s_hg, s_gl = ctx
        cos, sin = axial_rope_tables(L)
        o_att = latent_window_attention(apply_rope(q_a, cos, sin), apply_rope(k_a, cos, sin), v_a,
                                        ck, cv, lp['attn_sink'])
    o_hg, hg_f, hg_b = bidir_scan(q_hg, k_hf, k_hb, v_hg, la_hf, la_hb, s_hg[:, 0], s_hg[:, 1])
    o_gl, gl_f, gl_b = bidir_scan(q_g, k_g, k_g, v_g, la_gf, la_gb, s_gl[:, 0], s_gl[:, 1])
    o_hg = head_norm_gate(o_hg, hg, lp['hgrn_norm_g'])
    o_gl = head_norm_gate(o_gl, gg, lp['gla_norm_g'])
    out = jnp.concatenate([o_sgu, o_hg, o_att, o_gl], axis=-1) @ lp['w_out']
    ctx_tensors = (k_a, v_a, jnp.stack([hg_f, hg_b], axis=1), jnp.stack([gl_f, gl_b], axis=1))
    return out, ctx_tensors


def trunk_layer(x, mod, lp, lb, ctx):
    sh1, sc1, g1, sh2, sc2, g2 = [m[:, None, :] for m in jnp.split(mod, 6, axis=-1)]
    ng = lp['norm_g']
    h = rmsnorm(x, ng[0]) * (1.0 + sc1) + sh1
    m, ctx_tensors = token_mixers(h, lp, lb, ctx)
    x = x + g1 * rmsnorm(m, ng[1])
    h = rmsnorm(x, ng[2]) * (1.0 + sc2) + sh2
    f = conv_ffn(h, lp['ffn_w_up'], lp['ffn_conv_w'], lp['ffn_conv_b'], lp['ffn_w_down'])
    x = x + g2 * rmsnorm(f, ng[3])
    return x, ctx_tensors


def setup_inputs(seed: int = 0) -> dict:
    key = jax.random.key(seed)
    ks = iter(list(jax.random.split(key, 40)))
    nrm = lambda shape, scale=1.0: jax.random.normal(next(ks), shape, F32) * scale
    D = D_MODEL
    return {
        'x_prompt': nrm((BATCH, SEQ, D)),
        'x_sample': nrm((DEC_BATCH, DEC_SEQ, D)),
        'cache_k': nrm((DEC_BATCH, DEPTH, PAST_LEN, N_KV_HEADS, HEAD_DIM)),
        'cache_v': nrm((DEC_BATCH, DEPTH, PAST_LEN, N_KV_HEADS, HEAD_DIM)),
        'state_hgrn': nrm((DEC_BATCH, DEPTH, 2, HG_HEADS, HG_DK, HG_DV)),
        'state_gla': nrm((DEC_BATCH, DEPTH, 2, GLA_HEADS, GLA_DK, GLA_DV)),
        'c': nrm((DEC_BATCH, D)),
        'c_ctx': nrm((D,)),
        'w_ada': nrm((DEPTH, D, 6 * D), D ** -0.5),
        'b_ada': nrm((DEPTH, 6 * D), 0.02),
        'norm_g': 1.0 + nrm((DEPTH, 4, D), 0.02),
        'w_in': nrm((DEPTH, D, D_IN), D ** -0.5),
        'w_out': nrm((DEPTH, D_MIX, D), D_MIX ** -0.5),
        'sgu_ln_g': 1.0 + nrm((DEPTH, SGU_DIM), 0.02),
        'sgu_ln_b': nrm((DEPTH, SGU_DIM), 0.02),
        'sgu_w': nrm((DEPTH, SGU_GROUPS, SGU_CHUNK, SGU_CHUNK), SGU_CHUNK ** -0.5),
        'sgu_b': 1.0 + nrm((DEPTH, SGU_GROUPS, SGU_CHUNK), 0.02),
        'hgrn_lb_logits': nrm((DEPTH, 2, HG_K), 0.1),
        'hgrn_norm_g': 1.0 + nrm((DEPTH, HG_DV), 0.02),
        'attn_sink': nrm((DEPTH, N_HEADS), 0.5),
        'gla_w_gk': nrm((DEPTH, 2, GLA_RANK, GLA_HEADS * GLA_DK), GLA_RANK ** -0.5),
        'gla_b_gk': nrm((DEPTH, 2, GLA_HEADS * GLA_DK), 0.02),
        'gla_norm_g': 1.0 + nrm((DEPTH, GLA_DV), 0.02),
        'ffn_w_up': nrm((DEPTH, D, 2 * D_FF), D ** -0.5),
        'ffn_conv_w': nrm((DEPTH, CONV_W, D_FF), CONV_W ** -0.5),
        'ffn_conv_b': nrm((DEPTH, D_FF), 0.02),
        'ffn_w_down': nrm((DEPTH, D_FF, D), D_FF ** -0.5),
    }


def reference(x_prompt, x_sample, cache_k, cache_v, state_hgrn, state_gla, c, c_ctx,
              w_ada, b_ada, norm_g, w_in, w_out, sgu_ln_g, sgu_ln_b, sgu_w, sgu_b,
              hgrn_lb_logits, hgrn_norm_g, attn_sink, gla_w_gk, gla_b_gk, gla_norm_g,
              ffn_w_up, ffn_conv_w, ffn_conv_b, ffn_w_down):
    lb_all = jnp.cumsum(jax.nn.softmax(hgrn_lb_logits.astype(F32), axis=0), axis=0)
    lb_all = lb_all - lb_all[0:1]
    xp, xs = x_prompt, x_sample
    ks_new, vs_new, hg_new, gl_new = [], [], [], []
    for l in range(DEPTH):
        lp = {
            'norm_g': norm_g[l], 'w_in': w_in[l], 'w_out': w_out[l],
            'sgu_ln_g': sgu_ln_g[l], 'sgu_ln_b': sgu_ln_b[l], 'sgu_w': sgu_w[l], 'sgu_b': sgu_b[l],
            'hgrn_norm_g': hgrn_norm_g[l], 'attn_sink': attn_sink[l],
            'gla_w_gk': gla_w_gk[l], 'gla_b_gk': gla_b_gk[l], 'gla_norm_g': gla_norm_g[l],
            'ffn_w_up': ffn_w_up[l], 'ffn_conv_w': ffn_conv_w[l], 'ffn_conv_b': ffn_conv_b[l],
            'ffn_w_down': ffn_w_down[l],
        }
        mod_ctx = jax.nn.silu(c_ctx)[None, :] @ w_ada[l] + b_ada[l]
        mod_lat = jax.nn.silu(c) @ w_ada[l] + b_ada[l]
        xp, (k_l, v_l, hg_l, gl_l) = trunk_layer(xp, mod_ctx, lp, lb_all[l], None)
        ks_new.append(k_l)
        vs_new.append(v_l)
        hg_new.append(hg_l)
        gl_new.append(gl_l)
        ctx = (cache_k[:, l], cache_v[:, l], state_hgrn[:, l], state_gla[:, l])
        xs, _ = trunk_layer(xs, mod_lat, lp, lb_all[l], ctx)
    new_cache_k = jnp.stack(ks_new, axis=1)
    new_cache_v = jnp.stack(vs_new, axis=1)
    new_state_hgrn = jnp.stack(hg_new, axis=1)
    new_state_gla = jnp.stack(gl_new, axis=1)
    return (xp, xs, new_cache_k, new_cache_v, new_state_hgrn, new_state_gla)
```

```python
import functools

import numpy as np
import jax
import jax.numpy as jnp
from jax import lax
from jax.experimental import pallas as pl
from jax.experimental.pallas import tpu as pltpu

D_MODEL = 4096
BATCH = 32
SEQ = 256
DEPTH = 2
DEC_BATCH = 2
DEC_SEQ = 4096
PAST_LEN = 512
GRID_W = 64
GROUP_W = D_MODEL // 4
SGU_CHUNK = 128
SGU_GROUPS = 4
SGU_DIM = GROUP_W
SGU_GCH = SGU_DIM // SGU_GROUPS
HG_HEADS = 8
HG_DK = 128
HG_DV = GROUP_W // HG_HEADS
HG_K = HG_HEADS * HG_DK
N_HEADS = 8
N_KV_HEADS = 2
HEAD_DIM = GROUP_W // N_HEADS
WINDOW = 128
ATT_BLOCK = 128
ROPE_THETA = 10000.0
GLA_HEADS = 4
GLA_DK = 128
GLA_DV = GROUP_W // GLA_HEADS
GLA_RANK = 16
GLA_NORMALIZER = 16.0
D_FF = 11008
CONV_W = 3
EPS = 1e-6

F32 = jnp.float32
BF16 = jnp.bfloat16

IN_SIZES = (2 * SGU_DIM,
            HG_K, HG_K, HG_K, HG_HEADS * HG_DV, HG_HEADS * HG_DV,
            N_HEADS * HEAD_DIM, N_KV_HEADS * HEAD_DIM, N_KV_HEADS * HEAD_DIM,
            GLA_HEADS * GLA_DK, GLA_HEADS * GLA_DK, GLA_HEADS * GLA_DV, GLA_HEADS * GLA_DV,
            GLA_RANK, GLA_RANK)
D_IN = sum(IN_SIZES)
_OFF = [0] + [int(c) for c in np.cumsum(IN_SIZES)]
(OFF_SGU, OFF_HQ, OFF_HFF, OFF_HFB, OFF_HI, OFF_HG, OFF_AQ, OFF_AK, OFF_AV,
 OFF_GQ, OFF_GK, OFF_GV, OFF_GG, OFF_GLF, OFF_GLB) = _OFF[:-1]

LANE = 128
VMEM_LIMIT = 48 * 1024 * 1024

Z_TILE_N = 512
Z_COLS = -(-D_IN // Z_TILE_N) * Z_TILE_N
FF_TILE_N = 512
D_FF_PAD = -(-D_FF // (2 * FF_TILE_N)) * (2 * FF_TILE_N)
SCAN_C = 128
MOD_ROWS = 16

TP = BATCH * SEQ
TL = DEC_BATCH * DEC_SEQ
T = TP + TL


def _cparams(sem):
    return pltpu.CompilerParams(dimension_semantics=sem, vmem_limit_bytes=VMEM_LIMIT)


def _row_group(row0):
    return jnp.where(row0 < TP, 0, 1 + (row0 - TP) // DEC_SEQ)


def _dot(a, b):
    return jnp.dot(a, b, preferred_element_type=F32)


def _dot_nt(a, b):
    return lax.dot_general(a, b, (((1,), (1,)), ((), ())), preferred_element_type=F32)


def _dot_tn(a, b):
    return lax.dot_general(a, b, (((0,), (0,)), ((), ())), preferred_element_type=F32)


def _sigmoid(x):
    return 1.0 / (1.0 + jnp.exp(-x))


def _silu(x):
    return x * _sigmoid(x)


def _gelu(x):
    return 0.5 * x * (1.0 + lax.erf(x * (2.0 ** -0.5)))


def _rms(x, g):
    return x * lax.rsqrt(jnp.mean(x * x, axis=-1, keepdims=True) + EPS) * g


def _ada_kernel(c_ref, w_ref, b_ref, o_ref):
    c = _silu(c_ref[...]).astype(BF16)
    o_ref[...] = _dot(c, w_ref[...].astype(BF16)) + b_ref[...]


def _ada_mod(cvec, w_ada, b_ada):
    tn = 512
    n = w_ada.shape[-1]
    return pl.pallas_call(
        _ada_kernel,
        grid=(DEPTH, n // tn),
        in_specs=[pl.BlockSpec((MOD_ROWS, D_MODEL), lambda l, j: (0, 0)),
                  pl.BlockSpec((None, D_MODEL, tn), lambda l, j: (l, 0, j)),
                  pl.BlockSpec((None, 1, tn), lambda l, j: (l, 0, j))],
        out_specs=pl.BlockSpec((None, MOD_ROWS, tn), lambda l, j: (l, 0, j)),
        out_shape=jax.ShapeDtypeStruct((DEPTH, MOD_ROWS, n), F32),
        compiler_params=_cparams(("parallel", "parallel")),
        name="ada_mod",
    )(cvec, w_ada, b_ada.reshape(DEPTH, 1, n))


def _mod_spec(layer, chunk, tr):
    return pl.BlockSpec((None, None, 1, D_MODEL),
                        lambda i, *_: (layer, _row_group(i * tr), 0, chunk))


NORM_ROWS = 256


def _norm_mod_kernel(x_ref, g_ref, sc_ref, sh_ref, h_ref):
    y = _rms(x_ref[...], g_ref[...])
    h_ref[...] = (y * (1.0 + sc_ref[...]) + sh_ref[...]).astype(h_ref.dtype)


def _norm_mod(x, g, mod, layer, sc_chunk, sh_chunk):
    tr = NORM_ROWS
    row = pl.BlockSpec((tr, D_MODEL), lambda i: (i, 0))
    vec = pl.BlockSpec((1, D_MODEL), lambda i: (0, 0))
    return pl.pallas_call(
        _norm_mod_kernel,
        grid=(T // tr,),
        in_specs=[row, vec, _mod_spec(layer, sc_chunk, tr), _mod_spec(layer, sh_chunk, tr)],
        out_specs=row,
        out_shape=jax.ShapeDtypeStruct((T, D_MODEL), BF16),
        compiler_params=_cparams(("parallel",)),
        name="norm_mod",
    )(x, g.reshape(1, D_MODEL), mod, mod)


def _resid_kernel(x_ref, m_ref, ga_ref, gate_ref, *rest, emit_h):
    x = x_ref[...] + gate_ref[...] * _rms(m_ref[...], ga_ref[...])
    if emit_h:
        gb_ref, sc_ref, sh_ref, xo_ref, h_ref = rest
        xo_ref[...] = x
        y = _rms(x, gb_ref[...])
        h_ref[...] = (y * (1.0 + sc_ref[...]) + sh_ref[...]).astype(h_ref.dtype)
    else:
        (xo_ref,) = rest
        xo_ref[...] = x


def _resid(x, m, ga, mod, layer, gate_chunk, nxt=None):
    tr = NORM_ROWS
    row = pl.BlockSpec((tr, D_MODEL), lambda i: (i, 0))
    vec = pl.BlockSpec((1, D_MODEL), lambda i: (0, 0))
    in_specs = [row, row, vec, _mod_spec(layer, gate_chunk, tr)]
    args = [x, m, ga.reshape(1, D_MODEL), mod]
    out_specs = [row]
    out_shape = [jax.ShapeDtypeStruct((T, D_MODEL), F32)]
    if nxt is not None:
        gb, layer_b, sc_chunk, sh_chunk = nxt
        in_specs += [vec, _mod_spec(layer_b, sc_chunk, tr), _mod_spec(layer_b, sh_chunk, tr)]
        args += [gb.reshape(1, D_MODEL), mod, mod]
        out_specs.append(row)
        out_shape.append(jax.ShapeDtypeStruct((T, D_MODEL), BF16))
    res = pl.pallas_call(
        functools.partial(_resid_kernel, emit_h=nxt is not None),
        grid=(T // tr,),
        in_specs=in_specs,
        out_specs=out_specs,
        out_shape=out_shape,
        compiler_params=_cparams(("parallel",)),
        name="resid_norm",
    )(*args)
    return res if nxt is not None else (res[0], None)


MM_TILE_M = 1024


def _mm_kernel(a_ref, b_ref, o_ref):
    o_ref[...] = _dot(a_ref[...], b_ref[...]).astype(o_ref.dtype)


def _matmul(a, b, out_dtype, tn, name):
    m, k = a.shape
    n = b.shape[1]
    tm = MM_TILE_M
    return pl.pallas_call(
        _mm_kernel,
        grid=(m // tm, n // tn),
        in_specs=[pl.BlockSpec((tm, k), lambda i, j: (i, 0)),
                  pl.BlockSpec((k, tn), lambda i, j: (0, j))],
        out_specs=pl.BlockSpec((tm, tn), lambda i, j: (i, j)),
        out_shape=jax.ShapeDtypeStruct((m, n), out_dtype),
        compiler_params=_cparams(("parallel", "parallel")),
        name=name,
    )(a, b)


def _mm4_kernel(a0_ref, a1_ref, a2_ref, a3_ref, b_ref, o_ref):
    acc = _dot(a0_ref[...], b_ref[0 * GROUP_W:1 * GROUP_W, :])
    acc += _dot(a1_ref[...], b_ref[1 * GROUP_W:2 * GROUP_W, :])
    acc += _dot(a2_ref[...], b_ref[2 * GROUP_W:3 * GROUP_W, :])
    acc += _dot(a3_ref[...], b_ref[3 * GROUP_W:4 * GROUP_W, :])
    o_ref[...] = acc


def _out_proj(parts, w):
    tm, tn = MM_TILE_M, 512
    a_spec = pl.BlockSpec((tm, GROUP_W), lambda i, j: (i, 0))
    return pl.pallas_call(
        _mm4_kernel,
        grid=(T // tm, D_MODEL // tn),
        in_specs=[a_spec, a_spec, a_spec, a_spec,
                  pl.BlockSpec((4 * GROUP_W, tn), lambda i, j: (0, j))],
        out_specs=pl.BlockSpec((tm, tn), lambda i, j: (i, j)),
        out_shape=jax.ShapeDtypeStruct((T, D_MODEL), F32),
        compiler_params=_cparams(("parallel", "parallel")),
        name="out_proj",
    )(*parts, w)


def _mm_ktiled_kernel(a_ref, b_ref, o_ref):
    k = pl.program_id(2)
    d = _dot(a_ref[...], b_ref[...])

    @pl.when(k == 0)
    def _():
        o_ref[...] = d

    @pl.when(k != 0)
    def _():
        o_ref[...] += d


def _ffn_down(a, b):
    m, kk = a.shape
    n = b.shape[1]
    tm, tn, tk = MM_TILE_M, 512, kk // 2
    return pl.pallas_call(
        _mm_ktiled_kernel,
        grid=(m // tm, n // tn, kk // tk),
        in_specs=[pl.BlockSpec((tm, tk), lambda i, j, k: (i, k)),
                  pl.BlockSpec((tk, tn), lambda i, j, k: (k, j))],
        out_specs=pl.BlockSpec((tm, tn), lambda i, j, k: (i, j)),
        out_shape=jax.ShapeDtypeStruct((m, n), F32),
        compiler_params=_cparams(("parallel", "parallel", "arbitrary")),
        name="ffn_down",
    )(a, b)


FFN_TILE_M = 512
HALO_ROWS = 16


def _ffn_up_kernel(h_ref, halo_ref, wg_ref, wv_ref, cw_ref, cb_ref, o_ref):
    tm = h_ref.shape[0]
    h = h_ref[...]
    g = _dot(h, wg_ref[...])
    gh = _dot(halo_ref[...], wg_ref[...])
    val = _dot(h, wv_ref[...])
    row0 = pl.program_id(0) * tm
    r = lax.broadcasted_iota(jnp.int32, g.shape, 0)
    seq_len = jnp.where(row0 < TP, SEQ, DEC_SEQ)
    pos = (row0 + r) & (seq_len - 1)
    prev = jnp.where(r == 0, gh[0:1, :], pltpu.roll(g, 1, 0))
    prev = jnp.where(pos == 0, 0.0, prev)
    nxt = jnp.where(r == tm - 1, gh[1:2, :], pltpu.roll(g, tm - 1, 0))
    nxt = jnp.where(pos == seq_len - 1, 0.0, nxt)
    conv = prev * cw_ref[0:1, :] + g * cw_ref[1:2, :] + nxt * cw_ref[2:3, :] + cb_ref[...]
    o_ref[...] = (_gelu(conv) * val).astype(o_ref.dtype)


def _ffn_up(h, halo, w_up, conv_w, conv_b):
    tm, tn = FFN_TILE_M, FF_TILE_N
    nj = D_FF_PAD // tn
    return pl.pallas_call(
        _ffn_up_kernel,
        grid=(T // tm, nj),
        in_specs=[pl.BlockSpec((tm, D_MODEL), lambda i, j: (i, 0)),
                  pl.BlockSpec((None, HALO_ROWS, D_MODEL), lambda i, j: (i, 0, 0)),
                  pl.BlockSpec((D_MODEL, tn), lambda i, j: (0, j)),
                  pl.BlockSpec((D_MODEL, tn), lambda i, j: (0, nj + j)),
                  pl.BlockSpec((8, tn), lambda i, j: (0, j)),
                  pl.BlockSpec((1, tn), lambda i, j: (0, j))],
        out_specs=pl.BlockSpec((tm, tn), lambda i, j: (i, j)),
        out_shape=jax.ShapeDtypeStruct((T, D_FF_PAD), BF16),
        compiler_params=_cparams(("parallel", "parallel")),
        name="ffn_up",
    )(h, halo, w_up, w_up, conv_w, conv_b)


def _conv_halo(h):
    tm = FFN_TILE_M
    nt = T // tm
    ht = h.reshape(nt, tm, D_MODEL)
    zero = jnp.zeros((1, D_MODEL), h.dtype)
    before = jnp.concatenate([zero, ht[:-1, tm - 1]], axis=0)
    after = jnp.concatenate([ht[1:, 0], zero], axis=0)
    pad = jnp.zeros((nt, HALO_ROWS - 2, D_MODEL), h.dtype)
    return jnp.concatenate([before[:, None], after[:, None], pad], axis=1)


def _sgu_kernel(u_ref, v_ref, g_ref, b_ref, w_ref, bias_ref, o_ref):
    u = _gelu(u_ref[...])
    v = _gelu(v_ref[...])
    vc = v - jnp.mean(v, axis=-1, keepdims=True)
    vn = vc * lax.rsqrt(jnp.mean(vc * vc, axis=-1, keepdims=True) + EPS) * g_ref[...] + b_ref[...]
    vb = vn.astype(BF16)
    for g in range(SGU_GROUPS):
        sl = slice(g * SGU_GCH, (g + 1) * SGU_GCH)
        mixed = _dot(w_ref[g], vb[:, sl]) + bias_ref[:, sl]
        o_ref[:, sl] = (u[:, sl] * mixed).astype(o_ref.dtype)


def _sgu(z, ln_g, ln_b, w_s, b_s):
    c = SGU_CHUNK
    bias = jnp.repeat(b_s.T, SGU_GCH, axis=1)
    vec = pl.BlockSpec((1, SGU_DIM), lambda i: (0, 0))
    return pl.pallas_call(
        _sgu_kernel,
        grid=(T // c,),
        in_specs=[pl.BlockSpec((c, SGU_DIM), lambda i: (i, OFF_SGU // SGU_DIM)),
                  pl.BlockSpec((c, SGU_DIM), lambda i: (i, OFF_SGU // SGU_DIM + 1)),
                  vec, vec,
                  pl.BlockSpec((SGU_GROUPS, c, c), lambda i: (0, 0, 0)),
                  pl.BlockSpec((c, SGU_DIM), lambda i: (0, 0))],
        out_specs=pl.BlockSpec((c, SGU_DIM), lambda i: (i, 0)),
        out_shape=jax.ShapeDtypeStruct((T, SGU_DIM), BF16),
        compiler_params=_cparams(("parallel",)),
        name="sgu",
    )(z, z, ln_g.reshape(1, SGU_DIM), ln_b.reshape(1, SGU_DIM), w_s.astype(BF16), bias)


def _split3(x):
    hi = x.astype(BF16)
    r = x - hi.astype(F32)
    mid = r.astype(BF16)
    lo = (r - mid.astype(F32)).astype(BF16)
    return hi, mid, lo


def _chunk_scan(q, k, v, la, st, reverse):
    c, dk = q.shape
    t_i = lax.broadcasted_iota(jnp.int32, (c, c), 0)
    s_i = lax.broadcasted_iota(jnp.int32, (c, c), 1)
    tri = jnp.where((s_i >= t_i) if reverse else (s_i <= t_i), 1.0, 0.0).astype(BF16)
    hi, mid, lo = _split3(la)
    b = _dot(tri, hi) + _dot(tri, mid) + _dot(tri, lo)
    row = lax.broadcasted_iota(jnp.int32, (c, dk), 0)
    x = t_i ^ s_i
    scores = jnp.where(x == 0, _dot_nt(q.astype(BF16), k.astype(BF16)), 0.0)
    e = b
    m = 1
    while m < c:
        second = (row & m) != 0
        q_rows = jnp.logical_not(second) if reverse else second
        sh_q, sh_e = (c - m, m) if reverse else (m, c - m)
        qf = jnp.where(q_rows, q * jnp.exp(b - pltpu.roll(e, sh_q, 0)), 0.0)
        kf = jnp.where(q_rows, 0.0, k * jnp.exp(e - b))
        s_l = _dot_nt(qf.astype(BF16), kf.astype(BF16))
        scores = scores + jnp.where(x < 2 * m, s_l, 0.0)
        if 2 * m < c:
            e = jnp.where(q_rows, e, pltpu.roll(e, sh_e, 0))
        m *= 2
    o = _dot_nt((q * jnp.exp(b)).astype(BF16), st.astype(BF16))
    vb = v.astype(BF16)
    o = o + _dot(scores.astype(BF16), vb)
    bl = b[0:1, :] if reverse else b[c - 1:c, :]
    kd = (k * jnp.exp(bl - b)).astype(BF16)
    st_new = jnp.exp(bl) * st + _dot_tn(vb, kd)
    return o, st_new


def _scan_kernel(*refs, features, n_in, seq, with_s0, emit_state):
    in_refs = refs[:n_in]
    gate_ref, gain_ref = refs[n_in], refs[n_in + 1]
    pos = n_in + 2
    s0_ref = None
    if with_s0:
        s0_ref = refs[pos]
        pos += 1
    o_ref = refs[pos]
    pos += 1
    sf_ref = None
    if emit_state:
        sf_ref = refs[pos]
        pos += 1
    of_scr, st_scr = refs[pos], refs[pos + 1]
    c = SCAN_C
    n = seq // c

    def run(reverse):
        d = 1 if reverse else 0
        st_scr[...] = s0_ref[d] if with_s0 else jnp.zeros(st_scr.shape, F32)

        def body(i, carry):
            r0 = pl.multiple_of((n - 1 - i if reverse else i) * c, c)
            q, k, v, la = features(in_refs, r0, reverse)
            o, st = _chunk_scan(q, k, v, la, st_scr[...], reverse)
            st_scr[...] = st
            if reverse:
                o = o + of_scr[pl.ds(r0, c), :]
                y = _rms(o, gain_ref[...]) * _silu(gate_ref[pl.ds(r0, c), :])
                o_ref[pl.ds(r0, c), :] = y.astype(o_ref.dtype)
            else:
                of_scr[pl.ds(r0, c), :] = o
            return carry

        lax.fori_loop(0, n, body, 0)
        if emit_state:
            sf_ref[d] = st_scr[...].T

    run(False)
    run(True)


def _hgrn_features(in_refs, r0, reverse):
    hq_ref, hf_ref, hb_ref, hi_ref, lb_ref = in_refs
    rows = pl.ds(r0, SCAN_C)
    q = _silu(hq_ref[rows, :]) * (HG_DK ** -0.5)
    lb = lb_ref[1 if reverse else 0]
    f = lb + (1.0 - lb) * _sigmoid((hb_ref if reverse else hf_ref)[rows, :])
    return q, 1.0 - f, hi_ref[rows, :], jnp.log(f)


def _gla_features(in_refs, r0, reverse):
    gq_ref, gk_ref, gv_ref, low_ref, wgk_ref, bgk_ref = in_refs
    rows = pl.ds(r0, SCAN_C)
    d = 1 if reverse else 0
    g = _dot(low_ref[rows, :].astype(BF16), wgk_ref[d]) + bgk_ref[d]
    la = (jnp.minimum(g, 0.0) - jnp.log(1.0 + jnp.exp(-jnp.abs(g)))) * (1.0 / GLA_NORMALIZER)
    return gq_ref[rows, :] * (GLA_DK ** -0.5), gk_ref[rows, :], gv_ref[rows, :], la


def _seq_geom(latent):
    seq = DEC_SEQ if latent else SEQ
    return seq, (DEC_BATCH if latent else BATCH), (TP // seq if latent else 0)


def _zspec(latent, off, width, per_head=True):
    seq, _, blk0 = _seq_geom(latent)
    if per_head:
        return pl.BlockSpec((seq, width), lambda b, h, *_: (blk0 + b, off // width + h))
    return pl.BlockSpec((seq, width), lambda b, h, *_: (blk0 + b, off // width))


def _scan_call(name, features, feat_specs, feat_args, z, gate_off, gain, heads, dk, dv,
               latent, s0t, prev_out, prev_state, layer):
    seq, nb, row_blk0 = _seq_geom(latent)
    in_specs = list(feat_specs) + [_zspec(latent, gate_off, dv), pl.BlockSpec((1, dv), lambda b, h: (0, 0))]
    args = list(feat_args) + [z, gain.reshape(1, dv)]
    n_in = len(feat_specs)
    if latent:
        in_specs.append(pl.BlockSpec((None, None, 2, None, dv, dk), lambda b, h: (b, layer, 0, h, 0, 0)))
        args.append(s0t)
    out_specs = [pl.BlockSpec((seq, dv), lambda b, h: (row_blk0 + b, h))]
    out_shape = [jax.ShapeDtypeStruct((T, heads * dv), BF16)]
    aliases = {}
    if prev_out is not None:
        in_specs.append(pl.BlockSpec(memory_space=pl.ANY))
        aliases[len(args)] = 0
        args.append(prev_out)
    emit_state = not latent
    if emit_state:
        out_specs.append(pl.BlockSpec((None, None, 2, None, dk, dv), lambda b, h: (b, layer, 0, h, 0, 0)))
        out_shape.append(jax.ShapeDtypeStruct((BATCH, DEPTH, 2, heads, dk, dv), F32))
        if prev_state is not None:
            in_specs.append(pl.BlockSpec(memory_space=pl.ANY))
            aliases[len(args)] = 1
            args.append(prev_state)
    n_alias = len(aliases)

    def kern(*refs):
        n_inputs = len(args) - n_alias
        keep = refs[:n_inputs] + refs[n_inputs + n_alias:]
        _scan_kernel(*keep, features=features, n_in=n_in, seq=seq, with_s0=latent, emit_state=emit_state)

    res = pl.pallas_call(
        kern,
        grid=(nb, heads),
        in_specs=in_specs,
        out_specs=out_specs,
        out_shape=out_shape,
        input_output_aliases=aliases,
        scratch_shapes=[pltpu.VMEM((seq, dv), F32), pltpu.VMEM((dv, dk), F32)],
        compiler_params=_cparams(("parallel", "parallel")),
        name=name,
    )(*args)
    return (res[0], res[1]) if emit_state else (res[0], None)


def _hgrn(z, lb, gain, s0t, prev_state, layer):
    lb3 = lb.reshape(2, 1, HG_K)
    o = state = None
    for latent in (False, True):
        specs = [_zspec(latent, OFF_HQ, HG_DK), _zspec(latent, OFF_HFF, HG_DK), _zspec(latent, OFF_HFB, HG_DK),
                 _zspec(latent, OFF_HI, HG_DV), pl.BlockSpec((2, 1, HG_DK), lambda b, h: (0, 0, h))]
        o, st = _scan_call("hgrn_latent" if latent else "hgrn_prompt", _hgrn_features, specs, [z, z, z, z, lb3],
                           z, OFF_HG, gain, HG_HEADS, HG_DK, HG_DV, latent, s0t, o, prev_state, layer)
        state = st if st is not None else state
    return o, state


def _gla(z, w_gk, b_gk, gain, s0t, prev_state, layer):
    wpad = jnp.zeros((2, LANE, GLA_HEADS * GLA_DK), F32)
    wpad = wpad.at[0, 0:GLA_RANK].set(w_gk[0]).at[1, GLA_RANK:2 * GLA_RANK].set(w_gk[1]).astype(BF16)
    b3 = b_gk.reshape(2, 1, GLA_HEADS * GLA_DK)
    o = state = None
    for latent in (False, True):
        specs = [_zspec(latent, OFF_GQ, GLA_DK), _zspec(latent, OFF_GK, GLA_DK), _zspec(latent, OFF_GV, GLA_DV),
                 _zspec(latent, OFF_GLF, LANE, per_head=False),
                 pl.BlockSpec((2, LANE, GLA_DK), lambda b, h: (0, 0, h)),
                 pl.BlockSpec((2, 1, GLA_DK), lambda b, h: (0, 0, h))]
        o, st = _scan_call("gla_latent" if latent else "gla_prompt", _gla_features, specs, [z, z, z, z, wpad, b3],
                           z, OFF_GG, gain, GLA_HEADS, GLA_DK, GLA_DV, latent, s0t, o, prev_state, layer)
        state = st if st is not None else state
    return o, state


GQA = N_HEADS // N_KV_HEADS


def _ctx_attn_kernel(q_ref, k_ref, v_ref, sink_ref, o_ref):
    q = (q_ref[...] * (HEAD_DIM ** -0.5)).astype(BF16)
    s = _dot_nt(q, k_ref[...].astype(BF16))
    sink = sink_ref[:, 0:1]
    m = jnp.maximum(jnp.max(s, axis=-1, keepdims=True), sink)
    p = jnp.exp(s - m)
    den = jnp.sum(p, axis=-1, keepdims=True) + jnp.exp(sink - m)
    o_ref[...] = (_dot(p.astype(BF16), v_ref[...].astype(BF16)) / den).astype(o_ref.dtype)


def _ctx_attention(z, sink3):
    hd = HEAD_DIM
    return pl.pallas_call(
        _ctx_attn_kernel,
        grid=(BATCH, N_HEADS),
        in_specs=[_zspec(False, OFF_AQ, hd),
                  pl.BlockSpec((SEQ, hd), lambda b, h: (b, OFF_AK // hd + h // GQA)),
                  pl.BlockSpec((SEQ, hd), lambda b, h: (b, OFF_AV // hd + h // GQA)),
                  pl.BlockSpec((None, 1, LANE), lambda b, h: (h, 0, 0))],
        out_specs=pl.BlockSpec((SEQ, hd), lambda b, h: (b, h)),
        out_shape=jax.ShapeDtypeStruct((T, N_HEADS * hd), BF16),
        compiler_params=_cparams(("parallel", "parallel")),
        name="ctx_attention",
    )(z, z, z, sink3)


def _rope(x, cos_f, sin_f):
    return x * cos_f + pltpu.roll(x, HEAD_DIM // 2, 1) * sin_f


def _lat_attn_kernel(q_ref, k_ref, v_ref, ck_ref, cv_ref, cos_ref, sin_ref, sink_ref, o_ref, kr_scr, vr_scr):
    blk, hd, seq = ATT_BLOCK, HEAD_DIM, DEC_SEQ
    nb = seq // blk

    @pl.when(pl.program_id(2) == 0)
    def _():
        zeros = jnp.zeros((blk, hd), BF16)
        for scr in (kr_scr, vr_scr):
            scr[0:blk, :] = zeros
            scr[blk + seq:2 * blk + seq, :] = zeros

        def fill(n, carry):
            r0 = pl.multiple_of(n * blk, blk)
            rows = pl.ds(r0, blk)
            kr_scr[pl.ds(r0 + blk, blk), :] = _rope(k_ref[rows, :], cos_ref[rows, :], sin_ref[rows, :]).astype(BF16)
            vr_scr[pl.ds(r0 + blk, blk), :] = v_ref[rows, :].astype(BF16)
            return carry

        lax.fori_loop(0, nb, fill, 0)

    ck = ck_ref[...].astype(BF16)
    cv = cv_ref[...].astype(BF16)
    sink = sink_ref[:, 0:1]
    qi = lax.broadcasted_iota(jnp.int32, (blk, 3 * blk), 0)
    kj = lax.broadcasted_iota(jnp.int32, (blk, 3 * blk), 1)
    window_bias = jnp.where(jnp.abs(kj - qi - blk) <= WINDOW, 0.0, -jnp.inf)

    def qblock(n, carry):
        r0 = pl.multiple_of(n * blk, blk)
        rows = pl.ds(r0, blk)
        q = (_rope(q_ref[rows, :], cos_ref[rows, :], sin_ref[rows, :]) * (hd ** -0.5)).astype(BF16)
        band = pl.ds(r0, 3 * blk)
        key_pos = kj + (n - 1) * blk
        s_loc = _dot_nt(q, kr_scr[band, :]) + window_bias
        s_loc = jnp.where(key_pos >= 0, s_loc, -jnp.inf)
        s_loc = jnp.where(key_pos < seq, s_loc, -jnp.inf)
        s_ctx = _dot_nt(q, ck)
        m = jnp.maximum(jnp.maximum(jnp.max(s_loc, axis=-1, keepdims=True),
                                    jnp.max(s_ctx, axis=-1, keepdims=True)), sink)
        p_loc = jnp.exp(s_loc - m)
        p_ctx = jnp.exp(s_ctx - m)
        den = (jnp.sum(p_loc, axis=-1, keepdims=True) + jnp.sum(p_ctx, axis=-1, keepdims=True)
               + jnp.exp(sink - m))
        o = _dot(p_ctx.astype(BF16), cv) + _dot(p_loc.astype(BF16), vr_scr[band, :])
        o_ref[rows, :] = (o / den).astype(o_ref.dtype)
        return carry

    lax.fori_loop(0, nb, qblock, 0)


def _rope_tables():
    n = jnp.arange(DEC_SEQ)
    row = (n // GRID_W).astype(F32)
    col = (n % GRID_W).astype(F32)
    n_freq = HEAD_DIM // 4
    inv = ROPE_THETA ** (-jnp.arange(n_freq, dtype=F32) / n_freq)
    ang = jnp.concatenate([row[:, None] * inv, col[:, None] * inv], axis=-1)
    cos, sin = jnp.cos(ang), jnp.sin(ang)
    return jnp.concatenate([cos, cos], axis=-1), jnp.concatenate([-sin, sin], axis=-1)


def _lat_attention(z, ck, cv, sink3, prev_out):
    hd = HEAD_DIM
    seq, _, blk0 = _seq_geom(True)
    cos_f, sin_f = _rope_tables()
    tab = pl.BlockSpec((seq, hd), lambda b, kh, g: (0, 0))
    cache = pl.BlockSpec((None, PAST_LEN, hd), lambda b, kh, g: (b, 0, kh))
    return pl.pallas_call(
        lambda *refs: _lat_attn_kernel(*refs[:8], *refs[9:]),
        grid=(DEC_BATCH, N_KV_HEADS, GQA),
        in_specs=[pl.BlockSpec((seq, hd), lambda b, kh, g: (blk0 + b, OFF_AQ // hd + kh * GQA + g)),
                  pl.BlockSpec((seq, hd), lambda b, kh, g: (blk0 + b, OFF_AK // hd + kh)),
                  pl.BlockSpec((seq, hd), lambda b, kh, g: (blk0 + b, OFF_AV // hd + kh)),
                  cache, cache, tab, tab,
                  pl.BlockSpec((None, 1, LANE), lambda b, kh, g: (kh * GQA + g, 0, 0)),
                  pl.BlockSpec(memory_space=pl.ANY)],
        out_specs=pl.BlockSpec((seq, hd), lambda b, kh, g: (blk0 + b, kh * GQA + g)),
        out_shape=jax.ShapeDtypeStruct((T, N_HEADS * hd), BF16),
        input_output_aliases={8: 0},
        scratch_shapes=[pltpu.VMEM((seq + 2 * ATT_BLOCK, hd), BF16), pltpu.VMEM((seq + 2 * ATT_BLOCK, hd), BF16)],
        compiler_params=_cparams(("parallel", "parallel", "arbitrary")),
        name="lat_attention",
    )(z, z, z, ck, cv, cos_f, sin_f, sink3, prev_out)


def _pad_cols(w, n):
    return jnp.pad(w, ((0, 0), (0, n - w.shape[1])))


def kernel(x_prompt, x_sample, cache_k, cache_v, state_hgrn, state_gla, c, c_ctx, w_ada, b_ada, norm_g, w_in,
           w_out, sgu_ln_g, sgu_ln_b, sgu_w, sgu_b, hgrn_lb_logits, hgrn_norm_g, attn_sink, gla_w_gk, gla_b_gk,
           gla_norm_g, ffn_w_up, ffn_conv_w, ffn_conv_b, ffn_w_down):
    assert TP % DEC_SEQ == 0 and DEC_BATCH + 1 <= MOD_ROWS
    lb_all = jnp.cumsum(jax.nn.softmax(hgrn_lb_logits.astype(F32), axis=0), axis=0)
    lb_all = lb_all - lb_all[0:1]

    cvec = jnp.zeros((MOD_ROWS, D_MODEL), F32).at[0].set(c_ctx).at[1:1 + DEC_BATCH].set(c)
    mod = _ada_mod(cvec, w_ada, b_ada).reshape(DEPTH, MOD_ROWS, 1, 6 * D_MODEL)
    SH1, SC1, G1, SH2, SC2, G2 = range(6)

    x = jnp.concatenate([x_prompt.reshape(TP, D_MODEL), x_sample.reshape(TL, D_MODEL)], axis=0)
    hg_s0t = jnp.swapaxes(state_hgrn, -1, -2)
    gl_s0t = jnp.swapaxes(state_gla, -1, -2)
    ck_all = cache_k.reshape(DEC_BATCH, DEPTH, PAST_LEN, N_KV_HEADS * HEAD_DIM)
    cv_all = cache_v.reshape(DEC_BATCH, DEPTH, PAST_LEN, N_KV_HEADS * HEAD_DIM)

    h = _norm_mod(x, norm_g[0, 0], mod, 0, SC1, SH1)
    ks_new, vs_new = [], []
    hg_state = gl_state = None
    for l in range(DEPTH):
        w_in_b = _pad_cols(w_in[l], Z_COLS).astype(BF16)
        w_out_b = w_out[l].astype(BF16)
        w_up = ffn_w_up[l]
        w_up_b = jnp.concatenate([_pad_cols(w_up[:, :D_FF], D_FF_PAD), _pad_cols(w_up[:, D_FF:], D_FF_PAD)],
                                 axis=1).astype(BF16)
        w_down_b = jnp.pad(ffn_w_down[l], ((0, D_FF_PAD - D_FF), (0, 0))).astype(BF16)
        conv_w = jnp.pad(ffn_conv_w[l], ((0, 8 - CONV_W), (0, D_FF_PAD - D_FF)))
        conv_b = _pad_cols(ffn_conv_b[l].reshape(1, D_FF), D_FF_PAD)
        sink3 = jnp.broadcast_to(attn_sink[l].astype(F32)[:, None, None], (N_HEADS, 1, LANE))

        z = _matmul(h, w_in_b, F32, Z_TILE_N, "in_proj")
        o_sgu = _sgu(z, sgu_ln_g[l], sgu_ln_b[l], sgu_w[l], sgu_b[l])
        o_hg, hg_state = _hgrn(z, lb_all[l], hgrn_norm_g[l], hg_s0t, hg_state, l)
        o_att = _ctx_attention(z, sink3)
        o_att = _lat_attention(z, ck_all[:, l], cv_all[:, l], sink3, o_att)
        o_gl, gl_state = _gla(z, gla_w_gk[l], gla_b_gk[l], gla_norm_g[l], gl_s0t, gl_state, l)
        m = _out_proj([o_sgu, o_hg, o_att, o_gl], w_out_b)
        x, h = _resid(x, m, norm_g[l, 1], mod, l, G1, nxt=(norm_g[l, 2], l, SC2, SH2))
        ks_new.append(z[:TP, OFF_AK:OFF_AK + N_KV_HEADS * HEAD_DIM].reshape(BATCH, SEQ, N_KV_HEADS, HEAD_DIM))
        vs_new.append(z[:TP, OFF_AV:OFF_AV + N_KV_HEADS * HEAD_DIM].reshape(BATCH, SEQ, N_KV_HEADS, HEAD_DIM))

        act = _ffn_up(h, _conv_halo(h), w_up_b, conv_w, conv_b)
        f = _ffn_down(act, w_down_b)
        nxt = (norm_g[l + 1, 0], l + 1, SC1, SH1) if l + 1 < DEPTH else None
        x, h = _resid(x, f, norm_g[l, 3], mod, l, G2, nxt=nxt)

    y_prompt = x[:TP].reshape(BATCH, SEQ, D_MODEL)
    y_sample = x[TP:].reshape(DEC_BATCH, DEC_SEQ, D_MODEL)
    return (y_prompt, y_sample, jnp.stack(ks_new, axis=1), jnp.stack(vs_new, axis=1), hg_state, gl_state)
```

```python
import functools

import numpy as np
import jax
import jax.numpy as jnp
from jax import lax
from jax.experimental import pallas as pl
from jax.experimental.pallas import tpu as pltpu

D_MODEL = 4096
BATCH = 32
SEQ = 256
DEPTH = 2
DEC_BATCH = 2
DEC_SEQ = 4096
PAST_LEN = 512
GRID_W = 64
GROUP_W = D_MODEL // 4
SGU_CHUNK = 128
SGU_GROUPS = 4
SGU_DIM = GROUP_W
SGU_GCH = SGU_DIM // SGU_GROUPS
HG_HEADS = 8
HG_DK = 128
HG_DV = GROUP_W // HG_HEADS
HG_K = HG_HEADS * HG_DK
N_HEADS = 8
N_KV_HEADS = 2
HEAD_DIM = GROUP_W // N_HEADS
WINDOW = 128
ATT_BLOCK = 128
ROPE_THETA = 10000.0
GLA_HEADS = 4
GLA_DK = 128
GLA_DV = GROUP_W // GLA_HEADS
GLA_RANK = 16
GLA_NORMALIZER = 16.0
D_FF = 11008
CONV_W = 3
EPS = 1e-6

F32 = jnp.float32
BF16 = jnp.bfloat16

IN_SIZES = (2 * SGU_DIM,
            HG_K, HG_K, HG_K, HG_HEADS * HG_DV, HG_HEADS * HG_DV,
            N_HEADS * HEAD_DIM, N_KV_HEADS * HEAD_DIM, N_KV_HEADS * HEAD_DIM,
            GLA_HEADS * GLA_DK, GLA_HEADS * GLA_DK, GLA_HEADS * GLA_DV, GLA_HEADS * GLA_DV,
            GLA_RANK, GLA_RANK)
D_IN = sum(IN_SIZES)
_OFF = [0] + [int(c) for c in np.cumsum(IN_SIZES)]
(OFF_SGU, OFF_HQ, OFF_HFF, OFF_HFB, OFF_HI, OFF_HG, OFF_AQ, OFF_AK, OFF_AV,
 OFF_GQ, OFF_GK, OFF_GV, OFF_GG, OFF_GLF, OFF_GLB) = _OFF[:-1]

LANE = 128
SUBLANE = 8
VMEM_LIMIT = 56 * 1024 * 1024

Z_TILE_N = 1024
Z_COLS = -(-D_IN // Z_TILE_N) * Z_TILE_N
FF_TILE_N = 512
D_FF_PAD = -(-D_FF // (2 * FF_TILE_N)) * (2 * FF_TILE_N)
SCAN_C = 128
MOD_ROWS = 16

TP = BATCH * SEQ
TL = DEC_BATCH * DEC_SEQ
T = TP + TL
PIECES = ((0, TP), (TP, TL))


def _cparams(sem):
    return pltpu.CompilerParams(dimension_semantics=sem, vmem_limit_bytes=VMEM_LIMIT)


def _row_group(row0):
    return jnp.where(row0 < TP, 0, 1 + (row0 - TP) // DEC_SEQ)


def _dot(a, b):
    return jnp.dot(a, b, preferred_element_type=F32)


def _dot_nt(a, b):
    return lax.dot_general(a, b, (((1,), (1,)), ((), ())), preferred_element_type=F32)


def _dot_tn(a, b):
    return lax.dot_general(a, b, (((0,), (0,)), ((), ())), preferred_element_type=F32)


def _sigmoid(x):
    return 1.0 / (1.0 + jnp.exp(-x))


def _silu(x):
    return x * _sigmoid(x)


def _gelu(x):
    return 0.5 * x * (1.0 + lax.erf(x * (2.0 ** -0.5)))


def _rms(x, g):
    return x * lax.rsqrt(jnp.mean(x * x, axis=-1, keepdims=True) + EPS) * g


def _ada_kernel(c_ref, w_ref, b_ref, o_ref):
    c = _silu(c_ref[...]).astype(BF16)
    o_ref[...] = _dot(c, w_ref[...].astype(BF16)) + b_ref[...]


def _ada_mod(cvec, w_ada, b_ada):
    tn = 512
    n = w_ada.shape[-1]
    return pl.pallas_call(
        _ada_kernel,
        grid=(DEPTH, n // tn),
        in_specs=[pl.BlockSpec((MOD_ROWS, D_MODEL), lambda l, j: (0, 0)),
                  pl.BlockSpec((None, D_MODEL, tn), lambda l, j: (l, 0, j)),
                  pl.BlockSpec((None, 1, tn), lambda l, j: (l, 0, j))],
        out_specs=pl.BlockSpec((None, MOD_ROWS, tn), lambda l, j: (l, 0, j)),
        out_shape=jax.ShapeDtypeStruct((DEPTH, MOD_ROWS, n), F32),
        compiler_params=_cparams(("parallel", "parallel")),
        name="ada_mod",
    )(cvec, w_ada, b_ada.reshape(DEPTH, 1, n))


def _mod_spec(layer, chunk, tr, blk0=0):
    return pl.BlockSpec((None, None, 1, D_MODEL),
                        lambda i, *_: (layer, _row_group((blk0 + i) * tr), 0, chunk))


NORM_ROWS = 256
_VEC_SPEC = pl.BlockSpec((1, D_MODEL), lambda i: (0, 0))


def _row_spec(blk0=0):
    return pl.BlockSpec((NORM_ROWS, D_MODEL), lambda i: (blk0 + i, 0))


def _norm_mod_kernel(x_ref, g_ref, sc_ref, sh_ref, h_ref):
    y = _rms(x_ref[...], g_ref[...])
    h_ref[...] = (y * (1.0 + sc_ref[...]) + sh_ref[...]).astype(h_ref.dtype)


def _norm_mod(x_parts, g, mod, layer, sc_chunk, sh_chunk):
    tr = NORM_ROWS
    h = None
    for x, (row0, rows) in zip(x_parts, PIECES):
        blk0 = row0 // tr
        in_specs = [_row_spec(), _VEC_SPEC, _mod_spec(layer, sc_chunk, tr, blk0), _mod_spec(layer, sh_chunk, tr, blk0)]
        args = [x, g.reshape(1, D_MODEL), mod, mod]
        aliases = {}
        if h is not None:
            in_specs.append(pl.BlockSpec(memory_space=pl.ANY))
            aliases[len(args)] = 0
            args.append(h)
        h = pl.pallas_call(
            lambda x_ref, g_ref, sc_ref, sh_ref, *rest: _norm_mod_kernel(x_ref, g_ref, sc_ref, sh_ref, rest[-1]),
            grid=(rows // tr,),
            in_specs=in_specs,
            out_specs=_row_spec(blk0),
            out_shape=jax.ShapeDtypeStruct((T, D_MODEL), BF16),
            input_output_aliases=aliases,
            compiler_params=_cparams(("parallel",)),
            name="norm_mod",
        )(*args)
    return h


def _resid_kernel(x_ref, m_ref, ga_ref, gate_ref, *rest, emit_h):
    x = x_ref[...] + gate_ref[...] * _rms(m_ref[...].astype(F32), ga_ref[...])
    if emit_h:
        gb_ref, sc_ref, sh_ref, xo_ref, h_ref = rest
        xo_ref[...] = x
        y = _rms(x, gb_ref[...])
        h_ref[...] = (y * (1.0 + sc_ref[...]) + sh_ref[...]).astype(h_ref.dtype)
    else:
        (xo_ref,) = rest
        xo_ref[...] = x


def _resid(x, m, ga, mod, layer, gate_chunk, nxt=None, split_out=False):
    tr = NORM_ROWS
    x_parts = x if isinstance(x, (list, tuple)) else None
    pieces = PIECES if (x_parts is not None or split_out) else ((0, T),)
    emit_h = nxt is not None
    assert not (emit_h and split_out)
    x_new, h, outs = None, None, []
    for p, (row0, rows) in enumerate(pieces):
        blk0 = row0 // tr
        in_specs = [_row_spec(0 if x_parts is not None else blk0), _row_spec(blk0), _VEC_SPEC,
                    _mod_spec(layer, gate_chunk, tr, blk0)]
        args = [x_parts[p] if x_parts is not None else x, m, ga.reshape(1, D_MODEL), mod]
        if emit_h:
            gb, layer_b, sc_chunk, sh_chunk = nxt
            in_specs += [_VEC_SPEC, _mod_spec(layer_b, sc_chunk, tr, blk0), _mod_spec(layer_b, sh_chunk, tr, blk0)]
            args += [gb.reshape(1, D_MODEL), mod, mod]
        n_in = len(args)
        if split_out:
            out_specs = [_row_spec()]
            out_shape = [jax.ShapeDtypeStruct((rows, D_MODEL), F32)]
        else:
            out_specs = [_row_spec(blk0)]
            out_shape = [jax.ShapeDtypeStruct((T, D_MODEL), F32)]
        if emit_h:
            out_specs.append(_row_spec(blk0))
            out_shape.append(jax.ShapeDtypeStruct((T, D_MODEL), BF16))
        aliases = {}
        if x_new is not None and not split_out:
            for k, prev in enumerate((x_new, h) if emit_h else (x_new,)):
                in_specs.append(pl.BlockSpec(memory_space=pl.ANY))
                aliases[len(args)] = k
                args.append(prev)
        n_alias = len(aliases)

        def kern(*refs, n_in=n_in, n_alias=n_alias):
            _resid_kernel(*refs[:n_in], *refs[n_in + n_alias:], emit_h=emit_h)

        res = pl.pallas_call(
            kern,
            grid=(rows // tr,),
            in_specs=in_specs,
            out_specs=out_specs,
            out_shape=out_shape,
            input_output_aliases=aliases,
            compiler_params=_cparams(("parallel",)),
            name="resid_norm",
        )(*args)
        x_new = res[0]
        h = res[1] if emit_h else None
        outs.append(res[0])
    return (outs if split_out else x_new), h


MM_TILE_M = 1024


def _mm_kernel(a_ref, b_ref, o_ref):
    o_ref[...] = _dot(a_ref[...], b_ref[...]).astype(o_ref.dtype)


def _matmul(a, b, out_dtype, tn, name):
    m, k = a.shape
    n = b.shape[1]
    tm = MM_TILE_M
    return pl.pallas_call(
        _mm_kernel,
        grid=(m // tm, n // tn),
        in_specs=[pl.BlockSpec((tm, k), lambda i, j: (i, 0)),
                  pl.BlockSpec((k, tn), lambda i, j: (0, j))],
        out_specs=pl.BlockSpec((tm, tn), lambda i, j: (i, j)),
        out_shape=jax.ShapeDtypeStruct((m, n), out_dtype),
        compiler_params=_cparams(("parallel", "parallel")),
        name=name,
    )(a, b)


def _mm4_kernel(a0_ref, a1_ref, a2_ref, a3_ref, b_ref, o_ref):
    acc = _dot(a0_ref[...], b_ref[0 * GROUP_W:1 * GROUP_W, :])
    acc += _dot(a1_ref[...], b_ref[1 * GROUP_W:2 * GROUP_W, :])
    acc += _dot(a2_ref[...], b_ref[2 * GROUP_W:3 * GROUP_W, :])
    acc += _dot(a3_ref[...], b_ref[3 * GROUP_W:4 * GROUP_W, :])
    o_ref[...] = acc.astype(o_ref.dtype)


def _out_proj(parts, w):
    tm, tn = MM_TILE_M, 1024
    a_spec = pl.BlockSpec((tm, GROUP_W), lambda i, j: (i, 0))
    return pl.pallas_call(
        _mm4_kernel,
        grid=(T // tm, D_MODEL // tn),
        in_specs=[a_spec, a_spec, a_spec, a_spec,
                  pl.BlockSpec((4 * GROUP_W, tn), lambda i, j: (0, j))],
        out_specs=pl.BlockSpec((tm, tn), lambda i, j: (i, j)),
        out_shape=jax.ShapeDtypeStruct((T, D_MODEL), F32),
        compiler_params=_cparams(("parallel", "parallel")),
        name="out_proj",
    )(*parts, w)


def _mm_ktiled_kernel(a_ref, b_ref, o_ref):
    k = pl.program_id(2)
    d = _dot(a_ref[...], b_ref[...])

    @pl.when(k == 0)
    def _():
        o_ref[...] = d

    @pl.when(k != 0)
    def _():
        o_ref[...] += d


def _ffn_down(a, b):
    m, kk = a.shape
    n = b.shape[1]
    tm, tn, tk = MM_TILE_M, 512, kk // 2
    return pl.pallas_call(
        _mm_ktiled_kernel,
        grid=(m // tm, n // tn, kk // tk),
        in_specs=[pl.BlockSpec((tm, tk), lambda i, j, k: (i, k)),
                  pl.BlockSpec((tk, tn), lambda i, j, k: (k, j))],
        out_specs=pl.BlockSpec((tm, tn), lambda i, j, k: (i, j)),
        out_shape=jax.ShapeDtypeStruct((m, n), F32),
        compiler_params=_cparams(("parallel", "parallel", "arbitrary")),
        name="ffn_down",
    )(a, b)


FFN_TILE_M = 1024
FFN_SUB_N = 256
HALO_ROWS = 16


def _ffn_up_kernel(h_ref, halo_ref, wg_ref, wv_ref, cw_ref, cb_ref, o_ref):
    tm, tn = o_ref.shape
    h = h_ref[...]
    row0 = pl.program_id(0) * tm
    r = lax.broadcasted_iota(jnp.int32, (tm, FFN_SUB_N), 0)
    seq_len = jnp.where(row0 < TP, SEQ, DEC_SEQ)
    pos = (row0 + r) & (seq_len - 1)
    first, last = r == 0, r == tm - 1
    seq_first, seq_last = pos == 0, pos == seq_len - 1
    for s in range(tn // FFN_SUB_N):
        cols = slice(s * FFN_SUB_N, (s + 1) * FFN_SUB_N)
        g = _dot(h, wg_ref[:, cols])
        gh = _dot(halo_ref[...], wg_ref[:, cols])
        val = _dot(h, wv_ref[:, cols])
        prev = jnp.where(first, gh[0:1, :], pltpu.roll(g, 1, 0))
        prev = jnp.where(seq_first, 0.0, prev)
        nxt = jnp.where(last, gh[1:2, :], pltpu.roll(g, tm - 1, 0))
        nxt = jnp.where(seq_last, 0.0, nxt)
        conv = prev * cw_ref[0:1, cols] + g * cw_ref[1:2, cols] + nxt * cw_ref[2:3, cols] + cb_ref[:, cols]
        o_ref[:, cols] = (_gelu(conv) * val).astype(o_ref.dtype)


def _ffn_up(h, halo, w_up, conv_w, conv_b):
    tm, tn = FFN_TILE_M, FF_TILE_N
    nj = D_FF_PAD // tn
    return pl.pallas_call(
        _ffn_up_kernel,
        grid=(T // tm, nj),
        in_specs=[pl.BlockSpec((tm, D_MODEL), lambda i, j: (i, 0)),
                  pl.BlockSpec((None, HALO_ROWS, D_MODEL), lambda i, j: (i, 0, 0)),
                  pl.BlockSpec((D_MODEL, tn), lambda i, j: (0, j)),
                  pl.BlockSpec((D_MODEL, tn), lambda i, j: (0, nj + j)),
                  pl.BlockSpec((8, tn), lambda i, j: (0, j)),
                  pl.BlockSpec((1, tn), lambda i, j: (0, j))],
        out_specs=pl.BlockSpec((tm, tn), lambda i, j: (i, j)),
        out_shape=jax.ShapeDtypeStruct((T, D_FF_PAD), BF16),
        compiler_params=_cparams(("parallel", "parallel")),
        name="ffn_up",
    )(h, halo, w_up, w_up, conv_w, conv_b)


def _conv_halo(h):
    tm = FFN_TILE_M
    nt = T // tm
    ht = h.reshape(nt, tm, D_MODEL)
    zero = jnp.zeros((1, D_MODEL), h.dtype)
    before = jnp.concatenate([zero, ht[:-1, tm - 1]], axis=0)
    after = jnp.concatenate([ht[1:, 0], zero], axis=0)
    pad = jnp.zeros((nt, HALO_ROWS - 2, D_MODEL), h.dtype)
    return jnp.concatenate([before[:, None], after[:, None], pad], axis=1)


def _sgu_kernel(u_ref, v_ref, g_ref, b_ref, w_ref, bias_ref, o_ref):
    u = _gelu(u_ref[...])
    v = _gelu(v_ref[...])
    vc = v - jnp.mean(v, axis=-1, keepdims=True)
    vn = vc * lax.rsqrt(jnp.mean(vc * vc, axis=-1, keepdims=True) + EPS) * g_ref[...] + b_ref[...]
    vb = vn.astype(BF16)
    for g in range(SGU_GROUPS):
        sl = slice(g * SGU_GCH, (g + 1) * SGU_GCH)
        mixed = _dot(w_ref[g], vb[:, sl]) + bias_ref[:, sl]
        o_ref[:, sl] = (u[:, sl] * mixed).astype(o_ref.dtype)


def _sgu(z, ln_g, ln_b, w_s, b_s):
    c = SGU_CHUNK
    bias = jnp.repeat(b_s.T, SGU_GCH, axis=1)
    vec = pl.BlockSpec((1, SGU_DIM), lambda i: (0, 0))
    return pl.pallas_call(
        _sgu_kernel,
        grid=(T // c,),
        in_specs=[pl.BlockSpec((c, SGU_DIM), lambda i: (i, OFF_SGU // SGU_DIM)),
                  pl.BlockSpec((c, SGU_DIM), lambda i: (i, OFF_SGU // SGU_DIM + 1)),
                  vec, vec,
                  pl.BlockSpec((SGU_GROUPS, c, c), lambda i: (0, 0, 0)),
                  pl.BlockSpec((c, SGU_DIM), lambda i: (0, 0))],
        out_specs=pl.BlockSpec((c, SGU_DIM), lambda i: (i, 0)),
        out_shape=jax.ShapeDtypeStruct((T, SGU_DIM), BF16),
        compiler_params=_cparams(("parallel",)),
        name="sgu",
    )(z, z, ln_g.reshape(1, SGU_DIM), ln_b.reshape(1, SGU_DIM), w_s.astype(BF16), bias)


SMALL_LEVELS = tuple(m for m in (1, 2, 4) if m < SUBLANE)
BIG_LEVELS = tuple(m for m in (8, 16, 32, 64) if m < SCAN_C)


def _scan_consts():
    c = SCAN_C
    t = np.arange(c)[:, None]
    s = np.arange(c)[None, :]
    x = t ^ s
    tri = np.stack([s <= t, s >= t]).astype(np.float32)
    pair = [np.stack([x == 0, x == 0])]
    for m in SMALL_LEVELS:
        lvl = (x >= m) & (x < 2 * m)
        pair.append(np.stack([lvl & (t > s), lvl & (t < s)]))
    pair = np.stack(pair).astype(np.float32)
    rowq = np.stack([np.broadcast_to(((np.arange(c) & m) != 0)[:, None], (c, LANE)) for m in SMALL_LEVELS])
    hb = np.arange(c // 2)
    blk = np.stack([(hb[:, None] // m) == (hb[None, :] // m) for m in BIG_LEVELS])
    return (jnp.asarray(tri, BF16), jnp.asarray(pair, F32), jnp.asarray(rowq.astype(np.float32), F32),
            jnp.asarray(blk.astype(np.float32), F32))


def _split3(x):
    hi = x.astype(BF16)
    r = x - hi.astype(F32)
    mid = r.astype(BF16)
    lo = (r - mid.astype(F32)).astype(BF16)
    return hi, mid, lo


def _halves(a, m, second):
    off = m if second else 0
    return jnp.concatenate([a[j + off:j + off + m] for j in range(0, a.shape[0], 2 * m)], axis=0)


def _chunk_scans(probs, consts):
    tri_ref, pair_ref, rowq_ref, blk_ref = consts
    c, dk = probs[0][0].shape
    dirs = [1 if p[5] else 0 for p in probs]
    vbs = [p[2].astype(BF16) for p in probs]
    diag = [pair_ref[0, d] * _dot_nt(p[0].astype(BF16), p[1].astype(BF16)) for p, d in zip(probs, dirs)]
    bs = []
    for p, d in zip(probs, dirs):
        hi, mid, lo = _split3(p[3])
        tri = tri_ref[d]
        bs.append(_dot(tri, hi) + _dot(tri, mid) + _dot(tri, lo))
    inter = [_dot_nt((p[0] * jnp.exp(b)).astype(BF16), p[4].astype(BF16)) for p, b in zip(probs, bs)]

    big = [[] for _ in probs]
    for li, m in enumerate(BIG_LEVELS):
        for pi, ((q, k, _, _, _, reverse), b) in enumerate(zip(probs, bs)):
            q_second = not reverse
            refs = [b[j + m:j + m + 1] if reverse else b[j + m - 1:j + m] for j in range(0, c, 2 * m)]
            ref = jnp.concatenate([jnp.broadcast_to(r, (m, dk)) for r in refs], axis=0)
            qf = _halves(q, m, q_second) * jnp.exp(_halves(b, m, q_second) - ref)
            kf = _halves(k, m, not q_second) * jnp.exp(ref - _halves(b, m, not q_second))
            s_l = _dot_nt(qf.astype(BF16), kf.astype(BF16))
            if 2 * m < c:
                s_l = s_l * blk_ref[li]
            big[pi].append(s_l.astype(BF16))

    scores = diag
    es = list(bs)
    for li, m in enumerate(SMALL_LEVELS):
        for pi, ((q, k, _, _, _, reverse), b) in enumerate(zip(probs, bs)):
            is_q = (rowq_ref[li] < 0.5) if reverse else (rowq_ref[li] > 0.5)
            sh_q, sh_e = (c - m, m) if reverse else (m, c - m)
            fac = jnp.exp(jnp.where(is_q, b - pltpu.roll(es[pi], sh_q, 0), es[pi] - b))
            u = (jnp.where(is_q, q, k) * fac).astype(BF16)
            scores[pi] = scores[pi] + pair_ref[1 + li, dirs[pi]] * _dot_nt(u, u)
            if li + 1 < len(SMALL_LEVELS):
                es[pi] = jnp.where(is_q, es[pi], pltpu.roll(es[pi], sh_e, 0))

    outs = []
    for pi, ((q, k, _, _, st, reverse), b) in enumerate(zip(probs, bs)):
        o = inter[pi] + _dot(scores[pi].astype(BF16), vbs[pi])
        parts = [o[g:g + SUBLANE] for g in range(0, c, SUBLANE)]
        q_second = not reverse
        for li, m in enumerate(BIG_LEVELS):
            oc = _dot(big[pi][li], _halves(vbs[pi], m, not q_second))
            for jj, j in enumerate(range(0, c, 2 * m)):
                for g in range(0, m, SUBLANE):
                    dst = (j + (m if q_second else 0) + g) // SUBLANE
                    parts[dst] = parts[dst] + oc[jj * m + g:jj * m + g + SUBLANE]
        bl = b[0:1, :] if reverse else b[c - 1:c, :]
        kd = (k * jnp.exp(bl - b)).astype(BF16)
        st_new = jnp.exp(bl) * st + _dot_tn(vbs[pi], kd)
        outs.append((jnp.concatenate(parts, axis=0), st_new))
    return outs


def _scan_kernel(*refs, features, n_in, seq, hpb, dv, with_s0, emit_state):
    consts = refs[:4]
    in_refs = refs[4:4 + n_in]
    gate_ref, gain_ref = refs[4 + n_in], refs[5 + n_in]
    pos = 6 + n_in
    s0_ref = None
    if with_s0:
        s0_ref = refs[pos]
        pos += 1
    o_ref = refs[pos]
    pos += 1
    sf_ref = None
    if emit_state:
        sf_ref = refs[pos]
        pos += 1
    o_scr, st_scr = refs[pos:pos + 2]
    c = SCAN_C
    n = seq // c
    half = n // 2
    st_scr[...] = s0_ref[...] if with_s0 else jnp.zeros(st_scr.shape, F32)
    work = [(hh, reverse) for hh in range(hpb) for reverse in (False, True)]

    def step(i, finalize):
        r0s = {False: pl.multiple_of(i * c, c), True: pl.multiple_of((n - 1 - i) * c, c)}
        probs = [features(in_refs, r0s[rev], hh, rev) + (st_scr[1 if rev else 0, hh], rev) for hh, rev in work]
        for (hh, rev), (o, st) in zip(work, _chunk_scans(probs, consts)):
            st_scr[1 if rev else 0, hh] = st
            rows, cols = pl.ds(r0s[rev], c), slice(hh * dv, (hh + 1) * dv)
            if finalize:
                y = _rms(o + o_scr[rows, cols], gain_ref[...]) * _silu(gate_ref[rows, cols])
                o_ref[rows, cols] = y.astype(o_ref.dtype)
            else:
                o_scr[rows, cols] = o

    lax.fori_loop(0, half, lambda i, carry: (step(i, False), carry)[1], 0)
    lax.fori_loop(half, n, lambda i, carry: (step(i, True), carry)[1], 0)
    if emit_state:
        for d in range(2):
            for hh in range(hpb):
                sf_ref[d, hh] = st_scr[d, hh].T


def _hgrn_features(in_refs, r0, head, reverse):
    hq_ref, hf_ref, hb_ref, hi_ref, lb_ref = in_refs
    rows, cols = pl.ds(r0, SCAN_C), slice(head * HG_DK, (head + 1) * HG_DK)
    q = _silu(hq_ref[rows, cols]) * (HG_DK ** -0.5)
    lb = lb_ref[1 if reverse else 0][:, cols]
    f = lb + (1.0 - lb) * _sigmoid((hb_ref if reverse else hf_ref)[rows, cols])
    return q, 1.0 - f, hi_ref[rows, head * HG_DV:(head + 1) * HG_DV], jnp.log(f)


def _gla_features(in_refs, r0, head, reverse):
    gq_ref, gk_ref, gv_ref, low_ref, wgk_ref, bgk_ref = in_refs
    rows, cols = pl.ds(r0, SCAN_C), slice(head * GLA_DK, (head + 1) * GLA_DK)
    d = 1 if reverse else 0
    g = _dot(low_ref[rows, :].astype(BF16), wgk_ref[d][:, cols]) + bgk_ref[d][:, cols]
    la = (jnp.minimum(g, 0.0) - jnp.log(1.0 + jnp.exp(-jnp.abs(g)))) * (1.0 / GLA_NORMALIZER)
    return (gq_ref[rows, cols] * (GLA_DK ** -0.5), gk_ref[rows, cols],
            gv_ref[rows, head * GLA_DV:(head + 1) * GLA_DV], la)


def _seq_geom(latent):
    seq = DEC_SEQ if latent else SEQ
    return seq, (DEC_BATCH if latent else BATCH), (TP // seq if latent else 0)


def _zspec(latent, off, width, per_head=True):
    seq, _, blk0 = _seq_geom(latent)
    assert off % width == 0
    if per_head:
        return pl.BlockSpec((seq, width), lambda b, h, *_: (blk0 + b, off // width + h))
    return pl.BlockSpec((seq, width), lambda b, h, *_: (blk0 + b, off // width))


def _const_spec(a):
    return pl.BlockSpec(a.shape, lambda b, h, nd=a.ndim: (0,) * nd)


def _scan_hpb(latent):
    return 1 if latent else 2


def _scan_call(name, features, feat_specs, feat_args, z, gate_off, gain, heads, dk, dv,
               latent, s0t, prev_out, prev_state, layer):
    seq, nb, row_blk0 = _seq_geom(latent)
    hpb = _scan_hpb(latent)
    consts = _scan_consts()
    in_specs = ([_const_spec(a) for a in consts] + list(feat_specs)
                + [_zspec(latent, gate_off, hpb * dv), pl.BlockSpec((1, dv), lambda b, h: (0, 0))])
    args = list(consts) + list(feat_args) + [z, gain.reshape(1, dv)]
    n_in = len(feat_specs)
    if latent:
        in_specs.append(pl.BlockSpec((None, None, 2, hpb, dv, dk), lambda b, h: (b, layer, 0, h, 0, 0)))
        args.append(s0t)
    out_specs = [pl.BlockSpec((seq, hpb * dv), lambda b, h: (row_blk0 + b, h))]
    out_shape = [jax.ShapeDtypeStruct((T, heads * dv), BF16)]
    aliases = {}
    if prev_out is not None:
        in_specs.append(pl.BlockSpec(memory_space=pl.ANY))
        aliases[len(args)] = 0
        args.append(prev_out)
    emit_state = not latent
    if emit_state:
        out_specs.append(pl.BlockSpec((None, None, 2, hpb, dk, dv), lambda b, h: (b, layer, 0, h, 0, 0)))
        out_shape.append(jax.ShapeDtypeStruct((BATCH, DEPTH, 2, heads, dk, dv), F32))
        if prev_state is not None:
            in_specs.append(pl.BlockSpec(memory_space=pl.ANY))
            aliases[len(args)] = 1
            args.append(prev_state)
    n_alias = len(aliases)

    def kern(*refs):
        n_inputs = len(args) - n_alias
        keep = refs[:n_inputs] + refs[n_inputs + n_alias:]
        _scan_kernel(*keep, features=features, n_in=n_in, seq=seq, hpb=hpb, dv=dv, with_s0=latent,
                     emit_state=emit_state)

    res = pl.pallas_call(
        kern,
        grid=(nb, heads // hpb),
        in_specs=in_specs,
        out_specs=out_specs,
        out_shape=out_shape,
        input_output_aliases=aliases,
        scratch_shapes=[pltpu.VMEM((seq, hpb * dv), F32), pltpu.VMEM((2, hpb, dv, dk), F32)],
        compiler_params=_cparams(("parallel", "parallel")),
        name=name,
    )(*args)
    return (res[0], res[1]) if emit_state else (res[0], None)


def _hgrn(z, lb, gain, s0t, prev_state, layer):
    lb3 = lb.reshape(2, 1, HG_K)
    o = state = None
    for latent in (False, True):
        hpb = _scan_hpb(latent)
        specs = [_zspec(latent, OFF_HQ, hpb * HG_DK), _zspec(latent, OFF_HFF, hpb * HG_DK),
                 _zspec(latent, OFF_HFB, hpb * HG_DK), _zspec(latent, OFF_HI, hpb * HG_DV),
                 pl.BlockSpec((2, 1, hpb * HG_DK), lambda b, h: (0, 0, h))]
        o, st = _scan_call("hgrn_latent" if latent else "hgrn_prompt", _hgrn_features, specs, [z, z, z, z, lb3],
                           z, OFF_HG, gain, HG_HEADS, HG_DK, HG_DV, latent, s0t, o, prev_state, layer)
        state = st if st is not None else state
    return o, state


def _gla(z, w_gk, b_gk, gain, s0t, prev_state, layer):
    wpad = jnp.zeros((2, LANE, GLA_HEADS * GLA_DK), F32)
    wpad = wpad.at[0, 0:GLA_RANK].set(w_gk[0]).at[1, GLA_RANK:2 * GLA_RANK].set(w_gk[1]).astype(BF16)
    b3 = b_gk.reshape(2, 1, GLA_HEADS * GLA_DK)
    o = state = None
    for latent in (False, True):
        hpb = _scan_hpb(latent)
        specs = [_zspec(latent, OFF_GQ, hpb * GLA_DK), _zspec(latent, OFF_GK, hpb * GLA_DK),
                 _zspec(latent, OFF_GV, hpb * GLA_DV), _zspec(latent, OFF_GLF, LANE, per_head=False),
                 pl.BlockSpec((2, LANE, hpb * GLA_DK), lambda b, h: (0, 0, h)),
                 pl.BlockSpec((2, 1, hpb * GLA_DK), lambda b, h: (0, 0, h))]
        o, st = _scan_call("gla_latent" if latent else "gla_prompt", _gla_features, specs, [z, z, z, z, wpad, b3],
                           z, OFF_GG, gain, GLA_HEADS, GLA_DK, GLA_DV, latent, s0t, o, prev_state, layer)
        state = st if st is not None else state
    return o, state


GQA = N_HEADS // N_KV_HEADS


def _ctx_attn_kernel(q_ref, k_ref, v_ref, sink_ref, o_ref):
    q = (q_ref[...] * (HEAD_DIM ** -0.5)).astype(BF16)
    s = _dot_nt(q, k_ref[...].astype(BF16))
    sink = sink_ref[:, 0:1]
    m = jnp.maximum(jnp.max(s, axis=-1, keepdims=True), sink)
    p = jnp.exp(s - m)
    den = jnp.sum(p, axis=-1, keepdims=True) + jnp.exp(sink - m)
    o_ref[...] = (_dot(p.astype(BF16), v_ref[...].astype(BF16)) / den).astype(o_ref.dtype)


def _ctx_attention(z, sink3):
    hd = HEAD_DIM
    return pl.pallas_call(
        _ctx_attn_kernel,
        grid=(BATCH, N_HEADS),
        in_specs=[_zspec(False, OFF_AQ, hd),
                  pl.BlockSpec((SEQ, hd), lambda b, h: (b, OFF_AK // hd + h // GQA)),
                  pl.BlockSpec((SEQ, hd), lambda b, h: (b, OFF_AV // hd + h // GQA)),
                  pl.BlockSpec((None, 1, LANE), lambda b, h: (h, 0, 0))],
        out_specs=pl.BlockSpec((SEQ, hd), lambda b, h: (b, h)),
        out_shape=jax.ShapeDtypeStruct((T, N_HEADS * hd), BF16),
        compiler_params=_cparams(("parallel", "parallel")),
        name="ctx_attention",
    )(z, z, z, sink3)


def _rope(x, cos_f, sin_f):
    return x * cos_f + pltpu.roll(x, HEAD_DIM // 2, 1) * sin_f


def _lat_attn_kernel(q_ref, k_ref, v_ref, ck_ref, cv_ref, cos_ref, sin_ref, sink_ref, o_ref, kr_scr, vr_scr):
    blk, hd, seq = ATT_BLOCK, HEAD_DIM, DEC_SEQ
    nb = seq // blk

    @pl.when(pl.program_id(2) == 0)
    def _():
        zeros = jnp.zeros((blk, hd), BF16)
        for scr in (kr_scr, vr_scr):
            scr[0:blk, :] = zeros
            scr[blk + seq:2 * blk + seq, :] = zeros

        def fill(n, carry):
            r0 = pl.multiple_of(n * blk, blk)
            rows = pl.ds(r0, blk)
            kr_scr[pl.ds(r0 + blk, blk), :] = _rope(k_ref[rows, :], cos_ref[rows, :], sin_ref[rows, :]).astype(BF16)
            vr_scr[pl.ds(r0 + blk, blk), :] = v_ref[rows, :].astype(BF16)
            return carry

        lax.fori_loop(0, nb, fill, 0)

    ck = ck_ref[...].astype(BF16)
    cv = cv_ref[...].astype(BF16)
    sink = sink_ref[:, 0:1]
    qi = lax.broadcasted_iota(jnp.int32, (blk, 3 * blk), 0)
    kj = lax.broadcasted_iota(jnp.int32, (blk, 3 * blk), 1)
    window_bias = jnp.where(jnp.abs(kj - qi - blk) <= WINDOW, 0.0, -jnp.inf)

    def qblock(n, carry):
        r0 = pl.multiple_of(n * blk, blk)
        rows = pl.ds(r0, blk)
        q = (_rope(q_ref[rows, :], cos_ref[rows, :], sin_ref[rows, :]) * (hd ** -0.5)).astype(BF16)
        band = pl.ds(r0, 3 * blk)
        key_pos = kj + (n - 1) * blk
        s_loc = _dot_nt(q, kr_scr[band, :]) + window_bias
        s_loc = jnp.where(key_pos >= 0, s_loc, -jnp.inf)
        s_loc = jnp.where(key_pos < seq, s_loc, -jnp.inf)
        s_ctx = _dot_nt(q, ck)
        m = jnp.maximum(jnp.maximum(jnp.max(s_loc, axis=-1, keepdims=True),
                                    jnp.max(s_ctx, axis=-1, keepdims=True)), sink)
        p_loc = jnp.exp(s_loc - m)
        p_ctx = jnp.exp(s_ctx - m)
        den = (jnp.sum(p_loc, axis=-1, keepdims=True) + jnp.sum(p_ctx, axis=-1, keepdims=True)
               + jnp.exp(sink - m))
        o = _dot(p_ctx.astype(BF16), cv) + _dot(p_loc.astype(BF16), vr_scr[band, :])
        o_ref[rows, :] = (o / den).astype(o_ref.dtype)
        return carry

    lax.fori_loop(0, nb, qblock, 0)


def _rope_tables():
    n = jnp.arange(DEC_SEQ)
    row = (n // GRID_W).astype(F32)
    col = (n % GRID_W).astype(F32)
    n_freq = HEAD_DIM // 4
    inv = ROPE_THETA ** (-jnp.arange(n_freq, dtype=F32) / n_freq)
    ang = jnp.concatenate([row[:, None] * inv, col[:, None] * inv], axis=-1)
    cos, sin = jnp.cos(ang), jnp.sin(ang)
    return jnp.concatenate([cos, cos], axis=-1), jnp.concatenate([-sin, sin], axis=-1)


def _lat_attention(z, ck, cv, sink3, prev_out):
    hd = HEAD_DIM
    seq, _, blk0 = _seq_geom(True)
    cos_f, sin_f = _rope_tables()
    tab = pl.BlockSpec((seq, hd), lambda b, kh, g: (0, 0))
    cache = pl.BlockSpec((None, PAST_LEN, hd), lambda b, kh, g: (b, 0, kh))
    return pl.pallas_call(
        lambda *refs: _lat_attn_kernel(*refs[:8], *refs[9:]),
        grid=(DEC_BATCH, N_KV_HEADS, GQA),
        in_specs=[pl.BlockSpec((seq, hd), lambda b, kh, g: (blk0 + b, OFF_AQ // hd + kh * GQA + g)),
                  pl.BlockSpec((seq, hd), lambda b, kh, g: (blk0 + b, OFF_AK // hd + kh)),
                  pl.BlockSpec((seq, hd), lambda b, kh, g: (blk0 + b, OFF_AV // hd + kh)),
                  cache, cache, tab, tab,
                  pl.BlockSpec((None, 1, LANE), lambda b, kh, g: (kh * GQA + g, 0, 0)),
                  pl.BlockSpec(memory_space=pl.ANY)],
        out_specs=pl.BlockSpec((seq, hd), lambda b, kh, g: (blk0 + b, kh * GQA + g)),
        out_shape=jax.ShapeDtypeStruct((T, N_HEADS * hd), BF16),
        input_output_aliases={8: 0},
        scratch_shapes=[pltpu.VMEM((seq + 2 * ATT_BLOCK, hd), BF16), pltpu.VMEM((seq + 2 * ATT_BLOCK, hd), BF16)],
        compiler_params=_cparams(("parallel", "parallel", "arbitrary")),
        name="lat_attention",
    )(z, z, z, ck, cv, cos_f, sin_f, sink3, prev_out)


def _pad_cols(w, n):
    return jnp.pad(w, ((0, 0), (0, n - w.shape[1])))


def kernel(x_prompt, x_sample, cache_k, cache_v, state_hgrn, state_gla, c, c_ctx, w_ada, b_ada, norm_g, w_in,
           w_out, sgu_ln_g, sgu_ln_b, sgu_w, sgu_b, hgrn_lb_logits, hgrn_norm_g, attn_sink, gla_w_gk, gla_b_gk,
           gla_norm_g, ffn_w_up, ffn_conv_w, ffn_conv_b, ffn_w_down):
    assert TP % DEC_SEQ == 0 and DEC_BATCH + 1 <= MOD_ROWS
    lb_all = jnp.cumsum(jax.nn.softmax(hgrn_lb_logits.astype(F32), axis=0), axis=0)
    lb_all = lb_all - lb_all[0:1]

    cvec = jnp.zeros((MOD_ROWS, D_MODEL), F32).at[0].set(c_ctx).at[1:1 + DEC_BATCH].set(c)
    mod = _ada_mod(cvec, w_ada, b_ada).reshape(DEPTH, MOD_ROWS, 1, 6 * D_MODEL)
    SH1, SC1, G1, SH2, SC2, G2 = range(6)

    x = [x_prompt.reshape(TP, D_MODEL), x_sample.reshape(TL, D_MODEL)]
    hg_s0t = jnp.swapaxes(state_hgrn, -1, -2)
    gl_s0t = jnp.swapaxes(state_gla, -1, -2)
    ck_all = cache_k.reshape(DEC_BATCH, DEPTH, PAST_LEN, N_KV_HEADS * HEAD_DIM)
    cv_all = cache_v.reshape(DEC_BATCH, DEPTH, PAST_LEN, N_KV_HEADS * HEAD_DIM)

    h = _norm_mod(x, norm_g[0, 0], mod, 0, SC1, SH1)
    ks_new, vs_new = [], []
    hg_state = gl_state = None
    for l in range(DEPTH):
        w_in_b = _pad_cols(w_in[l], Z_COLS).astype(BF16)
        w_out_b = w_out[l].astype(BF16)
        w_up = ffn_w_up[l]
        w_up_b = jnp.concatenate([_pad_cols(w_up[:, :D_FF], D_FF_PAD), _pad_cols(w_up[:, D_FF:], D_FF_PAD)],
                                 axis=1).astype(BF16)
        w_down_b = jnp.pad(ffn_w_down[l], ((0, D_FF_PAD - D_FF), (0, 0))).astype(BF16)
        conv_w = jnp.pad(ffn_conv_w[l], ((0, 8 - CONV_W), (0, D_FF_PAD - D_FF)))
        conv_b = _pad_cols(ffn_conv_b[l].reshape(1, D_FF), D_FF_PAD)
        sink3 = jnp.broadcast_to(attn_sink[l].astype(F32)[:, None, None], (N_HEADS, 1, LANE))

        z = _matmul(h, w_in_b, F32, Z_TILE_N, "in_proj")
        o_sgu = _sgu(z, sgu_ln_g[l], sgu_ln_b[l], sgu_w[l], sgu_b[l])
        o_hg, hg_state = _hgrn(z, lb_all[l], hgrn_norm_g[l], hg_s0t, hg_state, l)
        o_att = _ctx_attention(z, sink3)
        o_att = _lat_attention(z, ck_all[:, l], cv_all[:, l], sink3, o_att)
        o_gl, gl_state = _gla(z, gla_w_gk[l], gla_b_gk[l], gla_norm_g[l], gl_s0t, gl_state, l)
        m = _out_proj([o_sgu, o_hg, o_att, o_gl], w_out_b)
        x, h = _resid(x, m, norm_g[l, 1], mod, l, G1, nxt=(norm_g[l, 2], l, SC2, SH2))
        ks_new.append(z[:TP, OFF_AK:OFF_AK + N_KV_HEADS * HEAD_DIM].reshape(BATCH, SEQ, N_KV_HEADS, HEAD_DIM))
        vs_new.append(z[:TP, OFF_AV:OFF_AV + N_KV_HEADS * HEAD_DIM].reshape(BATCH, SEQ, N_KV_HEADS, HEAD_DIM))

        act = _ffn_up(h, _conv_halo(h), w_up_b, conv_w, conv_b)
        f = _ffn_down(act, w_down_b)
        if l + 1 < DEPTH:
            x, h = _resid(x, f, norm_g[l, 3], mod, l, G2, nxt=(norm_g[l + 1, 0], l + 1, SC1, SH1))
        else:
            x, h = _resid(x, f, norm_g[l, 3], mod, l, G2, split_out=True)

    y_prompt = x[0].reshape(BATCH, SEQ, D_MODEL)
    y_sample = x[1].reshape(DEC_BATCH, DEC_SEQ, D_MODEL)
    return (y_prompt, y_sample, jnp.stack(ks_new, axis=1), jnp.stack(vs_new, axis=1), hg_state, gl_state)
```

```python
import functools

import numpy as np
import jax
import jax.numpy as jnp
from jax import lax
from jax.experimental import pallas as pl
from jax.experimental.pallas import tpu as pltpu

D_MODEL = 4096
BATCH = 32
SEQ = 256
DEPTH = 2
DEC_BATCH = 2
DEC_SEQ = 4096
PAST_LEN = 512
GRID_W = 64
GROUP_W = D_MODEL // 4
SGU_CHUNK = 128
SGU_GROUPS = 4
SGU_DIM = GROUP_W
SGU_GCH = SGU_DIM // SGU_GROUPS
HG_HEADS = 8
HG_DK = 128
HG_DV = GROUP_W // HG_HEADS
HG_K = HG_HEADS * HG_DK
N_HEADS = 8
N_KV_HEADS = 2
HEAD_DIM = GROUP_W // N_HEADS
WINDOW = 128
ATT_BLOCK = 128
ROPE_THETA = 10000.0
GLA_HEADS = 4
GLA_DK = 128
GLA_DV = GROUP_W // GLA_HEADS
GLA_RANK = 16
GLA_NORMALIZER = 16.0
D_FF = 11008
CONV_W = 3
EPS = 1e-6

F32 = jnp.float32
BF16 = jnp.bfloat16

IN_SIZES = (2 * SGU_DIM,
            HG_K, HG_K, HG_K, HG_HEADS * HG_DV, HG_HEADS * HG_DV,
            N_HEADS * HEAD_DIM, N_KV_HEADS * HEAD_DIM, N_KV_HEADS * HEAD_DIM,
            GLA_HEADS * GLA_DK, GLA_HEADS * GLA_DK, GLA_HEADS * GLA_DV, GLA_HEADS * GLA_DV,
            GLA_RANK, GLA_RANK)
D_IN = sum(IN_SIZES)
_OFF = [0] + [int(c) for c in np.cumsum(IN_SIZES)]
(OFF_SGU, OFF_HQ, OFF_HFF, OFF_HFB, OFF_HI, OFF_HG, OFF_AQ, OFF_AK, OFF_AV,
 OFF_GQ, OFF_GK, OFF_GV, OFF_GG, OFF_GLF, OFF_GLB) = _OFF[:-1]

LANE = 128
SUBLANE = 8
VMEM_LIMIT = 56 * 1024 * 1024

Z_TILE_N = 1024
Z_COLS = -(-D_IN // Z_TILE_N) * Z_TILE_N
SCAN_C = 128
MOD_ROWS = 16

TP = BATCH * SEQ
TL = DEC_BATCH * DEC_SEQ
T = TP + TL
PIECES = ((0, TP), (TP, TL))


def _cparams(sem):
    return pltpu.CompilerParams(dimension_semantics=sem, vmem_limit_bytes=VMEM_LIMIT)


def _row_group(row0):
    return jnp.where(row0 < TP, 0, 1 + (row0 - TP) // DEC_SEQ)


def _dot(a, b):
    return jnp.dot(a, b, preferred_element_type=F32)


def _dot_nt(a, b):
    return lax.dot_general(a, b, (((1,), (1,)), ((), ())), preferred_element_type=F32)


def _dot_tn(a, b):
    return lax.dot_general(a, b, (((0,), (0,)), ((), ())), preferred_element_type=F32)


def _sigmoid(x):
    return 1.0 / (1.0 + jnp.exp(-x))


def _silu(x):
    return x * _sigmoid(x)


def _gelu(x):
    return 0.5 * x * (1.0 + lax.erf(x * (2.0 ** -0.5)))


def _rms(x, g):
    return x * lax.rsqrt(jnp.mean(x * x, axis=-1, keepdims=True) + EPS) * g


def _ada_kernel(c_ref, w_ref, b_ref, o_ref):
    c = _silu(c_ref[...]).astype(BF16)
    o_ref[...] = _dot(c, w_ref[...].astype(BF16)) + b_ref[...]


def _ada_mod(cvec, w_ada, b_ada):
    tn = 512
    n = w_ada.shape[-1]
    return pl.pallas_call(
        _ada_kernel,
        grid=(DEPTH, n // tn),
        in_specs=[pl.BlockSpec((MOD_ROWS, D_MODEL), lambda l, j: (0, 0)),
                  pl.BlockSpec((None, D_MODEL, tn), lambda l, j: (l, 0, j)),
                  pl.BlockSpec((None, 1, tn), lambda l, j: (l, 0, j))],
        out_specs=pl.BlockSpec((None, MOD_ROWS, tn), lambda l, j: (l, 0, j)),
        out_shape=jax.ShapeDtypeStruct((DEPTH, MOD_ROWS, n), F32),
        compiler_params=_cparams(("parallel", "parallel")),
        name="ada_mod",
    )(cvec, w_ada, b_ada.reshape(DEPTH, 1, n))


def _mod_spec(layer, chunk, tr, blk0=0):
    return pl.BlockSpec((None, None, 1, D_MODEL),
                        lambda i, *_: (layer, _row_group((blk0 + i) * tr), 0, chunk))


NORM_ROWS = 256
_VEC_SPEC = pl.BlockSpec((1, D_MODEL), lambda i: (0, 0))


def _row_spec(blk0=0):
    return pl.BlockSpec((NORM_ROWS, D_MODEL), lambda i: (blk0 + i, 0))


def _norm_mod_kernel(x_ref, g_ref, sc_ref, sh_ref, h_ref):
    y = _rms(x_ref[...], g_ref[...])
    h_ref[...] = (y * (1.0 + sc_ref[...]) + sh_ref[...]).astype(h_ref.dtype)


def _norm_mod(x_parts, g, mod, layer, sc_chunk, sh_chunk):
    tr = NORM_ROWS
    h = None
    for x, (row0, rows) in zip(x_parts, PIECES):
        blk0 = row0 // tr
        in_specs = [_row_spec(), _VEC_SPEC, _mod_spec(layer, sc_chunk, tr, blk0), _mod_spec(layer, sh_chunk, tr, blk0)]
        args = [x, g.reshape(1, D_MODEL), mod, mod]
        aliases = {}
        if h is not None:
            in_specs.append(pl.BlockSpec(memory_space=pl.ANY))
            aliases[len(args)] = 0
            args.append(h)
        h = pl.pallas_call(
            lambda x_ref, g_ref, sc_ref, sh_ref, *rest: _norm_mod_kernel(x_ref, g_ref, sc_ref, sh_ref, rest[-1]),
            grid=(rows // tr,),
            in_specs=in_specs,
            out_specs=_row_spec(blk0),
            out_shape=jax.ShapeDtypeStruct((T, D_MODEL), BF16),
            input_output_aliases=aliases,
            compiler_params=_cparams(("parallel",)),
            name="norm_mod",
        )(*args)
    return h


def _resid_kernel(x_ref, m_ref, ga_ref, gate_ref, *rest, emit_h):
    x = x_ref[...] + gate_ref[...] * _rms(m_ref[...].astype(F32), ga_ref[...])
    if emit_h:
        gb_ref, sc_ref, sh_ref, xo_ref, h_ref = rest
        xo_ref[...] = x
        y = _rms(x, gb_ref[...])
        h_ref[...] = (y * (1.0 + sc_ref[...]) + sh_ref[...]).astype(h_ref.dtype)
    else:
        (xo_ref,) = rest
        xo_ref[...] = x


def _resid(x, m, ga, mod, layer, gate_chunk, nxt=None, split_out=False):
    tr = NORM_ROWS
    x_parts = x if isinstance(x, (list, tuple)) else None
    pieces = PIECES if (x_parts is not None or split_out) else ((0, T),)
    emit_h = nxt is not None
    assert not (emit_h and split_out)
    x_new, h, outs = None, None, []
    for p, (row0, rows) in enumerate(pieces):
        blk0 = row0 // tr
        in_specs = [_row_spec(0 if x_parts is not None else blk0), _row_spec(blk0), _VEC_SPEC,
                    _mod_spec(layer, gate_chunk, tr, blk0)]
        args = [x_parts[p] if x_parts is not None else x, m, ga.reshape(1, D_MODEL), mod]
        if emit_h:
            gb, layer_b, sc_chunk, sh_chunk = nxt
            in_specs += [_VEC_SPEC, _mod_spec(layer_b, sc_chunk, tr, blk0), _mod_spec(layer_b, sh_chunk, tr, blk0)]
            args += [gb.reshape(1, D_MODEL), mod, mod]
        n_in = len(args)
        if split_out:
            out_specs = [_row_spec()]
            out_shape = [jax.ShapeDtypeStruct((rows, D_MODEL), F32)]
        else:
            out_specs = [_row_spec(blk0)]
            out_shape = [jax.ShapeDtypeStruct((T, D_MODEL), F32)]
        if emit_h:
            out_specs.append(_row_spec(blk0))
            out_shape.append(jax.ShapeDtypeStruct((T, D_MODEL), BF16))
        aliases = {}
        if x_new is not None and not split_out:
            for k, prev in enumerate((x_new, h) if emit_h else (x_new,)):
                in_specs.append(pl.BlockSpec(memory_space=pl.ANY))
                aliases[len(args)] = k
                args.append(prev)
        n_alias = len(aliases)

        def kern(*refs, n_in=n_in, n_alias=n_alias):
            _resid_kernel(*refs[:n_in], *refs[n_in + n_alias:], emit_h=emit_h)

        res = pl.pallas_call(
            kern,
            grid=(rows // tr,),
            in_specs=in_specs,
            out_specs=out_specs,
            out_shape=out_shape,
            input_output_aliases=aliases,
            compiler_params=_cparams(("parallel",)),
            name="resid_norm",
        )(*args)
        x_new = res[0]
        h = res[1] if emit_h else None
        outs.append(res[0])
    return (outs if split_out else x_new), h


MM_TILE_M = 1024


def _mm_kernel(a_ref, b_ref, o_ref, *, n_valid):
    acc = _dot(a_ref[...], b_ref[...])
    tn = o_ref.shape[1]
    if n_valid % tn:
        col = lax.broadcasted_iota(jnp.int32, acc.shape, 1)
        acc = jnp.where(col < n_valid - pl.program_id(1) * tn, acc, 0.0)
    o_ref[...] = acc.astype(o_ref.dtype)


def _matmul(a, w, layer, n_out, out_dtype, tn, name):
    m, k = a.shape
    tm = MM_TILE_M
    return pl.pallas_call(
        functools.partial(_mm_kernel, n_valid=w.shape[2]),
        grid=(m // tm, n_out // tn),
        in_specs=[pl.BlockSpec((tm, k), lambda i, j: (i, 0)),
                  pl.BlockSpec((None, k, tn), lambda i, j: (layer, 0, j))],
        out_specs=pl.BlockSpec((tm, tn), lambda i, j: (i, j)),
        out_shape=jax.ShapeDtypeStruct((m, n_out), out_dtype),
        compiler_params=_cparams(("parallel", "parallel")),
        name=name,
    )(a, w)


def _mm4_kernel(a0_ref, a1_ref, a2_ref, a3_ref, b_ref, o_ref):
    acc = _dot(a0_ref[...], b_ref[0 * GROUP_W:1 * GROUP_W, :])
    acc += _dot(a1_ref[...], b_ref[1 * GROUP_W:2 * GROUP_W, :])
    acc += _dot(a2_ref[...], b_ref[2 * GROUP_W:3 * GROUP_W, :])
    acc += _dot(a3_ref[...], b_ref[3 * GROUP_W:4 * GROUP_W, :])
    o_ref[...] = acc.astype(o_ref.dtype)


def _out_proj(parts, w, layer):
    tm, tn = MM_TILE_M, 1024
    a_spec = pl.BlockSpec((tm, GROUP_W), lambda i, j: (i, 0))
    return pl.pallas_call(
        _mm4_kernel,
        grid=(T // tm, D_MODEL // tn),
        in_specs=[a_spec, a_spec, a_spec, a_spec,
                  pl.BlockSpec((None, 4 * GROUP_W, tn), lambda i, j: (layer, 0, j))],
        out_specs=pl.BlockSpec((tm, tn), lambda i, j: (i, j)),
        out_shape=jax.ShapeDtypeStruct((T, D_MODEL), BF16),
        compiler_params=_cparams(("parallel", "parallel")),
        name="out_proj",
    )(*parts, w)


def _mm_ktiled_kernel(a_ref, b_ref, o_ref, acc_ref):
    k = pl.program_id(2)
    last = pl.num_programs(2) - 1
    d = _dot(a_ref[...], b_ref[...])

    @pl.when(k == 0)
    def _():
        acc_ref[...] = d

    @pl.when(jnp.logical_and(k != 0, k != last))
    def _():
        acc_ref[...] += d

    @pl.when(k == last)
    def _():
        o_ref[...] = (acc_ref[...] + d).astype(o_ref.dtype)


def _ffn_down(a, w, layer):
    m, kk = a.shape
    n = w.shape[2]
    tm, tn, tk = MM_TILE_M, 512, kk // 2
    assert tk % LANE == 0
    return pl.pallas_call(
        _mm_ktiled_kernel,
        grid=(m // tm, n // tn, kk // tk),
        in_specs=[pl.BlockSpec((tm, tk), lambda i, j, k: (i, k)),
                  pl.BlockSpec((None, tk, tn), lambda i, j, k: (layer, k, j))],
        out_specs=pl.BlockSpec((tm, tn), lambda i, j, k: (i, j)),
        out_shape=jax.ShapeDtypeStruct((m, n), BF16),
        scratch_shapes=[pltpu.VMEM((tm, tn), F32)],
        compiler_params=_cparams(("parallel", "parallel", "arbitrary")),
        name="ffn_down",
    )(a, w)


FFN_TILE_M = 1024
FFN_TILE_N = 256
FFN_SUB_M = 512
HALO_ROWS = 16
assert D_FF % FFN_TILE_N == 0


def _ffn_up_kernel(h_ref, halo_ref, wg_ref, wv_ref, cw_ref, cb_ref, o_ref):
    tm, tn = o_ref.shape
    sub = FFN_SUB_M
    ns = tm // sub
    wg, wv = wg_ref[...], wv_ref[...]
    row0 = pl.program_id(0) * tm
    seq_len = jnp.where(row0 < TP, SEQ, DEC_SEQ)
    r = lax.broadcasted_iota(jnp.int32, (sub, tn), 0)
    gs = [_dot(h_ref[s * sub:(s + 1) * sub, :], wg) for s in range(ns)]
    gh = _dot(halo_ref[...], wg)
    for s in range(ns):
        val = _dot(h_ref[s * sub:(s + 1) * sub, :], wv)
        g = gs[s]
        before = gh[0:1, :] if s == 0 else gs[s - 1][sub - 1:sub, :]
        after = gh[1:2, :] if s == ns - 1 else gs[s + 1][0:1, :]
        pos = (row0 + s * sub + r) & (seq_len - 1)
        prev = jnp.where(r == 0, before, pltpu.roll(g, 1, 0))
        prev = jnp.where(pos == 0, 0.0, prev)
        nxt = jnp.where(r == sub - 1, after, pltpu.roll(g, sub - 1, 0))
        nxt = jnp.where(pos == seq_len - 1, 0.0, nxt)
        conv = prev * cw_ref[0:1, :] + g * cw_ref[1:2, :] + nxt * cw_ref[2:3, :] + cb_ref[...]
        o_ref[s * sub:(s + 1) * sub, :] = (_gelu(conv) * val).astype(o_ref.dtype)


def _ffn_up(h, halo, w_up, conv_w, conv_b, layer):
    tm, tn = FFN_TILE_M, FFN_TILE_N
    nj = D_FF // tn
    return pl.pallas_call(
        _ffn_up_kernel,
        grid=(T // tm, nj),
        in_specs=[pl.BlockSpec((tm, D_MODEL), lambda i, j: (i, 0)),
                  pl.BlockSpec((None, HALO_ROWS, D_MODEL), lambda i, j: (i, 0, 0)),
                  pl.BlockSpec((None, D_MODEL, tn), lambda i, j: (layer, 0, j)),
                  pl.BlockSpec((None, D_MODEL, tn), lambda i, j: (layer, 0, nj + j)),
                  pl.BlockSpec((None, CONV_W, tn), lambda i, j: (layer, 0, j)),
                  pl.BlockSpec((None, 1, tn), lambda i, j: (layer, 0, j))],
        out_specs=pl.BlockSpec((tm, tn), lambda i, j: (i, j)),
        out_shape=jax.ShapeDtypeStruct((T, D_FF), BF16),
        compiler_params=_cparams(("parallel", "parallel")),
        name="ffn_up",
    )(h, halo, w_up, w_up, conv_w, conv_b.reshape(DEPTH, 1, D_FF))


def _conv_halo(h):
    tm = FFN_TILE_M
    nt = T // tm
    ht = h.reshape(nt, tm, D_MODEL)
    zero = jnp.zeros((1, D_MODEL), h.dtype)
    before = jnp.concatenate([zero, ht[:-1, tm - 1]], axis=0)
    after = jnp.concatenate([ht[1:, 0], zero], axis=0)
    pad = jnp.zeros((nt, HALO_ROWS - 2, D_MODEL), h.dtype)
    return jnp.concatenate([before[:, None], after[:, None], pad], axis=1)


def _sgu_kernel(u_ref, v_ref, g_ref, b_ref, w_ref, bias_ref, o_ref):
    u = _gelu(u_ref[...])
    v = _gelu(v_ref[...])
    vc = v - jnp.mean(v, axis=-1, keepdims=True)
    vn = vc * lax.rsqrt(jnp.mean(vc * vc, axis=-1, keepdims=True) + EPS) * g_ref[...] + b_ref[...]
    vb = vn.astype(BF16)
    for g in range(SGU_GROUPS):
        sl = slice(g * SGU_GCH, (g + 1) * SGU_GCH)
        mixed = _dot(w_ref[g], vb[:, sl]) + bias_ref[:, sl]
        o_ref[:, sl] = (u[:, sl] * mixed).astype(o_ref.dtype)


def _sgu(z, ln_g, ln_b, w_s, b_s):
    c = SGU_CHUNK
    bias = jnp.repeat(b_s.T, SGU_GCH, axis=1)
    vec = pl.BlockSpec((1, SGU_DIM), lambda i: (0, 0))
    return pl.pallas_call(
        _sgu_kernel,
        grid=(T // c,),
        in_specs=[pl.BlockSpec((c, SGU_DIM), lambda i: (i, OFF_SGU // SGU_DIM)),
                  pl.BlockSpec((c, SGU_DIM), lambda i: (i, OFF_SGU // SGU_DIM + 1)),
                  vec, vec,
                  pl.BlockSpec((SGU_GROUPS, c, c), lambda i: (0, 0, 0)),
                  pl.BlockSpec((c, SGU_DIM), lambda i: (0, 0))],
        out_specs=pl.BlockSpec((c, SGU_DIM), lambda i: (i, 0)),
        out_shape=jax.ShapeDtypeStruct((T, SGU_DIM), BF16),
        compiler_params=_cparams(("parallel",)),
        name="sgu",
    )(z, z, ln_g.reshape(1, SGU_DIM), ln_b.reshape(1, SGU_DIM), w_s.astype(BF16), bias)


SMALL_LEVELS = tuple(m for m in (1, 2, 4) if m < SUBLANE)
BIG_LEVELS = tuple(m for m in (8, 16, 32, 64) if m < SCAN_C)


def _scan_consts():
    c = SCAN_C
    t = np.arange(c)[:, None]
    s = np.arange(c)[None, :]
    x = t ^ s
    tri = np.stack([s <= t, s >= t]).astype(np.float32)
    pair = [np.stack([x == 0, x == 0])]
    for m in SMALL_LEVELS:
        lvl = (x >= m) & (x < 2 * m)
        pair.append(np.stack([lvl & (t > s), lvl & (t < s)]))
    pair = np.stack(pair).astype(np.float32)
    rowq = np.stack([np.broadcast_to(((np.arange(c) & m) != 0)[:, None], (c, LANE)) for m in SMALL_LEVELS])
    hb = np.arange(c // 2)
    blk = np.stack([(hb[:, None] // m) == (hb[None, :] // m) for m in BIG_LEVELS])
    return (jnp.asarray(tri, BF16), jnp.asarray(pair, F32), jnp.asarray(rowq.astype(np.float32), F32),
            jnp.asarray(blk.astype(np.float32), F32))


def _split3(x):
    hi = x.astype(BF16)
    r = x - hi.astype(F32)
    mid = r.astype(BF16)
    lo = (r - mid.astype(F32)).astype(BF16)
    return hi, mid, lo


def _halves(a, m, second):
    off = m if second else 0
    return jnp.concatenate([a[j + off:j + off + m] for j in range(0, a.shape[0], 2 * m)], axis=0)


def _chunk_scans(probs, consts):
    tri_ref, pair_ref, rowq_ref, blk_ref = consts
    c, dk = probs[0][0].shape
    dirs = [1 if p[5] else 0 for p in probs]
    vbs = [p[2].astype(BF16) for p in probs]
    diag = [pair_ref[0, d] * _dot_nt(p[0].astype(BF16), p[1].astype(BF16)) for p, d in zip(probs, dirs)]
    bs = []
    for p, d in zip(probs, dirs):
        hi, mid, lo = _split3(p[3])
        tri = tri_ref[d]
        bs.append(_dot(tri, hi) + _dot(tri, mid) + _dot(tri, lo))
    inter = [_dot_nt((p[0] * jnp.exp(b)).astype(BF16), p[4].astype(BF16)) for p, b in zip(probs, bs)]

    big = [[] for _ in probs]
    for li, m in enumerate(BIG_LEVELS):
        for pi, ((q, k, _, _, _, reverse), b) in enumerate(zip(probs, bs)):
            q_second = not reverse
            refs = [b[j + m:j + m + 1] if reverse else b[j + m - 1:j + m] for j in range(0, c, 2 * m)]
            ref = jnp.concatenate([jnp.broadcast_to(r, (m, dk)) for r in refs], axis=0)
            qf = _halves(q, m, q_second) * jnp.exp(_halves(b, m, q_second) - ref)
            kf = _halves(k, m, not q_second) * jnp.exp(ref - _halves(b, m, not q_second))
            s_l = _dot_nt(qf.astype(BF16), kf.astype(BF16))
            if 2 * m < c:
                s_l = s_l * blk_ref[li]
            big[pi].append(s_l.astype(BF16))

    scores = diag
    es = list(bs)
    for li, m in enumerate(SMALL_LEVELS):
        for pi, ((q, k, _, _, _, reverse), b) in enumerate(zip(probs, bs)):
            is_q = (rowq_ref[li] < 0.5) if reverse else (rowq_ref[li] > 0.5)
            sh_q, sh_e = (c - m, m) if reverse else (m, c - m)
            fac = jnp.exp(jnp.where(is_q, b - pltpu.roll(es[pi], sh_q, 0), es[pi] - b))
            u = (jnp.where(is_q, q, k) * fac).astype(BF16)
            scores[pi] = scores[pi] + pair_ref[1 + li, dirs[pi]] * _dot_nt(u, u)
            if li + 1 < len(SMALL_LEVELS):
                es[pi] = jnp.where(is_q, es[pi], pltpu.roll(es[pi], sh_e, 0))

    outs = []
    for pi, ((q, k, _, _, st, reverse), b) in enumerate(zip(probs, bs)):
        o = inter[pi] + _dot(scores[pi].astype(BF16), vbs[pi])
        parts = [o[g:g + SUBLANE] for g in range(0, c, SUBLANE)]
        q_second = not reverse
        for li, m in enumerate(BIG_LEVELS):
            oc = _dot(big[pi][li], _halves(vbs[pi], m, not q_second))
            for jj, j in enumerate(range(0, c, 2 * m)):
                for g in range(0, m, SUBLANE):
                    dst = (j + (m if q_second else 0) + g) // SUBLANE
                    parts[dst] = parts[dst] + oc[jj * m + g:jj * m + g + SUBLANE]
        bl = b[0:1, :] if reverse else b[c - 1:c, :]
        kd = (k * jnp.exp(bl - b)).astype(BF16)
        st_new = jnp.exp(bl) * st + _dot_tn(vbs[pi], kd)
        outs.append((jnp.concatenate(parts, axis=0), st_new))
    return outs


def _scan_kernel(*refs, features, n_in, seq, hpb, dv, with_s0, emit_state):
    consts = refs[:4]
    in_refs = refs[4:4 + n_in]
    gate_ref, gain_ref = refs[4 + n_in], refs[5 + n_in]
    pos = 6 + n_in
    s0_ref = None
    if with_s0:
        s0_ref = refs[pos]
        pos += 1
    o_ref = refs[pos]
    pos += 1
    sf_ref = None
    if emit_state:
        sf_ref = refs[pos]
        pos += 1
    o_scr, st_scr = refs[pos:pos + 2]
    c = SCAN_C
    n = seq // c
    half = n // 2
    st_scr[...] = s0_ref[...] if with_s0 else jnp.zeros(st_scr.shape, F32)
    work = [(hh, reverse) for hh in range(hpb) for reverse in (False, True)]

    def step(i, finalize):
        r0s = {False: pl.multiple_of(i * c, c), True: pl.multiple_of((n - 1 - i) * c, c)}
        probs = [features(in_refs, r0s[rev], hh, rev) + (st_scr[1 if rev else 0, hh], rev) for hh, rev in work]
        for (hh, rev), (o, st) in zip(work, _chunk_scans(probs, consts)):
            st_scr[1 if rev else 0, hh] = st
            rows, cols = pl.ds(r0s[rev], c), slice(hh * dv, (hh + 1) * dv)
            if finalize:
                y = _rms(o + o_scr[rows, cols], gain_ref[...]) * _silu(gate_ref[rows, cols])
                o_ref[rows, cols] = y.astype(o_ref.dtype)
            else:
                o_scr[rows, cols] = o

    lax.fori_loop(0, half, lambda i, carry: (step(i, False), carry)[1], 0)
    lax.fori_loop(half, n, lambda i, carry: (step(i, True), carry)[1], 0)
    if emit_state:
        for d in range(2):
            for hh in range(hpb):
                sf_ref[d, hh] = st_scr[d, hh].T


def _hgrn_features(in_refs, r0, head, reverse):
    hq_ref, hf_ref, hb_ref, hi_ref, lb_ref = in_refs
    rows, cols = pl.ds(r0, SCAN_C), slice(head * HG_DK, (head + 1) * HG_DK)
    q = _silu(hq_ref[rows, cols]) * (HG_DK ** -0.5)
    lb = lb_ref[1 if reverse else 0][:, cols]
    f = lb + (1.0 - lb) * _sigmoid((hb_ref if reverse else hf_ref)[rows, cols])
    return q, 1.0 - f, hi_ref[rows, head * HG_DV:(head + 1) * HG_DV], jnp.log(f)


def _gla_features(in_refs, r0, head, reverse):
    gq_ref, gk_ref, gv_ref, low_ref, wgk_ref, bgk_ref = in_refs
    rows, cols = pl.ds(r0, SCAN_C), slice(head * GLA_DK, (head + 1) * GLA_DK)
    d = 1 if reverse else 0
    g = _dot(low_ref[rows, :].astype(BF16), wgk_ref[d][:, cols]) + bgk_ref[d][:, cols]
    la = (jnp.minimum(g, 0.0) - jnp.log(1.0 + jnp.exp(-jnp.abs(g)))) * (1.0 / GLA_NORMALIZER)
    return (gq_ref[rows, cols] * (GLA_DK ** -0.5), gk_ref[rows, cols],
            gv_ref[rows, head * GLA_DV:(head + 1) * GLA_DV], la)


def _seq_geom(latent):
    seq = DEC_SEQ if latent else SEQ
    return seq, (DEC_BATCH if latent else BATCH), (TP // seq if latent else 0)


def _zspec(latent, off, width, per_head=True):
    seq, _, blk0 = _seq_geom(latent)
    assert off % width == 0
    if per_head:
        return pl.BlockSpec((seq, width), lambda b, h, *_: (blk0 + b, off // width + h))
    return pl.BlockSpec((seq, width), lambda b, h, *_: (blk0 + b, off // width))


def _const_spec(a):
    return pl.BlockSpec(a.shape, lambda b, h, nd=a.ndim: (0,) * nd)


def _scan_hpb(latent):
    return 1 if latent else 2


def _scan_call(name, features, feat_specs, feat_args, z, gate_off, gain, heads, dk, dv,
               latent, s0t, prev_out, prev_state, layer):
    seq, nb, row_blk0 = _seq_geom(latent)
    hpb = _scan_hpb(latent)
    consts = _scan_consts()
    in_specs = ([_const_spec(a) for a in consts] + list(feat_specs)
                + [_zspec(latent, gate_off, hpb * dv), pl.BlockSpec((1, dv), lambda b, h: (0, 0))])
    args = list(consts) + list(feat_args) + [z, gain.reshape(1, dv)]
    n_in = len(feat_specs)
    if latent:
        in_specs.append(pl.BlockSpec((None, None, 2, hpb, dv, dk), lambda b, h: (b, layer, 0, h, 0, 0)))
        args.append(s0t)
    out_specs = [pl.BlockSpec((seq, hpb * dv), lambda b, h: (row_blk0 + b, h))]
    out_shape = [jax.ShapeDtypeStruct((T, heads * dv), BF16)]
    aliases = {}
    if prev_out is not None:
        in_specs.append(pl.BlockSpec(memory_space=pl.ANY))
        aliases[len(args)] = 0
        args.append(prev_out)
    emit_state = not latent
    if emit_state:
        out_specs.append(pl.BlockSpec((None, None, 2, hpb, dk, dv), lambda b, h: (b, layer, 0, h, 0, 0)))
        out_shape.append(jax.ShapeDtypeStruct((BATCH, DEPTH, 2, heads, dk, dv), F32))
        if prev_state is not None:
            in_specs.append(pl.BlockSpec(memory_space=pl.ANY))
            aliases[len(args)] = 1
            args.append(prev_state)
    n_alias = len(aliases)

    def kern(*refs):
        n_inputs = len(args) - n_alias
        keep = refs[:n_inputs] + refs[n_inputs + n_alias:]
        _scan_kernel(*keep, features=features, n_in=n_in, seq=seq, hpb=hpb, dv=dv, with_s0=latent,
                     emit_state=emit_state)

    res = pl.pallas_call(
        kern,
        grid=(nb, heads // hpb),
        in_specs=in_specs,
        out_specs=out_specs,
        out_shape=out_shape,
        input_output_aliases=aliases,
        scratch_shapes=[pltpu.VMEM((seq, hpb * dv), F32), pltpu.VMEM((2, hpb, dv, dk), F32)],
        compiler_params=_cparams(("parallel", "parallel")),
        name=name,
    )(*args)
    return (res[0], res[1]) if emit_state else (res[0], None)


def _hgrn(z, lb, gain, s0t, prev_state, layer):
    lb3 = lb.reshape(2, 1, HG_K)
    o = state = None
    for latent in (False, True):
        hpb = _scan_hpb(latent)
        specs = [_zspec(latent, OFF_HQ, hpb * HG_DK), _zspec(latent, OFF_HFF, hpb * HG_DK),
                 _zspec(latent, OFF_HFB, hpb * HG_DK), _zspec(latent, OFF_HI, hpb * HG_DV),
                 pl.BlockSpec((2, 1, hpb * HG_DK), lambda b, h: (0, 0, h))]
        o, st = _scan_call("hgrn_latent" if latent else "hgrn_prompt", _hgrn_features, specs, [z, z, z, z, lb3],
                           z, OFF_HG, gain, HG_HEADS, HG_DK, HG_DV, latent, s0t, o, prev_state, layer)
        state = st if st is not None else state
    return o, state


def _gla(z, w_gk, b_gk, gain, s0t, prev_state, layer):
    wpad = jnp.zeros((2, LANE, GLA_HEADS * GLA_DK), F32)
    wpad = wpad.at[0, 0:GLA_RANK].set(w_gk[0]).at[1, GLA_RANK:2 * GLA_RANK].set(w_gk[1]).astype(BF16)
    b3 = b_gk.reshape(2, 1, GLA_HEADS * GLA_DK)
    o = state = None
    for latent in (False, True):
        hpb = _scan_hpb(latent)
        specs = [_zspec(latent, OFF_GQ, hpb * GLA_DK), _zspec(latent, OFF_GK, hpb * GLA_DK),
                 _zspec(latent, OFF_GV, hpb * GLA_DV), _zspec(latent, OFF_GLF, LANE, per_head=False),
                 pl.BlockSpec((2, LANE, hpb * GLA_DK), lambda b, h: (0, 0, h)),
                 pl.BlockSpec((2, 1, hpb * GLA_DK), lambda b, h: (0, 0, h))]
        o, st = _scan_call("gla_latent" if latent else "gla_prompt", _gla_features, specs, [z, z, z, z, wpad, b3],
                           z, OFF_GG, gain, GLA_HEADS, GLA_DK, GLA_DV, latent, s0t, o, prev_state, layer)
        state = st if st is not None else state
    return o, state


GQA = N_HEADS // N_KV_HEADS


def _ctx_attn_kernel(q_ref, k_ref, v_ref, sink_ref, o_ref):
    hd = HEAD_DIM
    k = k_ref[...].astype(BF16)
    v = v_ref[...].astype(BF16)
    heads = range(GQA)
    ss = [_dot_nt((q_ref[:, g * hd:(g + 1) * hd] * (hd ** -0.5)).astype(BF16), k) for g in heads]
    ps, dens = [], []
    for g in heads:
        sink = sink_ref[g][:, 0:1]
        m = jnp.maximum(jnp.max(ss[g], axis=-1, keepdims=True), sink)
        p = jnp.exp(ss[g] - m)
        dens.append(jnp.sum(p, axis=-1, keepdims=True) + jnp.exp(sink - m))
        ps.append(p.astype(BF16))
    for g in heads:
        o_ref[:, g * hd:(g + 1) * hd] = (_dot(ps[g], v) / dens[g]).astype(o_ref.dtype)


def _ctx_attention(z, sink3):
    hd = HEAD_DIM
    return pl.pallas_call(
        _ctx_attn_kernel,
        grid=(BATCH, N_KV_HEADS),
        in_specs=[_zspec(False, OFF_AQ, GQA * hd),
                  pl.BlockSpec((SEQ, hd), lambda b, kh: (b, OFF_AK // hd + kh)),
                  pl.BlockSpec((SEQ, hd), lambda b, kh: (b, OFF_AV // hd + kh)),
                  pl.BlockSpec((GQA, 1, LANE), lambda b, kh: (kh, 0, 0))],
        out_specs=pl.BlockSpec((SEQ, GQA * hd), lambda b, kh: (b, kh)),
        out_shape=jax.ShapeDtypeStruct((T, N_HEADS * hd), BF16),
        compiler_params=_cparams(("parallel", "parallel")),
        name="ctx_attention",
    )(z, z, z, sink3)


def _rope(x, cos_f, sin_f):
    return x * cos_f + pltpu.roll(x, HEAD_DIM // 2, 1) * sin_f


def _lat_attn_kernel(q_ref, k_ref, v_ref, ck_ref, cv_ref, cos_ref, sin_ref, sink_ref, o_ref, kr_scr, vr_scr):
    blk, hd, seq = ATT_BLOCK, HEAD_DIM, DEC_SEQ
    nb = seq // blk

    zeros = jnp.zeros((blk, hd), BF16)
    for scr in (kr_scr, vr_scr):
        scr[0:blk, :] = zeros
        scr[blk + seq:2 * blk + seq, :] = zeros

    def fill(n, carry):
        r0 = pl.multiple_of(n * blk, blk)
        rows = pl.ds(r0, blk)
        kr_scr[pl.ds(r0 + blk, blk), :] = _rope(k_ref[rows, :], cos_ref[rows, :], sin_ref[rows, :]).astype(BF16)
        vr_scr[pl.ds(r0 + blk, blk), :] = v_ref[rows, :].astype(BF16)
        return carry

    lax.fori_loop(0, nb, fill, 0)

    ck = ck_ref[...].astype(BF16)
    cv = cv_ref[...].astype(BF16)
    sink = jnp.concatenate([jnp.broadcast_to(sink_ref[g][:, 0:1], (blk, 1)) for g in range(GQA)], axis=0)
    qi = lax.broadcasted_iota(jnp.int32, (GQA * blk, 3 * blk), 0) & (blk - 1)
    kj = lax.broadcasted_iota(jnp.int32, (GQA * blk, 3 * blk), 1)
    window_bias = jnp.where(jnp.abs(kj - qi - blk) <= WINDOW, 0.0, -jnp.inf)

    def qblock(n, carry):
        r0 = pl.multiple_of(n * blk, blk)
        rows = pl.ds(r0, blk)
        cos_f, sin_f = cos_ref[rows, :], sin_ref[rows, :]
        q = jnp.concatenate([(_rope(q_ref[rows, g * hd:(g + 1) * hd], cos_f, sin_f) * (hd ** -0.5)).astype(BF16)
                             for g in range(GQA)], axis=0)
        band = pl.ds(r0, 3 * blk)
        key_pos = kj + (n - 1) * blk
        s_loc = _dot_nt(q, kr_scr[band, :]) + window_bias
        s_loc = jnp.where(key_pos >= 0, s_loc, -jnp.inf)
        s_loc = jnp.where(key_pos < seq, s_loc, -jnp.inf)
        s_ctx = _dot_nt(q, ck)
        m = jnp.maximum(jnp.maximum(jnp.max(s_loc, axis=-1, keepdims=True),
                                    jnp.max(s_ctx, axis=-1, keepdims=True)), sink)
        p_loc = jnp.exp(s_loc - m)
        p_ctx = jnp.exp(s_ctx - m)
        den = (jnp.sum(p_loc, axis=-1, keepdims=True) + jnp.sum(p_ctx, axis=-1, keepdims=True)
               + jnp.exp(sink - m))
        o = (_dot(p_ctx.astype(BF16), cv) + _dot(p_loc.astype(BF16), vr_scr[band, :])) / den
        for g in range(GQA):
            o_ref[rows, g * hd:(g + 1) * hd] = o[g * blk:(g + 1) * blk].astype(o_ref.dtype)
        return carry

    lax.fori_loop(0, nb, qblock, 0)


def _rope_tables():
    n = jnp.arange(DEC_SEQ)
    row = (n // GRID_W).astype(F32)
    col = (n % GRID_W).astype(F32)
    n_freq = HEAD_DIM // 4
    inv = ROPE_THETA ** (-jnp.arange(n_freq, dtype=F32) / n_freq)
    ang = jnp.concatenate([row[:, None] * inv, col[:, None] * inv], axis=-1)
    cos, sin = jnp.cos(ang), jnp.sin(ang)
    return jnp.concatenate([cos, cos], axis=-1), jnp.concatenate([-sin, sin], axis=-1)


def _lat_attention(z, ck, cv, sink3, prev_out):
    hd = HEAD_DIM
    seq, _, blk0 = _seq_geom(True)
    cos_f, sin_f = _rope_tables()
    tab = pl.BlockSpec((seq, hd), lambda b, kh: (0, 0))
    cache = pl.BlockSpec((None, PAST_LEN, hd), lambda b, kh: (b, 0, kh))
    return pl.pallas_call(
        lambda *refs: _lat_attn_kernel(*refs[:8], *refs[9:]),
        grid=(DEC_BATCH, N_KV_HEADS),
        in_specs=[_zspec(True, OFF_AQ, GQA * hd),
                  pl.BlockSpec((seq, hd), lambda b, kh: (blk0 + b, OFF_AK // hd + kh)),
                  pl.BlockSpec((seq, hd), lambda b, kh: (blk0 + b, OFF_AV // hd + kh)),
                  cache, cache, tab, tab,
                  pl.BlockSpec((GQA, 1, LANE), lambda b, kh: (kh, 0, 0)),
                  pl.BlockSpec(memory_space=pl.ANY)],
        out_specs=pl.BlockSpec((seq, GQA * hd), lambda b, kh: (blk0 + b, kh)),
        out_shape=jax.ShapeDtypeStruct((T, N_HEADS * hd), BF16),
        input_output_aliases={8: 0},
        scratch_shapes=[pltpu.VMEM((seq + 2 * ATT_BLOCK, hd), BF16), pltpu.VMEM((seq + 2 * ATT_BLOCK, hd), BF16)],
        compiler_params=_cparams(("parallel", "parallel")),
        name="lat_attention",
    )(z, z, z, ck, cv, cos_f, sin_f, sink3, prev_out)


def _pad_cols(w, n):
    return jnp.pad(w, ((0, 0), (0, n - w.shape[1])))


def kernel(x_prompt, x_sample, cache_k, cache_v, state_hgrn, state_gla, c, c_ctx, w_ada, b_ada, norm_g, w_in,
           w_out, sgu_ln_g, sgu_ln_b, sgu_w, sgu_b, hgrn_lb_logits, hgrn_norm_g, attn_sink, gla_w_gk, gla_b_gk,
           gla_norm_g, ffn_w_up, ffn_conv_w, ffn_conv_b, ffn_w_down):
    assert TP % DEC_SEQ == 0 and DEC_BATCH + 1 <= MOD_ROWS
    lb_all = jnp.cumsum(jax.nn.softmax(hgrn_lb_logits.astype(F32), axis=0), axis=0)
    lb_all = lb_all - lb_all[0:1]

    cvec = jnp.zeros((MOD_ROWS, D_MODEL), F32).at[0].set(c_ctx).at[1:1 + DEC_BATCH].set(c)
    mod = _ada_mod(cvec, w_ada, b_ada).reshape(DEPTH, MOD_ROWS, 1, 6 * D_MODEL)
    SH1, SC1, G1, SH2, SC2, G2 = range(6)

    x = [x_prompt.reshape(TP, D_MODEL), x_sample.reshape(TL, D_MODEL)]
    hg_s0t = jnp.swapaxes(state_hgrn, -1, -2)
    gl_s0t = jnp.swapaxes(state_gla, -1, -2)
    ck_all = cache_k.reshape(DEC_BATCH, DEPTH, PAST_LEN, N_KV_HEADS * HEAD_DIM)
    cv_all = cache_v.reshape(DEC_BATCH, DEPTH, PAST_LEN, N_KV_HEADS * HEAD_DIM)

    h = _norm_mod(x, norm_g[0, 0], mod, 0, SC1, SH1)
    ks_new, vs_new = [], []
    hg_state = gl_state = None
    w_in_b, w_out_b = w_in.astype(BF16), w_out.astype(BF16)
    w_up_b, w_down_b = ffn_w_up.astype(BF16), ffn_w_down.astype(BF16)
    for l in range(DEPTH):
        sink3 = jnp.broadcast_to(attn_sink[l].astype(F32)[:, None, None], (N_HEADS, 1, LANE))

        z = _matmul(h, w_in_b, l, Z_COLS, F32, Z_TILE_N, "in_proj")
        o_sgu = _sgu(z, sgu_ln_g[l], sgu_ln_b[l], sgu_w[l], sgu_b[l])
        o_hg, hg_state = _hgrn(z, lb_all[l], hgrn_norm_g[l], hg_s0t, hg_state, l)
        o_att = _ctx_attention(z, sink3)
        o_att = _lat_attention(z, ck_all[:, l], cv_all[:, l], sink3, o_att)
        o_gl, gl_state = _gla(z, gla_w_gk[l], gla_b_gk[l], gla_norm_g[l], gl_s0t, gl_state, l)
        m = _out_proj([o_sgu, o_hg, o_att, o_gl], w_out_b, l)
        x, h = _resid(x, m, norm_g[l, 1], mod, l, G1, nxt=(norm_g[l, 2], l, SC2, SH2))
        ks_new.append(z[:TP, OFF_AK:OFF_AK + N_KV_HEADS * HEAD_DIM].reshape(BATCH, SEQ, N_KV_HEADS, HEAD_DIM))
        vs_new.append(z[:TP, OFF_AV:OFF_AV + N_KV_HEADS * HEAD_DIM].reshape(BATCH, SEQ, N_KV_HEADS, HEAD_DIM))

        act = _ffn_up(h, _conv_halo(h), w_up_b, ffn_conv_w, ffn_conv_b, l)
        f = _ffn_down(act, w_down_b, l)
        if l + 1 < DEPTH:
            x, h = _resid(x, f, norm_g[l, 3], mod, l, G2, nxt=(norm_g[l + 1, 0], l + 1, SC1, SH1))
        else:
            x, h = _resid(x, f, norm_g[l, 3], mod, l, G2, split_out=True)

    y_prompt = x[0].reshape(BATCH, SEQ, D_MODEL)
    y_sample = x[1].reshape(DEC_BATCH, DEC_SEQ, D_MODEL)
    return (y_prompt, y_sample, jnp.stack(ks_new, axis=1), jnp.stack(vs_new, axis=1), hg_state, gl_state)
```

```python
import functools

import numpy as np
import jax
import jax.numpy as jnp
from jax import lax
from jax.experimental import pallas as pl
from jax.experimental.pallas import tpu as pltpu

D_MODEL = 4096
BATCH = 32
SEQ = 256
DEPTH = 2
DEC_BATCH = 2
DEC_SEQ = 4096
PAST_LEN = 512
GRID_W = 64
GROUP_W = D_MODEL // 4
SGU_CHUNK = 128
SGU_GROUPS = 4
SGU_DIM = GROUP_W
SGU_GCH = SGU_DIM // SGU_GROUPS
HG_HEADS = 8
HG_DK = 128
HG_DV = GROUP_W // HG_HEADS
HG_K = HG_HEADS * HG_DK
N_HEADS = 8
N_KV_HEADS = 2
HEAD_DIM = GROUP_W // N_HEADS
WINDOW = 128
ATT_BLOCK = 128
ROPE_THETA = 10000.0
GLA_HEADS = 4
GLA_DK = 128
GLA_DV = GROUP_W // GLA_HEADS
GLA_RANK = 16
GLA_NORMALIZER = 16.0
D_FF = 11008
CONV_W = 3
EPS = 1e-6

F32 = jnp.float32
BF16 = jnp.bfloat16
LOG2E = 1.4426950408889634

IN_SIZES = (2 * SGU_DIM,
            HG_K, HG_K, HG_K, HG_HEADS * HG_DV, HG_HEADS * HG_DV,
            N_HEADS * HEAD_DIM, N_KV_HEADS * HEAD_DIM, N_KV_HEADS * HEAD_DIM,
            GLA_HEADS * GLA_DK, GLA_HEADS * GLA_DK, GLA_HEADS * GLA_DV, GLA_HEADS * GLA_DV,
            GLA_RANK, GLA_RANK)
D_IN = sum(IN_SIZES)
_OFF = [0] + [int(c) for c in np.cumsum(IN_SIZES)]
(OFF_SGU, OFF_HQ, OFF_HFF, OFF_HFB, OFF_HI, OFF_HG, OFF_AQ, OFF_AK, OFF_AV,
 OFF_GQ, OFF_GK, OFF_GV, OFF_GG, OFF_GLF, OFF_GLB) = _OFF[:-1]

LANE = 128
SUBLANE = 8
VMEM_LIMIT = 56 * 1024 * 1024

Z_TILE_N = 1024
Z_COLS = -(-D_IN // Z_TILE_N) * Z_TILE_N
SCAN_C = 128
MOD_ROWS = 16

TP = BATCH * SEQ
TL = DEC_BATCH * DEC_SEQ
T = TP + TL
PIECES = ((0, TP), (TP, TL))


def _cparams(sem):
    return pltpu.CompilerParams(dimension_semantics=sem, vmem_limit_bytes=VMEM_LIMIT)


def _row_group(row0):
    return jnp.where(row0 < TP, 0, 1 + (row0 - TP) // DEC_SEQ)


def _dot(a, b):
    return jnp.dot(a, b, preferred_element_type=F32)


def _dot_nt(a, b):
    return lax.dot_general(a, b, (((1,), (1,)), ((), ())), preferred_element_type=F32)


def _dot_tn(a, b):
    return lax.dot_general(a, b, (((0,), (0,)), ((), ())), preferred_element_type=F32)


def _sigmoid(x):
    return 1.0 / (1.0 + jnp.exp(-x))


def _silu(x):
    return x * _sigmoid(x)


def _gelu(x):
    return 0.5 * x * (1.0 + lax.erf(x * (2.0 ** -0.5)))


def _rms(x, g):
    return x * lax.rsqrt(jnp.mean(x * x, axis=-1, keepdims=True) + EPS) * g


def _ada_kernel(c_ref, w_ref, b_ref, o_ref):
    c = _silu(c_ref[...]).astype(BF16)
    o_ref[...] = _dot(c, w_ref[...].astype(BF16)) + b_ref[...]


def _ada_mod(cvec, w_ada, b_ada):
    tn = 512
    n = w_ada.shape[-1]
    return pl.pallas_call(
        _ada_kernel,
        grid=(DEPTH, n // tn),
        in_specs=[pl.BlockSpec((MOD_ROWS, D_MODEL), lambda l, j: (0, 0)),
                  pl.BlockSpec((None, D_MODEL, tn), lambda l, j: (l, 0, j)),
                  pl.BlockSpec((None, 1, tn), lambda l, j: (l, 0, j))],
        out_specs=pl.BlockSpec((None, MOD_ROWS, tn), lambda l, j: (l, 0, j)),
        out_shape=jax.ShapeDtypeStruct((DEPTH, MOD_ROWS, n), F32),
        compiler_params=_cparams(("parallel", "parallel")),
        name="ada_mod",
    )(cvec, w_ada, b_ada.reshape(DEPTH, 1, n))


def _mod_spec(layer, chunk, tr, blk0=0):
    return pl.BlockSpec((None, None, 1, D_MODEL),
                        lambda i, *_: (layer, _row_group((blk0 + i) * tr), 0, chunk))


NORM_ROWS = 256
_VEC_SPEC = pl.BlockSpec((1, D_MODEL), lambda i: (0, 0))


def _row_spec(blk0=0):
    return pl.BlockSpec((NORM_ROWS, D_MODEL), lambda i: (blk0 + i, 0))


def _norm_mod_kernel(x_ref, g_ref, sc_ref, sh_ref, h_ref):
    y = _rms(x_ref[...], g_ref[...])
    h_ref[...] = (y * (1.0 + sc_ref[...]) + sh_ref[...]).astype(h_ref.dtype)


def _norm_mod(x_parts, g, mod, layer, sc_chunk, sh_chunk):
    tr = NORM_ROWS
    h = None
    for x, (row0, rows) in zip(x_parts, PIECES):
        blk0 = row0 // tr
        in_specs = [_row_spec(), _VEC_SPEC, _mod_spec(layer, sc_chunk, tr, blk0), _mod_spec(layer, sh_chunk, tr, blk0)]
        args = [x, g.reshape(1, D_MODEL), mod, mod]
        aliases = {}
        if h is not None:
            in_specs.append(pl.BlockSpec(memory_space=pl.ANY))
            aliases[len(args)] = 0
            args.append(h)
        h = pl.pallas_call(
            lambda x_ref, g_ref, sc_ref, sh_ref, *rest: _norm_mod_kernel(x_ref, g_ref, sc_ref, sh_ref, rest[-1]),
            grid=(rows // tr,),
            in_specs=in_specs,
            out_specs=_row_spec(blk0),
            out_shape=jax.ShapeDtypeStruct((T, D_MODEL), BF16),
            input_output_aliases=aliases,
            compiler_params=_cparams(("parallel",)),
            name="norm_mod",
        )(*args)
    return h


def _resid_kernel(x_ref, m_ref, ga_ref, gate_ref, *rest, emit_h):
    x = x_ref[...] + gate_ref[...] * _rms(m_ref[...].astype(F32), ga_ref[...])
    if emit_h:
        gb_ref, sc_ref, sh_ref, xo_ref, h_ref = rest
        xo_ref[...] = x
        y = _rms(x, gb_ref[...])
        h_ref[...] = (y * (1.0 + sc_ref[...]) + sh_ref[...]).astype(h_ref.dtype)
    else:
        (xo_ref,) = rest
        xo_ref[...] = x


def _resid(x, m, ga, mod, layer, gate_chunk, nxt=None, split_out=False):
    tr = NORM_ROWS
    x_parts = x if isinstance(x, (list, tuple)) else None
    pieces = PIECES if (x_parts is not None or split_out) else ((0, T),)
    emit_h = nxt is not None
    assert not (emit_h and split_out)
    x_new, h, outs = None, None, []
    for p, (row0, rows) in enumerate(pieces):
        blk0 = row0 // tr
        in_specs = [_row_spec(0 if x_parts is not None else blk0), _row_spec(blk0), _VEC_SPEC,
                    _mod_spec(layer, gate_chunk, tr, blk0)]
        args = [x_parts[p] if x_parts is not None else x, m, ga.reshape(1, D_MODEL), mod]
        if emit_h:
            gb, layer_b, sc_chunk, sh_chunk = nxt
            in_specs += [_VEC_SPEC, _mod_spec(layer_b, sc_chunk, tr, blk0), _mod_spec(layer_b, sh_chunk, tr, blk0)]
            args += [gb.reshape(1, D_MODEL), mod, mod]
        n_in = len(args)
        if split_out:
            out_specs = [_row_spec()]
            out_shape = [jax.ShapeDtypeStruct((rows, D_MODEL), F32)]
        else:
            out_specs = [_row_spec(blk0)]
            out_shape = [jax.ShapeDtypeStruct((T, D_MODEL), F32)]
        if emit_h:
            out_specs.append(_row_spec(blk0))
            out_shape.append(jax.ShapeDtypeStruct((T, D_MODEL), BF16))
        aliases = {}
        if x_new is not None and not split_out:
            for k, prev in enumerate((x_new, h) if emit_h else (x_new,)):
                in_specs.append(pl.BlockSpec(memory_space=pl.ANY))
                aliases[len(args)] = k
                args.append(prev)
        n_alias = len(aliases)

        def kern(*refs, n_in=n_in, n_alias=n_alias):
            _resid_kernel(*refs[:n_in], *refs[n_in + n_alias:], emit_h=emit_h)

        res = pl.pallas_call(
            kern,
            grid=(rows // tr,),
            in_specs=in_specs,
            out_specs=out_specs,
            out_shape=out_shape,
            input_output_aliases=aliases,
            compiler_params=_cparams(("parallel",)),
            name="resid_norm",
        )(*args)
        x_new = res[0]
        h = res[1] if emit_h else None
        outs.append(res[0])
    return (outs if split_out else x_new), h


MM_TILE_M = 1024


def _mm_kernel(a_ref, b_ref, o_ref, *, n_valid):
    acc = _dot(a_ref[...], b_ref[...])
    tn = o_ref.shape[1]
    if n_valid % tn:
        col = lax.broadcasted_iota(jnp.int32, acc.shape, 1)
        acc = jnp.where(col < n_valid - pl.program_id(1) * tn, acc, 0.0)
    o_ref[...] = acc.astype(o_ref.dtype)


def _matmul(a, w, layer, n_out, out_dtype, tn, name):
    m, k = a.shape
    tm = MM_TILE_M
    return pl.pallas_call(
        functools.partial(_mm_kernel, n_valid=w.shape[2]),
        grid=(m // tm, n_out // tn),
        in_specs=[pl.BlockSpec((tm, k), lambda i, j: (i, 0)),
                  pl.BlockSpec((None, k, tn), lambda i, j: (layer, 0, j))],
        out_specs=pl.BlockSpec((tm, tn), lambda i, j: (i, j)),
        out_shape=jax.ShapeDtypeStruct((m, n_out), out_dtype),
        compiler_params=_cparams(("parallel", "parallel")),
        name=name,
    )(a, w)


def _mm4_kernel(a0_ref, a1_ref, a2_ref, a3_ref, b_ref, o_ref):
    acc = _dot(a0_ref[...], b_ref[0 * GROUP_W:1 * GROUP_W, :])
    acc += _dot(a1_ref[...], b_ref[1 * GROUP_W:2 * GROUP_W, :])
    acc += _dot(a2_ref[...], b_ref[2 * GROUP_W:3 * GROUP_W, :])
    acc += _dot(a3_ref[...], b_ref[3 * GROUP_W:4 * GROUP_W, :])
    o_ref[...] = acc.astype(o_ref.dtype)


def _out_proj(parts, w, layer):
    tm, tn = MM_TILE_M, 1024
    a_spec = pl.BlockSpec((tm, GROUP_W), lambda i, j: (i, 0))
    return pl.pallas_call(
        _mm4_kernel,
        grid=(T // tm, D_MODEL // tn),
        in_specs=[a_spec, a_spec, a_spec, a_spec,
                  pl.BlockSpec((None, 4 * GROUP_W, tn), lambda i, j: (layer, 0, j))],
        out_specs=pl.BlockSpec((tm, tn), lambda i, j: (i, j)),
        out_shape=jax.ShapeDtypeStruct((T, D_MODEL), BF16),
        compiler_params=_cparams(("parallel", "parallel")),
        name="out_proj",
    )(*parts, w)


def _mm_ktiled_kernel(a_ref, b_ref, o_ref, acc_ref):
    k = pl.program_id(2)
    last = pl.num_programs(2) - 1
    d = _dot(a_ref[...], b_ref[...])

    @pl.when(k == 0)
    def _():
        acc_ref[...] = d

    @pl.when(jnp.logical_and(k != 0, k != last))
    def _():
        acc_ref[...] += d

    @pl.when(k == last)
    def _():
        o_ref[...] = (acc_ref[...] + d).astype(o_ref.dtype)


def _ffn_down(a, w, layer):
    m, kk = a.shape
    n = w.shape[2]
    tm, tn, tk = MM_TILE_M, 512, kk // 2
    assert tk % LANE == 0
    return pl.pallas_call(
        _mm_ktiled_kernel,
        grid=(m // tm, n // tn, kk // tk),
        in_specs=[pl.BlockSpec((tm, tk), lambda i, j, k: (i, k)),
                  pl.BlockSpec((None, tk, tn), lambda i, j, k: (layer, k, j))],
        out_specs=pl.BlockSpec((tm, tn), lambda i, j, k: (i, j)),
        out_shape=jax.ShapeDtypeStruct((m, n), BF16),
        scratch_shapes=[pltpu.VMEM((tm, tn), F32)],
        compiler_params=_cparams(("parallel", "parallel", "arbitrary")),
        name="ffn_down",
    )(a, w)


FFN_TILE_M = 2048
FFN_TILE_N = 256
FFN_SUB_M = 256
HALO_ROWS = 16
assert D_FF % FFN_TILE_N == 0


def _ffn_up_kernel(h_ref, halo_ref, wg_ref, wv_ref, cw_ref, cb_ref, o_ref):
    tm, tn = o_ref.shape
    sub = FFN_SUB_M
    ns = tm // sub
    wg, wv = wg_ref[...], wv_ref[...]
    row0 = pl.program_id(0) * tm
    seq_len = jnp.where(row0 < TP, SEQ, DEC_SEQ)
    r = lax.broadcasted_iota(jnp.int32, (sub, tn), 0)
    g0 = _dot(jnp.concatenate([halo_ref[...], h_ref[0:sub, :]], axis=0), wg)
    gh = g0[0:HALO_ROWS]
    gs = [g0[HALO_ROWS:]] + [_dot(h_ref[s * sub:(s + 1) * sub, :], wg) for s in range(1, ns)]
    for s in range(ns):
        val = _dot(h_ref[s * sub:(s + 1) * sub, :], wv)
        g = gs[s]
        before = gh[0:1, :] if s == 0 else gs[s - 1][sub - 1:sub, :]
        after = gh[1:2, :] if s == ns - 1 else gs[s + 1][0:1, :]
        pos = (row0 + s * sub + r) & (seq_len - 1)
        prev = jnp.where(r == 0, before, pltpu.roll(g, 1, 0))
        prev = jnp.where(pos == 0, 0.0, prev)
        nxt = jnp.where(r == sub - 1, after, pltpu.roll(g, sub - 1, 0))
        nxt = jnp.where(pos == seq_len - 1, 0.0, nxt)
        conv = prev * cw_ref[0:1, :] + g * cw_ref[1:2, :] + nxt * cw_ref[2:3, :] + cb_ref[...]
        o_ref[s * sub:(s + 1) * sub, :] = (_gelu(conv) * val).astype(o_ref.dtype)


def _ffn_up(h, halo, w_up, conv_w, conv_b, layer):
    tm, tn = FFN_TILE_M, FFN_TILE_N
    nj = D_FF // tn
    return pl.pallas_call(
        _ffn_up_kernel,
        grid=(T // tm, nj),
        in_specs=[pl.BlockSpec((tm, D_MODEL), lambda i, j: (i, 0)),
                  pl.BlockSpec((None, HALO_ROWS, D_MODEL), lambda i, j: (i, 0, 0)),
                  pl.BlockSpec((None, D_MODEL, tn), lambda i, j: (layer, 0, j)),
                  pl.BlockSpec((None, D_MODEL, tn), lambda i, j: (layer, 0, nj + j)),
                  pl.BlockSpec((None, CONV_W, tn), lambda i, j: (layer, 0, j)),
                  pl.BlockSpec((None, 1, tn), lambda i, j: (layer, 0, j))],
        out_specs=pl.BlockSpec((tm, tn), lambda i, j: (i, j)),
        out_shape=jax.ShapeDtypeStruct((T, D_FF), BF16),
        compiler_params=_cparams(("parallel", "parallel")),
        name="ffn_up",
    )(h, halo, w_up, w_up, conv_w, conv_b.reshape(DEPTH, 1, D_FF))


def _conv_halo(h):
    tm = FFN_TILE_M
    nt = T // tm
    ht = h.reshape(nt, tm, D_MODEL)
    zero = jnp.zeros((1, D_MODEL), h.dtype)
    before = jnp.concatenate([zero, ht[:-1, tm - 1]], axis=0)
    after = jnp.concatenate([ht[1:, 0], zero], axis=0)
    pad = jnp.zeros((nt, HALO_ROWS - 2, D_MODEL), h.dtype)
    return jnp.concatenate([before[:, None], after[:, None], pad], axis=1)


def _sgu_kernel(u_ref, v_ref, g_ref, b_ref, w_ref, bias_ref, o_ref):
    u = _gelu(u_ref[...])
    v = _gelu(v_ref[...])
    vc = v - jnp.mean(v, axis=-1, keepdims=True)
    vn = vc * lax.rsqrt(jnp.mean(vc * vc, axis=-1, keepdims=True) + EPS) * g_ref[...] + b_ref[...]
    vb = vn.astype(BF16)
    for g in range(SGU_GROUPS):
        sl = slice(g * SGU_GCH, (g + 1) * SGU_GCH)
        mixed = _dot(w_ref[g], vb[:, sl]) + bias_ref[:, sl]
        o_ref[:, sl] = (u[:, sl] * mixed).astype(o_ref.dtype)


def _sgu(z, ln_g, ln_b, w_s, b_s):
    c = SGU_CHUNK
    bias = jnp.repeat(b_s.T, SGU_GCH, axis=1)
    vec = pl.BlockSpec((1, SGU_DIM), lambda i: (0, 0))
    return pl.pallas_call(
        _sgu_kernel,
        grid=(T // c,),
        in_specs=[pl.BlockSpec((c, SGU_DIM), lambda i: (i, OFF_SGU // SGU_DIM)),
                  pl.BlockSpec((c, SGU_DIM), lambda i: (i, OFF_SGU // SGU_DIM + 1)),
                  vec, vec,
                  pl.BlockSpec((SGU_GROUPS, c, c), lambda i: (0, 0, 0)),
                  pl.BlockSpec((c, SGU_DIM), lambda i: (0, 0))],
        out_specs=pl.BlockSpec((c, SGU_DIM), lambda i: (i, 0)),
        out_shape=jax.ShapeDtypeStruct((T, SGU_DIM), BF16),
        compiler_params=_cparams(("parallel",)),
        name="sgu",
    )(z, z, ln_g.reshape(1, SGU_DIM), ln_b.reshape(1, SGU_DIM), w_s.astype(BF16), bias)


SMALL_LEVELS = tuple(m for m in (1, 2, 4) if m < SUBLANE)
BIG_LEVELS = tuple(m for m in (8, 16, 32, 64) if m < SCAN_C)


def _scan_consts():
    c = SCAN_C
    t = np.arange(c)[:, None]
    s = np.arange(c)[None, :]
    x = t ^ s
    tri = np.stack([s <= t, s >= t]).astype(np.float32)
    pair = [np.stack([x == 0, x == 0])]
    for m in SMALL_LEVELS:
        lvl = (x >= m) & (x < 2 * m)
        pair.append(np.stack([lvl & (t > s), lvl & (t < s)]))
    pair = np.stack(pair).astype(np.float32)
    rowq = np.stack([np.broadcast_to(((np.arange(c) & m) != 0)[:, None], (c, LANE)) for m in SMALL_LEVELS])
    hb = np.arange(c // 2)
    blk = np.stack([(hb[:, None] // m) == (hb[None, :] // m) for m in BIG_LEVELS])
    return (jnp.asarray(tri, BF16), jnp.asarray(pair, F32), jnp.asarray(rowq.astype(np.float32), F32),
            jnp.asarray(blk.astype(np.float32), F32))


def _split3(x):
    hi = x.astype(BF16)
    r = x - hi.astype(F32)
    mid = r.astype(BF16)
    lo = (r - mid.astype(F32)).astype(BF16)
    return hi, mid, lo


def _halves(a, m, second):
    off = m if second else 0
    return jnp.concatenate([a[j + off:j + off + m] for j in range(0, a.shape[0], 2 * m)], axis=0)


def _chunk_scans(probs, consts):
    tri_ref, pair_ref, rowq_ref, blk_ref = consts
    c, dk = probs[0][0].shape
    dirs = [1 if p[5] else 0 for p in probs]
    vbs = [p[2].astype(BF16) for p in probs]
    diag = [pair_ref[0, d] * _dot_nt(p[0].astype(BF16), p[1].astype(BF16)) for p, d in zip(probs, dirs)]
    bs = []
    for p, d in zip(probs, dirs):
        hi, mid, lo = _split3(p[3])
        tri = tri_ref[d]
        bs.append(_dot(tri, hi) + _dot(tri, mid) + _dot(tri, lo))
    inter = [_dot_nt((p[0] * jnp.exp2(b)).astype(BF16), p[4].astype(BF16)) for p, b in zip(probs, bs)]

    big = [[] for _ in probs]
    for li, m in enumerate(BIG_LEVELS):
        for pi, ((q, k, _, _, _, reverse), b) in enumerate(zip(probs, bs)):
            q_second = not reverse
            refs = [b[j + m:j + m + 1] if reverse else b[j + m - 1:j + m] for j in range(0, c, 2 * m)]
            ref = jnp.concatenate([jnp.broadcast_to(r, (m, dk)) for r in refs], axis=0)
            qf = _halves(q, m, q_second) * jnp.exp2(_halves(b, m, q_second) - ref)
            kf = _halves(k, m, not q_second) * jnp.exp2(ref - _halves(b, m, not q_second))
            s_l = _dot_nt(qf.astype(BF16), kf.astype(BF16))
            if 2 * m < c:
                s_l = s_l * blk_ref[li]
            big[pi].append(s_l.astype(BF16))

    scores = diag
    es = list(bs)
    for li, m in enumerate(SMALL_LEVELS):
        for pi, ((q, k, _, _, _, reverse), b) in enumerate(zip(probs, bs)):
            is_q = (rowq_ref[li] < 0.5) if reverse else (rowq_ref[li] > 0.5)
            sh_q, sh_e = (c - m, m) if reverse else (m, c - m)
            fac = jnp.exp2(jnp.where(is_q, b - pltpu.roll(es[pi], sh_q, 0), es[pi] - b))
            u = (jnp.where(is_q, q, k) * fac).astype(BF16)
            scores[pi] = scores[pi] + pair_ref[1 + li, dirs[pi]] * _dot_nt(u, u)
            if li + 1 < len(SMALL_LEVELS):
                es[pi] = jnp.where(is_q, es[pi], pltpu.roll(es[pi], sh_e, 0))

    outs = []
    for pi, ((q, k, _, _, st, reverse), b) in enumerate(zip(probs, bs)):
        o = inter[pi] + _dot(scores[pi].astype(BF16), vbs[pi])
        parts = [o[g:g + SUBLANE] for g in range(0, c, SUBLANE)]
        q_second = not reverse
        for li, m in enumerate(BIG_LEVELS):
            oc = _dot(big[pi][li], _halves(vbs[pi], m, not q_second))
            for jj, j in enumerate(range(0, c, 2 * m)):
                for g in range(0, m, SUBLANE):
                    dst = (j + (m if q_second else 0) + g) // SUBLANE
                    parts[dst] = parts[dst] + oc[jj * m + g:jj * m + g + SUBLANE]
        bl = b[0:1, :] if reverse else b[c - 1:c, :]
        kd = (k * jnp.exp2(bl - b)).astype(BF16)
        st_new = jnp.exp2(bl) * st + _dot_tn(vbs[pi], kd)
        outs.append((jnp.concatenate(parts, axis=0), st_new))
    return outs


def _scan_kernel(*refs, features, n_in, seq, nseg, hpb, dv, with_s0, emit_state):
    consts = refs[:4]
    pos = 4
    dir_refs = []
    for _ in range(1 if nseg == 1 else 2):
        dir_refs.append((refs[pos:pos + n_in], refs[pos + n_in]))
        pos += n_in + 1
    dir_refs = dir_refs * 2 if nseg == 1 else dir_refs
    gain_ref = refs[pos]
    pos += 1
    s0_ref = None
    if with_s0:
        s0_ref = refs[pos]
        pos += 1
    o_ref = refs[pos]
    pos += 1
    sf_ref = None
    if emit_state:
        sf_ref = refs[pos]
        pos += 1
    o_scr, st_scr = refs[pos:pos + 2]
    c = SCAN_C
    n = seq // c
    cps = n // nseg
    g = pl.program_id(2) if nseg > 1 else 0
    work = [(hh, reverse) for hh in range(hpb) for reverse in (False, True)]

    def init():
        st_scr[...] = s0_ref[...] if with_s0 else jnp.zeros(st_scr.shape, F32)

    def step(i, finalize):
        loc = {False: pl.multiple_of(i * c, c), True: pl.multiple_of((cps - 1 - i) * c, c)}
        pair = g * cps + i
        glob = {False: pl.multiple_of(pair * c, c), True: pl.multiple_of((n - 1 - pair) * c, c)}
        probs = [features(dir_refs[rev][0], loc[rev], hh, rev) + (st_scr[1 if rev else 0, hh], rev)
                 for hh, rev in work]
        for (hh, rev), (o, st) in zip(work, _chunk_scans(probs, consts)):
            st_scr[1 if rev else 0, hh] = st
            rows, cols = pl.ds(glob[rev], c), slice(hh * dv, (hh + 1) * dv)
            if finalize:
                gate = dir_refs[rev][1][pl.ds(loc[rev], c), cols]
                y = _rms(o + o_scr[rows, cols], gain_ref[...]) * _silu(gate)
                o_ref[rows, cols] = y.astype(o_ref.dtype)
            else:
                o_scr[rows, cols] = o

    def loop(lo, hi, finalize):
        lax.fori_loop(lo, hi, lambda i, carry: (step(i, finalize), carry)[1], 0)

    if nseg == 1:
        init()
        loop(0, n // 2, False)
        loop(n // 2, n, True)
    else:
        pl.when(g == 0)(init)
        pl.when(g < nseg // 2)(lambda: loop(0, cps, False))
        pl.when(g >= nseg // 2)(lambda: loop(0, cps, True))
    if emit_state:
        assert nseg == 1
        for d in range(2):
            for hh in range(hpb):
                sf_ref[d, hh] = st_scr[d, hh].T


def _hgrn_features(in_refs, r0, head, reverse):
    hq_ref, hf_ref, hb_ref, hi_ref, lb_ref = in_refs
    rows, cols = pl.ds(r0, SCAN_C), slice(head * HG_DK, (head + 1) * HG_DK)
    q = _silu(hq_ref[rows, cols]) * (HG_DK ** -0.5)
    lb = lb_ref[1 if reverse else 0][:, cols]
    f = lb + (1.0 - lb) * _sigmoid((hb_ref if reverse else hf_ref)[rows, cols])
    return q, 1.0 - f, hi_ref[rows, head * HG_DV:(head + 1) * HG_DV], jnp.log(f) * LOG2E


def _gla_features(in_refs, r0, head, reverse):
    gq_ref, gk_ref, gv_ref, low_ref, wgk_ref, bgk_ref = in_refs
    rows, cols = pl.ds(r0, SCAN_C), slice(head * GLA_DK, (head + 1) * GLA_DK)
    d = 1 if reverse else 0
    g = _dot(low_ref[rows, :].astype(BF16), wgk_ref[d][:, cols]) + bgk_ref[d][:, cols]
    la = (jnp.minimum(g, 0.0) - jnp.log(1.0 + jnp.exp(-jnp.abs(g)))) * (LOG2E / GLA_NORMALIZER)
    return (gq_ref[rows, cols] * (GLA_DK ** -0.5), gk_ref[rows, cols],
            gv_ref[rows, head * GLA_DV:(head + 1) * GLA_DV], la)


def _seq_geom(latent):
    seq = DEC_SEQ if latent else SEQ
    return seq, (DEC_BATCH if latent else BATCH), (TP // seq if latent else 0)


def _zspec(latent, off, width, per_head=True):
    seq, _, blk0 = _seq_geom(latent)
    assert off % width == 0
    if per_head:
        return pl.BlockSpec((seq, width), lambda b, h, *_: (blk0 + b, off // width + h))
    return pl.BlockSpec((seq, width), lambda b, h, *_: (blk0 + b, off // width))


def _const_spec(a):
    return pl.BlockSpec(a.shape, lambda *_, nd=a.ndim: (0,) * nd)


def _scan_geom(latent):
    return (2, 8) if latent else (2, 1)


def _zseg_spec(latent, off, width, reverse, per_head=True):
    seq, _, blk0 = _seq_geom(latent)
    nseg = _scan_geom(latent)[1]
    assert off % width == 0
    col0 = off // width

    def index(b, h, g=0):
        seg = (nseg - 1 - g) if reverse else g
        return ((blk0 + b) * nseg + seg, col0 + (h if per_head else 0))

    return pl.BlockSpec((seq // nseg, width), index)


def _scan_call(name, features, feat_fn, z, gate_off, gain, heads, dk, dv,
               latent, s0t, prev_out, prev_state, layer):
    seq, nb, row_blk0 = _seq_geom(latent)
    hpb, nseg = _scan_geom(latent)
    consts = _scan_consts()
    in_specs = [_const_spec(a) for a in consts]
    args = list(consts)
    for reverse in ((False,) if nseg == 1 else (False, True)):
        specs, fargs = feat_fn(reverse)
        n_in = len(specs)
        in_specs += list(specs) + [_zseg_spec(latent, gate_off, hpb * dv, reverse)]
        args += list(fargs) + [z]
    in_specs.append(pl.BlockSpec((1, dv), lambda *_: (0, 0)))
    args.append(gain.reshape(1, dv))
    if latent:
        in_specs.append(pl.BlockSpec((None, None, 2, hpb, dv, dk), lambda b, h, *_: (b, layer, 0, h, 0, 0)))
        args.append(s0t)
    out_specs = [pl.BlockSpec((seq, hpb * dv), lambda b, h, *_: (row_blk0 + b, h))]
    out_shape = [jax.ShapeDtypeStruct((T, heads * dv), BF16)]
    aliases = {}
    if prev_out is not None:
        in_specs.append(pl.BlockSpec(memory_space=pl.ANY))
        aliases[len(args)] = 0
        args.append(prev_out)
    emit_state = not latent
    if emit_state:
        out_specs.append(pl.BlockSpec((None, None, 2, hpb, dk, dv), lambda b, h, *_: (b, layer, 0, h, 0, 0)))
        out_shape.append(jax.ShapeDtypeStruct((BATCH, DEPTH, 2, heads, dk, dv), F32))
        if prev_state is not None:
            in_specs.append(pl.BlockSpec(memory_space=pl.ANY))
            aliases[len(args)] = 1
            args.append(prev_state)
    n_alias = len(aliases)

    def kern(*refs):
        n_inputs = len(args) - n_alias
        keep = refs[:n_inputs] + refs[n_inputs + n_alias:]
        _scan_kernel(*keep, features=features, n_in=n_in, seq=seq, nseg=nseg, hpb=hpb, dv=dv, with_s0=latent,
                     emit_state=emit_state)

    grid = (nb, heads // hpb) + ((nseg,) if nseg > 1 else ())
    sem = ("parallel", "parallel") + (("arbitrary",) if nseg > 1 else ())
    res = pl.pallas_call(
        kern,
        grid=grid,
        in_specs=in_specs,
        out_specs=out_specs,
        out_shape=out_shape,
        input_output_aliases=aliases,
        scratch_shapes=[pltpu.VMEM((seq, hpb * dv), F32), pltpu.VMEM((2, hpb, dv, dk), F32)],
        compiler_params=_cparams(sem),
        name=name,
    )(*args)
    return (res[0], res[1]) if emit_state else (res[0], None)


def _hgrn(z, lb, gain, s0t, prev_state, layer):
    lb3 = lb.reshape(2, 1, HG_K)
    o = state = None
    for latent in (False, True):
        hpb = _scan_geom(latent)[0]

        def feat_fn(reverse, latent=latent, hpb=hpb):
            zs = functools.partial(_zseg_spec, latent, reverse=reverse)
            return ([zs(OFF_HQ, hpb * HG_DK), zs(OFF_HFF, hpb * HG_DK), zs(OFF_HFB, hpb * HG_DK),
                     zs(OFF_HI, hpb * HG_DV), pl.BlockSpec((2, 1, hpb * HG_DK), lambda b, h, *_: (0, 0, h))],
                    [z, z, z, z, lb3])

        o, st = _scan_call("hgrn_latent" if latent else "hgrn_prompt", _hgrn_features, feat_fn,
                           z, OFF_HG, gain, HG_HEADS, HG_DK, HG_DV, latent, s0t, o, prev_state, layer)
        state = st if st is not None else state
    return o, state


def _gla(z, w_gk, b_gk, gain, s0t, prev_state, layer):
    wpad = jnp.zeros((2, LANE, GLA_HEADS * GLA_DK), F32)
    wpad = wpad.at[0, 0:GLA_RANK].set(w_gk[0]).at[1, GLA_RANK:2 * GLA_RANK].set(w_gk[1]).astype(BF16)
    b3 = b_gk.reshape(2, 1, GLA_HEADS * GLA_DK)
    o = state = None
    for latent in (False, True):
        hpb = _scan_geom(latent)[0]

        def feat_fn(reverse, latent=latent, hpb=hpb):
            zs = functools.partial(_zseg_spec, latent, reverse=reverse)
            return ([zs(OFF_GQ, hpb * GLA_DK), zs(OFF_GK, hpb * GLA_DK), zs(OFF_GV, hpb * GLA_DV),
                     zs(OFF_GLF, LANE, per_head=False),
                     pl.BlockSpec((2, LANE, hpb * GLA_DK), lambda b, h, *_: (0, 0, h)),
                     pl.BlockSpec((2, 1, hpb * GLA_DK), lambda b, h, *_: (0, 0, h))],
                    [z, z, z, z, wpad, b3])

        o, st = _scan_call("gla_latent" if latent else "gla_prompt", _gla_features, feat_fn,
                           z, OFF_GG, gain, GLA_HEADS, GLA_DK, GLA_DV, latent, s0t, o, prev_state, layer)
        state = st if st is not None else state
    return o, state


GQA = N_HEADS // N_KV_HEADS


def _ctx_attn_kernel(q_ref, k_ref, v_ref, sink_ref, o_ref):
    hd = HEAD_DIM
    k = k_ref[...].astype(BF16)
    v = v_ref[...].astype(BF16)
    heads = range(GQA)
    ss = [_dot_nt((q_ref[:, g * hd:(g + 1) * hd] * (hd ** -0.5)).astype(BF16), k) for g in heads]
    ps, dens = [], []
    for g in heads:
        sink = sink_ref[g][:, 0:1]
        m = jnp.maximum(jnp.max(ss[g], axis=-1, keepdims=True), sink)
        p = jnp.exp(ss[g] - m)
        dens.append(jnp.sum(p, axis=-1, keepdims=True) + jnp.exp(sink - m))
        ps.append(p.astype(BF16))
    for g in heads:
        o_ref[:, g * hd:(g + 1) * hd] = (_dot(ps[g], v) / dens[g]).astype(o_ref.dtype)


def _ctx_attention(z, sink3):
    hd = HEAD_DIM
    return pl.pallas_call(
        _ctx_attn_kernel,
        grid=(BATCH, N_KV_HEADS),
        in_specs=[_zspec(False, OFF_AQ, GQA * hd),
                  pl.BlockSpec((SEQ, hd), lambda b, kh: (b, OFF_AK // hd + kh)),
                  pl.BlockSpec((SEQ, hd), lambda b, kh: (b, OFF_AV // hd + kh)),
                  pl.BlockSpec((GQA, 1, LANE), lambda b, kh: (kh, 0, 0))],
        out_specs=pl.BlockSpec((SEQ, GQA * hd), lambda b, kh: (b, kh)),
        out_shape=jax.ShapeDtypeStruct((T, N_HEADS * hd), BF16),
        compiler_params=_cparams(("parallel", "parallel")),
        name="ctx_attention",
    )(z, z, z, sink3)


def _rope(x, cos_f, sin_f):
    return x * cos_f + pltpu.roll(x, HEAD_DIM // 2, 1) * sin_f


def _lat_attn_kernel(q_ref, k_ref, v_ref, ck_ref, cv_ref, cos_ref, sin_ref, sink_ref, o_ref, kr_scr, vr_scr):
    blk, hd, seq = ATT_BLOCK, HEAD_DIM, DEC_SEQ
    nb = seq // blk

    zeros = jnp.zeros((blk, hd), BF16)
    for scr in (kr_scr, vr_scr):
        scr[0:blk, :] = zeros
        scr[blk + seq:2 * blk + seq, :] = zeros

    def fill(n, carry):
        r0 = pl.multiple_of(n * blk, blk)
        rows = pl.ds(r0, blk)
        kr_scr[pl.ds(r0 + blk, blk), :] = _rope(k_ref[rows, :], cos_ref[rows, :], sin_ref[rows, :]).astype(BF16)
        vr_scr[pl.ds(r0 + blk, blk), :] = v_ref[rows, :].astype(BF16)
        return carry

    lax.fori_loop(0, nb, fill, 0)

    ck = ck_ref[...].astype(BF16)
    cv = cv_ref[...].astype(BF16)
    sink = jnp.concatenate([jnp.broadcast_to(sink_ref[g][:, 0:1], (blk, 1)) for g in range(GQA)], axis=0)
    qi = lax.broadcasted_iota(jnp.int32, (GQA * blk, 3 * blk), 0) & (blk - 1)
    kj = lax.broadcasted_iota(jnp.int32, (GQA * blk, 3 * blk), 1)
    window_bias = jnp.where(jnp.abs(kj - qi - blk) <= WINDOW, 0.0, -jnp.inf)

    def qblock(n, carry):
        r0 = pl.multiple_of(n * blk, blk)
        rows = pl.ds(r0, blk)
        cos_f, sin_f = cos_ref[rows, :], sin_ref[rows, :]
        q = jnp.concatenate([(_rope(q_ref[rows, g * hd:(g + 1) * hd], cos_f, sin_f) * (hd ** -0.5)).astype(BF16)
                             for g in range(GQA)], axis=0)
        band = pl.ds(r0, 3 * blk)
        key_pos = kj + (n - 1) * blk
        s_loc = _dot_nt(q, kr_scr[band, :]) + window_bias
        s_loc = jnp.where(key_pos >= 0, s_loc, -jnp.inf)
        s_loc = jnp.where(key_pos < seq, s_loc, -jnp.inf)
        s_ctx = _dot_nt(q, ck)
        m = jnp.maximum(jnp.maximum(jnp.max(s_loc, axis=-1, keepdims=True),
                                    jnp.max(s_ctx, axis=-1, keepdims=True)), sink)
        p_loc = jnp.exp(s_loc - m)
        p_ctx = jnp.exp(s_ctx - m)
        den = (jnp.sum(p_loc, axis=-1, keepdims=True) + jnp.sum(p_ctx, axis=-1, keepdims=True)
               + jnp.exp(sink - m))
        o = (_dot(p_ctx.astype(BF16), cv) + _dot(p_loc.astype(BF16), vr_scr[band, :])) / den
        for g in range(GQA):
            o_ref[rows, g * hd:(g + 1) * hd] = o[g * blk:(g + 1) * blk].astype(o_ref.dtype)
        return carry

    lax.fori_loop(0, nb, qblock, 0)


def _rope_tables():
    n = jnp.arange(DEC_SEQ)
    row = (n // GRID_W).astype(F32)
    col = (n % GRID_W).astype(F32)
    n_freq = HEAD_DIM // 4
    inv = ROPE_THETA ** (-jnp.arange(n_freq, dtype=F32) / n_freq)
    ang = jnp.concatenate([row[:, None] * inv, col[:, None] * inv], axis=-1)
    cos, sin = jnp.cos(ang), jnp.sin(ang)
    return jnp.concatenate([cos, cos], axis=-1), jnp.concatenate([-sin, sin], axis=-1)


def _lat_attention(z, ck, cv, sink3, prev_out):
    hd = HEAD_DIM
    seq, _, blk0 = _seq_geom(True)
    cos_f, sin_f = _rope_tables()
    tab = pl.BlockSpec((seq, hd), lambda b, kh: (0, 0))
    cache = pl.BlockSpec((None, PAST_LEN, hd), lambda b, kh: (b, 0, kh))
    return pl.pallas_call(
        lambda *refs: _lat_attn_kernel(*refs[:8], *refs[9:]),
        grid=(DEC_BATCH, N_KV_HEADS),
        in_specs=[_zspec(True, OFF_AQ, GQA * hd),
                  pl.BlockSpec((seq, hd), lambda b, kh: (blk0 + b, OFF_AK // hd + kh)),
                  pl.BlockSpec((seq, hd), lambda b, kh: (blk0 + b, OFF_AV // hd + kh)),
                  cache, cache, tab, tab,
                  pl.BlockSpec((GQA, 1, LANE), lambda b, kh: (kh, 0, 0)),
                  pl.BlockSpec(memory_space=pl.ANY)],
        out_specs=pl.BlockSpec((seq, GQA * hd), lambda b, kh: (blk0 + b, kh)),
        out_shape=jax.ShapeDtypeStruct((T, N_HEADS * hd), BF16),
        input_output_aliases={8: 0},
        scratch_shapes=[pltpu.VMEM((seq + 2 * ATT_BLOCK, hd), BF16), pltpu.VMEM((seq + 2 * ATT_BLOCK, hd), BF16)],
        compiler_params=_cparams(("parallel", "parallel")),
        name="lat_attention",
    )(z, z, z, ck, cv, cos_f, sin_f, sink3, prev_out)


def _pad_cols(w, n):
    return jnp.pad(w, ((0, 0), (0, n - w.shape[1])))


def kernel(x_prompt, x_sample, cache_k, cache_v, state_hgrn, state_gla, c, c_ctx, w_ada, b_ada, norm_g, w_in,
           w_out, sgu_ln_g, sgu_ln_b, sgu_w, sgu_b, hgrn_lb_logits, hgrn_norm_g, attn_sink, gla_w_gk, gla_b_gk,
           gla_norm_g, ffn_w_up, ffn_conv_w, ffn_conv_b, ffn_w_down):
    assert TP % DEC_SEQ == 0 and DEC_BATCH + 1 <= MOD_ROWS
    lb_all = jnp.cumsum(jax.nn.softmax(hgrn_lb_logits.astype(F32), axis=0), axis=0)
    lb_all = lb_all - lb_all[0:1]

    cvec = jnp.zeros((MOD_ROWS, D_MODEL), F32).at[0].set(c_ctx).at[1:1 + DEC_BATCH].set(c)
    mod = _ada_mod(cvec, w_ada, b_ada).reshape(DEPTH, MOD_ROWS, 1, 6 * D_MODEL)
    SH1, SC1, G1, SH2, SC2, G2 = range(6)

    x = [x_prompt.reshape(TP, D_MODEL), x_sample.reshape(TL, D_MODEL)]
    hg_s0t = jnp.swapaxes(state_hgrn, -1, -2)
    gl_s0t = jnp.swapaxes(state_gla, -1, -2)
    ck_all = cache_k.reshape(DEC_BATCH, DEPTH, PAST_LEN, N_KV_HEADS * HEAD_DIM)
    cv_all = cache_v.reshape(DEC_BATCH, DEPTH, PAST_LEN, N_KV_HEADS * HEAD_DIM)

    h = _norm_mod(x, norm_g[0, 0], mod, 0, SC1, SH1)
    ks_new, vs_new = [], []
    hg_state = gl_state = None
    w_in_b, w_out_b = w_in.astype(BF16), w_out.astype(BF16)
    w_up_b, w_down_b = ffn_w_up.astype(BF16), ffn_w_down.astype(BF16)
    for l in range(DEPTH):
        sink3 = jnp.broadcast_to(attn_sink[l].astype(F32)[:, None, None], (N_HEADS, 1, LANE))

        z = _matmul(h, w_in_b, l, Z_COLS, F32, Z_TILE_N, "in_proj")
        o_sgu = _sgu(z, sgu_ln_g[l], sgu_ln_b[l], sgu_w[l], sgu_b[l])
        o_hg, hg_state = _hgrn(z, lb_all[l], hgrn_norm_g[l], hg_s0t, hg_state, l)
        o_att = _ctx_attention(z, sink3)
        o_att = _lat_attention(z, ck_all[:, l], cv_all[:, l], sink3, o_att)
        o_gl, gl_state = _gla(z, gla_w_gk[l], gla_b_gk[l], gla_norm_g[l], gl_s0t, gl_state, l)
        m = _out_proj([o_sgu, o_hg, o_att, o_gl], w_out_b, l)
        x, h = _resid(x, m, norm_g[l, 1], mod, l, G1, nxt=(norm_g[l, 2], l, SC2, SH2))
        ks_new.append(z[:TP, OFF_AK:OFF_AK + N_KV_HEADS * HEAD_DIM].reshape(BATCH, SEQ, N_KV_HEADS, HEAD_DIM))
        vs_new.append(z[:TP, OFF_AV:OFF_AV + N_KV_HEADS * HEAD_DIM].reshape(BATCH, SEQ, N_KV_HEADS, HEAD_DIM))

        act = _ffn_up(h, _conv_halo(h), w_up_b, ffn_conv_w, ffn_conv_b, l)
        f = _ffn_down(act, w_down_b, l)
        if l + 1 < DEPTH:
            x, h = _resid(x, f, norm_g[l, 3], mod, l, G2, nxt=(norm_g[l + 1, 0], l + 1, SC1, SH1))
        else:
            x, h = _resid(x, f, norm_g[l, 3], mod, l, G2, split_out=True)

    y_prompt = x[0].reshape(BATCH, SEQ, D_MODEL)
    y_sample = x[1].reshape(DEC_BATCH, DEC_SEQ, D_MODEL)
    return (y_prompt, y_sample, jnp.stack(ks_new, axis=1), jnp.stack(vs_new, axis=1), hg_state, gl_state)
```

```python
import functools

import numpy as np
import jax
import jax.numpy as jnp
from jax import lax
from jax.experimental import pallas as pl
from jax.experimental.pallas import tpu as pltpu

D_MODEL = 4096
BATCH = 32
SEQ = 256
DEPTH = 2
DEC_BATCH = 2
DEC_SEQ = 4096
PAST_LEN = 512
GRID_W = 64
GROUP_W = D_MODEL // 4
SGU_CHUNK = 128
SGU_GROUPS = 4
SGU_DIM = GROUP_W
SGU_GCH = SGU_DIM // SGU_GROUPS
HG_HEADS = 8
HG_DK = 128
HG_DV = GROUP_W // HG_HEADS
HG_K = HG_HEADS * HG_DK
N_HEADS = 8
N_KV_HEADS = 2
HEAD_DIM = GROUP_W // N_HEADS
WINDOW = 128
ATT_BLOCK = 128
ROPE_THETA = 10000.0
GLA_HEADS = 4
GLA_DK = 128
GLA_DV = GROUP_W // GLA_HEADS
GLA_RANK = 16
GLA_NORMALIZER = 16.0
D_FF = 11008
CONV_W = 3
EPS = 1e-6

F32 = jnp.float32
BF16 = jnp.bfloat16
LOG2E = 1.4426950408889634

IN_SIZES = (2 * SGU_DIM,
            HG_K, HG_K, HG_K, HG_HEADS * HG_DV, HG_HEADS * HG_DV,
            N_HEADS * HEAD_DIM, N_KV_HEADS * HEAD_DIM, N_KV_HEADS * HEAD_DIM,
            GLA_HEADS * GLA_DK, GLA_HEADS * GLA_DK, GLA_HEADS * GLA_DV, GLA_HEADS * GLA_DV,
            GLA_RANK, GLA_RANK)
D_IN = sum(IN_SIZES)
_OFF = [0] + [int(c) for c in np.cumsum(IN_SIZES)]
(OFF_SGU, OFF_HQ, OFF_HFF, OFF_HFB, OFF_HI, OFF_HG, OFF_AQ, OFF_AK, OFF_AV,
 OFF_GQ, OFF_GK, OFF_GV, OFF_GG, OFF_GLF, OFF_GLB) = _OFF[:-1]

LANE = 128
SUBLANE = 8
VMEM_LIMIT = 56 * 1024 * 1024

Z_TILE_N = 512
Z_COLS = -(-D_IN // Z_TILE_N) * Z_TILE_N
SCAN_C = 128
MOD_ROWS = 16

TP = BATCH * SEQ
TL = DEC_BATCH * DEC_SEQ
T = TP + TL
PIECES = ((0, TP), (TP, TL))


def _cparams(sem):
    return pltpu.CompilerParams(dimension_semantics=sem, vmem_limit_bytes=VMEM_LIMIT)


def _row_group(row0):
    return jnp.where(row0 < TP, 0, 1 + (row0 - TP) // DEC_SEQ)


def _dot(a, b):
    return jnp.dot(a, b, preferred_element_type=F32)


def _dot_nt(a, b):
    return lax.dot_general(a, b, (((1,), (1,)), ((), ())), preferred_element_type=F32)


def _dot_tn(a, b):
    return lax.dot_general(a, b, (((0,), (0,)), ((), ())), preferred_element_type=F32)


def _sigmoid(x):
    return 1.0 / (1.0 + jnp.exp(-x))


def _silu(x):
    return x * _sigmoid(x)


def _gelu(x):
    return 0.5 * x * (1.0 + lax.erf(x * (2.0 ** -0.5)))


def _rms(x, g):
    return x * lax.rsqrt(jnp.mean(x * x, axis=-1, keepdims=True) + EPS) * g


def _ada_kernel(c_ref, w_ref, b_ref, o_ref):
    c = _silu(c_ref[...]).astype(BF16)
    o_ref[...] = _dot(c, w_ref[...].astype(BF16)) + b_ref[...]


def _ada_mod(cvec, w_ada, b_ada):
    tn = 512
    n = w_ada.shape[-1]
    return pl.pallas_call(
        _ada_kernel,
        grid=(DEPTH, n // tn),
        in_specs=[pl.BlockSpec((MOD_ROWS, D_MODEL), lambda l, j: (0, 0)),
                  pl.BlockSpec((None, D_MODEL, tn), lambda l, j: (l, 0, j)),
                  pl.BlockSpec((None, 1, tn), lambda l, j: (l, 0, j))],
        out_specs=pl.BlockSpec((None, MOD_ROWS, tn), lambda l, j: (l, 0, j)),
        out_shape=jax.ShapeDtypeStruct((DEPTH, MOD_ROWS, n), F32),
        compiler_params=_cparams(("parallel", "parallel")),
        name="ada_mod",
    )(cvec, w_ada, b_ada.reshape(DEPTH, 1, n))


def _mod_spec(layer, chunk, tr, blk0=0):
    return pl.BlockSpec((None, None, 1, D_MODEL),
                        lambda i, *_: (layer, _row_group((blk0 + i) * tr), 0, chunk))


NORM_ROWS = 256
_VEC_SPEC = pl.BlockSpec((1, D_MODEL), lambda i: (0, 0))


def _row_spec(blk0=0):
    return pl.BlockSpec((NORM_ROWS, D_MODEL), lambda i: (blk0 + i, 0))


def _norm_mod_kernel(x_ref, g_ref, sc_ref, sh_ref, h_ref):
    y = _rms(x_ref[...], g_ref[...])
    h_ref[...] = (y * (1.0 + sc_ref[...]) + sh_ref[...]).astype(h_ref.dtype)


def _norm_mod(x_parts, g, mod, layer, sc_chunk, sh_chunk):
    tr = NORM_ROWS
    h = None
    for x, (row0, rows) in zip(x_parts, PIECES):
        blk0 = row0 // tr
        in_specs = [_row_spec(), _VEC_SPEC, _mod_spec(layer, sc_chunk, tr, blk0), _mod_spec(layer, sh_chunk, tr, blk0)]
        args = [x, g.reshape(1, D_MODEL), mod, mod]
        aliases = {}
        if h is not None:
            in_specs.append(pl.BlockSpec(memory_space=pl.ANY))
            aliases[len(args)] = 0
            args.append(h)
        h = pl.pallas_call(
            lambda x_ref, g_ref, sc_ref, sh_ref, *rest: _norm_mod_kernel(x_ref, g_ref, sc_ref, sh_ref, rest[-1]),
            grid=(rows // tr,),
            in_specs=in_specs,
            out_specs=_row_spec(blk0),
            out_shape=jax.ShapeDtypeStruct((T, D_MODEL), BF16),
            input_output_aliases=aliases,
            compiler_params=_cparams(("parallel",)),
            name="norm_mod",
        )(*args)
    return h


def _resid_kernel(x_ref, m_ref, ga_ref, gate_ref, *rest, emit_h):
    x = x_ref[...] + gate_ref[...] * _rms(m_ref[...].astype(F32), ga_ref[...])
    if emit_h:
        gb_ref, sc_ref, sh_ref, xo_ref, h_ref = rest
        xo_ref[...] = x
        y = _rms(x, gb_ref[...])
        h_ref[...] = (y * (1.0 + sc_ref[...]) + sh_ref[...]).astype(h_ref.dtype)
    else:
        (xo_ref,) = rest
        xo_ref[...] = x


def _resid(x, m, ga, mod, layer, gate_chunk, nxt=None, split_out=False):
    tr = NORM_ROWS
    x_parts = x if isinstance(x, (list, tuple)) else None
    pieces = PIECES if (x_parts is not None or split_out) else ((0, T),)
    emit_h = nxt is not None
    assert not (emit_h and split_out)
    x_new, h, outs = None, None, []
    for p, (row0, rows) in enumerate(pieces):
        blk0 = row0 // tr
        in_specs = [_row_spec(0 if x_parts is not None else blk0), _row_spec(blk0), _VEC_SPEC,
                    _mod_spec(layer, gate_chunk, tr, blk0)]
        args = [x_parts[p] if x_parts is not None else x, m, ga.reshape(1, D_MODEL), mod]
        if emit_h:
            gb, layer_b, sc_chunk, sh_chunk = nxt
            in_specs += [_VEC_SPEC, _mod_spec(layer_b, sc_chunk, tr, blk0), _mod_spec(layer_b, sh_chunk, tr, blk0)]
            args += [gb.reshape(1, D_MODEL), mod, mod]
        n_in = len(args)
        if split_out:
            out_specs = [_row_spec()]
            out_shape = [jax.ShapeDtypeStruct((rows, D_MODEL), F32)]
        else:
            out_specs = [_row_spec(blk0)]
            out_shape = [jax.ShapeDtypeStruct((T, D_MODEL), F32)]
        if emit_h:
            out_specs.append(_row_spec(blk0))
            out_shape.append(jax.ShapeDtypeStruct((T, D_MODEL), BF16))
        aliases = {}
        if x_new is not None and not split_out:
            for k, prev in enumerate((x_new, h) if emit_h else (x_new,)):
                in_specs.append(pl.BlockSpec(memory_space=pl.ANY))
                aliases[len(args)] = k
                args.append(prev)
        n_alias = len(aliases)

        def kern(*refs, n_in=n_in, n_alias=n_alias):
            _resid_kernel(*refs[:n_in], *refs[n_in + n_alias:], emit_h=emit_h)

        res = pl.pallas_call(
            kern,
            grid=(rows // tr,),
            in_specs=in_specs,
            out_specs=out_specs,
            out_shape=out_shape,
            input_output_aliases=aliases,
            compiler_params=_cparams(("parallel",)),
            name="resid_norm",
        )(*args)
        x_new = res[0]
        h = res[1] if emit_h else None
        outs.append(res[0])
    return (outs if split_out else x_new), h


MM_TILE_M = 1024


def _mm_kernel(a_ref, b_ref, o_ref, *, n_valid):
    acc = _dot(a_ref[...], b_ref[...].astype(BF16))
    tn = o_ref.shape[1]
    if n_valid % tn:
        col = lax.broadcasted_iota(jnp.int32, acc.shape, 1)
        acc = jnp.where(col < n_valid - pl.program_id(1) * tn, acc, 0.0)
    o_ref[...] = acc.astype(o_ref.dtype)


def _matmul(a, w, layer, n_out, out_dtype, tn, name):
    m, k = a.shape
    tm = 2 * MM_TILE_M
    return pl.pallas_call(
        functools.partial(_mm_kernel, n_valid=w.shape[2]),
        grid=(m // tm, n_out // tn),
        in_specs=[pl.BlockSpec((tm, k), lambda i, j: (i, 0), pipeline_mode=pl.Buffered(1)),
                  pl.BlockSpec((None, k, tn), lambda i, j: (layer, 0, j))],
        out_specs=pl.BlockSpec((tm, tn), lambda i, j: (i, j)),
        out_shape=jax.ShapeDtypeStruct((m, n_out), out_dtype),
        compiler_params=_cparams(("parallel", "parallel")),
        name=name,
    )(a, w)


def _mm4_kernel(a0_ref, a1_ref, a2_ref, a3_ref, b_ref, o_ref):
    acc = _dot(a0_ref[...], b_ref[0 * GROUP_W:1 * GROUP_W, :])
    acc += _dot(a1_ref[...], b_ref[1 * GROUP_W:2 * GROUP_W, :])
    acc += _dot(a2_ref[...], b_ref[2 * GROUP_W:3 * GROUP_W, :])
    acc += _dot(a3_ref[...], b_ref[3 * GROUP_W:4 * GROUP_W, :])
    o_ref[...] = acc.astype(o_ref.dtype)


def _out_proj(parts, w, layer):
    tm, tn = MM_TILE_M, 1024
    a_spec = pl.BlockSpec((tm, GROUP_W), lambda i, j: (i, 0))
    return pl.pallas_call(
        _mm4_kernel,
        grid=(T // tm, D_MODEL // tn),
        in_specs=[a_spec, a_spec, a_spec, a_spec,
                  pl.BlockSpec((None, 4 * GROUP_W, tn), lambda i, j: (layer, 0, j))],
        out_specs=pl.BlockSpec((tm, tn), lambda i, j: (i, j)),
        out_shape=jax.ShapeDtypeStruct((T, D_MODEL), BF16),
        compiler_params=_cparams(("parallel", "parallel")),
        name="out_proj",
    )(*parts, w)


def _mm_ktiled_kernel(a_ref, b_ref, o_ref, acc_ref):
    k = pl.program_id(2)
    last = pl.num_programs(2) - 1
    d = _dot(a_ref[...], b_ref[...])

    @pl.when(k == 0)
    def _():
        acc_ref[...] = d

    @pl.when(jnp.logical_and(k != 0, k != last))
    def _():
        acc_ref[...] += d

    @pl.when(k == last)
    def _():
        o_ref[...] = (acc_ref[...] + d).astype(o_ref.dtype)


def _ffn_down(a, w, layer):
    m, kk = a.shape
    n = w.shape[2]
    tm, tn, tk = MM_TILE_M, 512, kk // 2
    assert tk % LANE == 0
    return pl.pallas_call(
        _mm_ktiled_kernel,
        grid=(m // tm, n // tn, kk // tk),
        in_specs=[pl.BlockSpec((tm, tk), lambda i, j, k: (i, k)),
                  pl.BlockSpec((None, tk, tn), lambda i, j, k: (layer, k, j))],
        out_specs=pl.BlockSpec((tm, tn), lambda i, j, k: (i, j)),
        out_shape=jax.ShapeDtypeStruct((m, n), BF16),
        scratch_shapes=[pltpu.VMEM((tm, tn), F32)],
        compiler_params=_cparams(("parallel", "parallel", "arbitrary")),
        name="ffn_down",
    )(a, w)


FFN_TILE_M = 2048
FFN_TILE_N = 256
FFN_SUB_M = 256
HALO_ROWS = 16
assert D_FF % FFN_TILE_N == 0


def _ffn_up_kernel(h_ref, halo_ref, wg_ref, wv_ref, cw_ref, cb_ref, o_ref):
    tm, tn = o_ref.shape
    sub = FFN_SUB_M
    ns = tm // sub
    wg, wv = wg_ref[...].astype(BF16), wv_ref[...].astype(BF16)
    row0 = pl.program_id(0) * tm
    seq_len = jnp.where(row0 < TP, SEQ, DEC_SEQ)
    r = lax.broadcasted_iota(jnp.int32, (sub, tn), 0)
    g0 = _dot(jnp.concatenate([halo_ref[...], h_ref[0:sub, :]], axis=0), wg)
    gh = g0[0:HALO_ROWS]
    gs = [g0[HALO_ROWS:]] + [_dot(h_ref[s * sub:(s + 1) * sub, :], wg) for s in range(1, ns)]
    for s in range(ns):
        val = _dot(h_ref[s * sub:(s + 1) * sub, :], wv)
        g = gs[s]
        before = gh[0:1, :] if s == 0 else gs[s - 1][sub - 1:sub, :]
        after = gh[1:2, :] if s == ns - 1 else gs[s + 1][0:1, :]
        pos = (row0 + s * sub + r) & (seq_len - 1)
        prev = jnp.where(r == 0, before, pltpu.roll(g, 1, 0))
        prev = jnp.where(pos == 0, 0.0, prev)
        nxt = jnp.where(r == sub - 1, after, pltpu.roll(g, sub - 1, 0))
        nxt = jnp.where(pos == seq_len - 1, 0.0, nxt)
        conv = prev * cw_ref[0:1, :] + g * cw_ref[1:2, :] + nxt * cw_ref[2:3, :] + cb_ref[...]
        o_ref[s * sub:(s + 1) * sub, :] = (_gelu(conv) * val).astype(o_ref.dtype)


def _ffn_up(h, halo, w_up, conv_w, conv_b, layer):
    tm, tn = FFN_TILE_M, FFN_TILE_N
    nj = D_FF // tn
    return pl.pallas_call(
        _ffn_up_kernel,
        grid=(T // tm, nj),
        in_specs=[pl.BlockSpec((tm, D_MODEL), lambda i, j: (i, 0), pipeline_mode=pl.Buffered(1)),
                  pl.BlockSpec((None, HALO_ROWS, D_MODEL), lambda i, j: (i, 0, 0)),
                  pl.BlockSpec((None, D_MODEL, tn), lambda i, j: (layer, 0, j)),
                  pl.BlockSpec((None, D_MODEL, tn), lambda i, j: (layer, 0, nj + j)),
                  pl.BlockSpec((None, CONV_W, tn), lambda i, j: (layer, 0, j)),
                  pl.BlockSpec((None, 1, tn), lambda i, j: (layer, 0, j))],
        out_specs=pl.BlockSpec((tm, tn), lambda i, j: (i, j)),
        out_shape=jax.ShapeDtypeStruct((T, D_FF), BF16),
        compiler_params=_cparams(("parallel", "parallel")),
        name="ffn_up",
    )(h, halo, w_up, w_up, conv_w, conv_b.reshape(DEPTH, 1, D_FF))


def _conv_halo(h):
    tm = FFN_TILE_M
    nt = T // tm
    ht = h.reshape(nt, tm, D_MODEL)
    zero = jnp.zeros((1, D_MODEL), h.dtype)
    before = jnp.concatenate([zero, ht[:-1, tm - 1]], axis=0)
    after = jnp.concatenate([ht[1:, 0], zero], axis=0)
    pad = jnp.zeros((nt, HALO_ROWS - 2, D_MODEL), h.dtype)
    return jnp.concatenate([before[:, None], after[:, None], pad], axis=1)


SGU_CHUNKS_PER_STEP = 4


def _sgu_kernel(u_ref, v_ref, g_ref, b_ref, w_ref, bias_ref, o_ref):
    c = SGU_CHUNK
    for ci in range(SGU_CHUNKS_PER_STEP):
        rows = slice(ci * c, (ci + 1) * c)
        u = _gelu(u_ref[rows, :])
        v = _gelu(v_ref[rows, :])
        vc = v - jnp.mean(v, axis=-1, keepdims=True)
        vn = vc * lax.rsqrt(jnp.mean(vc * vc, axis=-1, keepdims=True) + EPS) * g_ref[...] + b_ref[...]
        vb = vn.astype(BF16)
        for g in range(SGU_GROUPS):
            sl = slice(g * SGU_GCH, (g + 1) * SGU_GCH)
            mixed = _dot(w_ref[g], vb[:, sl]) + bias_ref[:, sl]
            o_ref[rows, sl] = (u[:, sl] * mixed).astype(o_ref.dtype)


def _sgu(z, ln_g, ln_b, w_s, b_s):
    c = SGU_CHUNK
    rows = SGU_CHUNKS_PER_STEP * c
    bias = jnp.repeat(b_s.T, SGU_GCH, axis=1)
    vec = pl.BlockSpec((1, SGU_DIM), lambda i: (0, 0))
    return pl.pallas_call(
        _sgu_kernel,
        grid=(T // rows,),
        in_specs=[pl.BlockSpec((rows, SGU_DIM), lambda i: (i, OFF_SGU // SGU_DIM)),
                  pl.BlockSpec((rows, SGU_DIM), lambda i: (i, OFF_SGU // SGU_DIM + 1)),
                  vec, vec,
                  pl.BlockSpec((SGU_GROUPS, c, c), lambda i: (0, 0, 0)),
                  pl.BlockSpec((c, SGU_DIM), lambda i: (0, 0))],
        out_specs=pl.BlockSpec((rows, SGU_DIM), lambda i: (i, 0)),
        out_shape=jax.ShapeDtypeStruct((T, SGU_DIM), BF16),
        compiler_params=_cparams(("parallel",)),
        name="sgu",
    )(z, z, ln_g.reshape(1, SGU_DIM), ln_b.reshape(1, SGU_DIM), w_s.astype(BF16), bias)


SMALL_LEVELS = tuple(m for m in (1, 2, 4) if m < SUBLANE)
BIG_LEVELS = tuple(m for m in (8, 16, 32, 64) if m < SCAN_C)


def _scan_consts():
    c = SCAN_C
    t = np.arange(c)[:, None]
    s = np.arange(c)[None, :]
    x = t ^ s
    tri = np.stack([s <= t, s >= t]).astype(np.float32)
    pair = [np.stack([x == 0, x == 0])]
    for m in SMALL_LEVELS:
        lvl = (x >= m) & (x < 2 * m)
        pair.append(np.stack([lvl & (t > s), lvl & (t < s)]))
    pair = np.stack(pair).astype(np.float32)
    rowq = np.stack([np.broadcast_to(((np.arange(c) & m) != 0)[:, None], (c, LANE)) for m in SMALL_LEVELS])
    hb = np.arange(c // 2)
    blk = np.stack([(hb[:, None] // m) == (hb[None, :] // m) for m in BIG_LEVELS])
    return (jnp.asarray(tri, BF16), jnp.asarray(pair, F32), jnp.asarray(rowq.astype(np.float32), F32),
            jnp.asarray(blk.astype(np.float32), F32))


def _split2(x):
    hi = x.astype(BF16)
    lo = (x - hi.astype(F32)).astype(BF16)
    return hi, lo


def _halves(a, m, second):
    off = m if second else 0
    return jnp.concatenate([a[j + off:j + off + m] for j in range(0, a.shape[0], 2 * m)], axis=0)


def _chunk_scans(probs, consts):
    tri_ref, pair_ref, rowq_ref, blk_ref = consts
    c, dk = probs[0][0].shape
    dirs = [1 if p[5] else 0 for p in probs]
    vbs = [p[2].astype(BF16) for p in probs]
    diag = [pair_ref[0, d] * _dot_nt(p[0].astype(BF16), p[1].astype(BF16)) for p, d in zip(probs, dirs)]
    bs = []
    for p, d in zip(probs, dirs):
        hi, lo = _split2(p[3])
        tri = tri_ref[d]
        bs.append(_dot(tri, hi) + _dot(tri, lo))
    inter = [_dot_nt((p[0] * jnp.exp2(b)).astype(BF16), p[4].astype(BF16)) for p, b in zip(probs, bs)]

    big = [[] for _ in probs]
    for li, m in enumerate(BIG_LEVELS):
        for pi, ((q, k, _, _, _, reverse), b) in enumerate(zip(probs, bs)):
            q_second = not reverse
            refs = [b[j + m:j + m + 1] if reverse else b[j + m - 1:j + m] for j in range(0, c, 2 * m)]
            ref = jnp.concatenate([jnp.broadcast_to(r, (m, dk)) for r in refs], axis=0)
            qf = _halves(q, m, q_second) * jnp.exp2(_halves(b, m, q_second) - ref)
            kf = _halves(k, m, not q_second) * jnp.exp2(ref - _halves(b, m, not q_second))
            s_l = _dot_nt(qf.astype(BF16), kf.astype(BF16))
            if 2 * m < c:
                s_l = s_l * blk_ref[li]
            big[pi].append(s_l.astype(BF16))

    scores = diag
    es = list(bs)
    for li, m in enumerate(SMALL_LEVELS):
        for pi, ((q, k, _, _, _, reverse), b) in enumerate(zip(probs, bs)):
            is_q = (rowq_ref[li] < 0.5) if reverse else (rowq_ref[li] > 0.5)
            sh_q, sh_e = (c - m, m) if reverse else (m, c - m)
            fac = jnp.exp2(jnp.where(is_q, b - pltpu.roll(es[pi], sh_q, 0), es[pi] - b))
            u = (jnp.where(is_q, q, k) * fac).astype(BF16)
            scores[pi] = scores[pi] + pair_ref[1 + li, dirs[pi]] * _dot_nt(u, u)
            if li + 1 < len(SMALL_LEVELS):
                es[pi] = jnp.where(is_q, es[pi], pltpu.roll(es[pi], sh_e, 0))

    outs = []
    for pi, ((q, k, _, _, st, reverse), b) in enumerate(zip(probs, bs)):
        o = inter[pi] + _dot(scores[pi].astype(BF16), vbs[pi])
        parts = [o[g:g + SUBLANE] for g in range(0, c, SUBLANE)]
        q_second = not reverse
        for li, m in enumerate(BIG_LEVELS):
            oc = _dot(big[pi][li], _halves(probs[pi][2], m, not q_second).astype(BF16))
            for jj, j in enumerate(range(0, c, 2 * m)):
                for g in range(0, m, SUBLANE):
                    dst = (j + (m if q_second else 0) + g) // SUBLANE
                    parts[dst] = parts[dst] + oc[jj * m + g:jj * m + g + SUBLANE]
        bl = b[0:1, :] if reverse else b[c - 1:c, :]
        kd = (k * jnp.exp2(bl - b)).astype(BF16)
        st_new = jnp.exp2(bl) * st + _dot_tn(vbs[pi], kd)
        outs.append((jnp.concatenate(parts, axis=0), st_new))
    return outs


def _scan_kernel(*refs, features, n_in, seq, nseg, hpb, dv, with_s0, emit_state):
    consts = refs[:4]
    pos = 4
    dir_refs = []
    for _ in range(1 if nseg == 1 else 2):
        dir_refs.append((refs[pos:pos + n_in], refs[pos + n_in]))
        pos += n_in + 1
    dir_refs = dir_refs * 2 if nseg == 1 else dir_refs
    gain_ref = refs[pos]
    pos += 1
    s0_ref = None
    if with_s0:
        s0_ref = refs[pos]
        pos += 1
    o_ref = refs[pos]
    pos += 1
    sf_ref = None
    if emit_state:
        sf_ref = refs[pos]
        pos += 1
    o_scr, st_scr = refs[pos:pos + 2]
    c = SCAN_C
    n = seq // c
    cps = n // nseg
    g = pl.program_id(2) if nseg > 1 else 0
    work = [(hh, reverse) for hh in range(hpb) for reverse in (False, True)]

    def init():
        st_scr[...] = s0_ref[...] if with_s0 else jnp.zeros(st_scr.shape, F32)

    def step(i, finalize):
        loc = {False: pl.multiple_of(i * c, c), True: pl.multiple_of((cps - 1 - i) * c, c)}
        pair = g * cps + i
        glob = {False: pl.multiple_of(pair * c, c), True: pl.multiple_of((n - 1 - pair) * c, c)}
        probs = [features(dir_refs[rev][0], loc[rev], hh, rev) + (st_scr[1 if rev else 0, hh], rev)
                 for hh, rev in work]
        for (hh, rev), (o, st) in zip(work, _chunk_scans(probs, consts)):
            st_scr[1 if rev else 0, hh] = st
            rows, cols = pl.ds(glob[rev], c), slice(hh * dv, (hh + 1) * dv)
            if finalize:
                gate = dir_refs[rev][1][pl.ds(loc[rev], c), cols]
                y = _rms(o + o_scr[rows, cols], gain_ref[...]) * _silu(gate)
                o_ref[rows, cols] = y.astype(o_ref.dtype)
            else:
                o_scr[rows, cols] = o

    def loop(lo, hi, finalize):
        lax.fori_loop(lo, hi, lambda i, carry: (step(i, finalize), carry)[1], 0)

    if nseg == 1:
        init()
        loop(0, n // 2, False)
        loop(n // 2, n, True)
    else:
        pl.when(g == 0)(init)
        pl.when(g < nseg // 2)(lambda: loop(0, cps, False))
        pl.when(g >= nseg // 2)(lambda: loop(0, cps, True))
    if emit_state:
        assert nseg == 1
        for d in range(2):
            for hh in range(hpb):
                sf_ref[d, hh] = st_scr[d, hh].T


def _hgrn_features(in_refs, r0, head, reverse):
    hq_ref, hf_ref, hb_ref, hi_ref, lb_ref = in_refs
    rows, cols = pl.ds(r0, SCAN_C), slice(head * HG_DK, (head + 1) * HG_DK)
    q = _silu(hq_ref[rows, cols]) * (HG_DK ** -0.5)
    lb = lb_ref[1 if reverse else 0][:, cols]
    f = lb + (1.0 - lb) * _sigmoid((hb_ref if reverse else hf_ref)[rows, cols])
    return q, 1.0 - f, hi_ref[rows, head * HG_DV:(head + 1) * HG_DV], jnp.log(f) * LOG2E


def _gla_features(in_refs, r0, head, reverse):
    gq_ref, gk_ref, gv_ref, low_ref, wgk_ref, bgk_ref = in_refs
    rows, cols = pl.ds(r0, SCAN_C), slice(head * GLA_DK, (head + 1) * GLA_DK)
    d = 1 if reverse else 0
    g = _dot(low_ref[rows, :].astype(BF16), wgk_ref[d][:, cols]) + bgk_ref[d][:, cols]
    la = (jnp.minimum(g, 0.0) - jnp.log(1.0 + jnp.exp(-jnp.abs(g)))) * (LOG2E / GLA_NORMALIZER)
    return (gq_ref[rows, cols] * (GLA_DK ** -0.5), gk_ref[rows, cols],
            gv_ref[rows, head * GLA_DV:(head + 1) * GLA_DV], la)


def _seq_geom(latent):
    seq = DEC_SEQ if latent else SEQ
    return seq, (DEC_BATCH if latent else BATCH), (TP // seq if latent else 0)


def _zspec(latent, off, width, per_head=True):
    seq, _, blk0 = _seq_geom(latent)
    assert off % width == 0
    if per_head:
        return pl.BlockSpec((seq, width), lambda b, h, *_: (blk0 + b, off // width + h))
    return pl.BlockSpec((seq, width), lambda b, h, *_: (blk0 + b, off // width))


def _const_spec(a):
    return pl.BlockSpec(a.shape, lambda *_, nd=a.ndim: (0,) * nd)


def _scan_geom(latent):
    return (2, 8) if latent else (2, 1)


def _zseg_spec(latent, off, width, reverse, per_head=True):
    seq, _, blk0 = _seq_geom(latent)
    nseg = _scan_geom(latent)[1]
    assert off % width == 0
    col0 = off // width

    def index(b, h, g=0):
        seg = (nseg - 1 - g) if reverse else g
        return ((blk0 + b) * nseg + seg, col0 + (h if per_head else 0))

    return pl.BlockSpec((seq // nseg, width), index)


def _scan_call(name, features, feat_fn, z, gate_off, gain, heads, dk, dv,
               latent, s0t, prev_out, prev_state, layer):
    seq, nb, row_blk0 = _seq_geom(latent)
    hpb, nseg = _scan_geom(latent)
    consts = _scan_consts()
    in_specs = [_const_spec(a) for a in consts]
    args = list(consts)
    for reverse in ((False,) if nseg == 1 else (False, True)):
        specs, fargs = feat_fn(reverse)
        n_in = len(specs)
        in_specs += list(specs) + [_zseg_spec(latent, gate_off, hpb * dv, reverse)]
        args += list(fargs) + [z]
    in_specs.append(pl.BlockSpec((1, dv), lambda *_: (0, 0)))
    args.append(gain.reshape(1, dv))
    if latent:
        in_specs.append(pl.BlockSpec((None, None, 2, hpb, dv, dk), lambda b, h, *_: (b, layer, 0, h, 0, 0)))
        args.append(s0t)
    out_specs = [pl.BlockSpec((seq, hpb * dv), lambda b, h, *_: (row_blk0 + b, h))]
    out_shape = [jax.ShapeDtypeStruct((T, heads * dv), BF16)]
    aliases = {}
    if prev_out is not None:
        in_specs.append(pl.BlockSpec(memory_space=pl.ANY))
        aliases[len(args)] = 0
        args.append(prev_out)
    emit_state = not latent
    if emit_state:
        out_specs.append(pl.BlockSpec((None, None, 2, hpb, dk, dv), lambda b, h, *_: (b, layer, 0, h, 0, 0)))
        out_shape.append(jax.ShapeDtypeStruct((BATCH, DEPTH, 2, heads, dk, dv), F32))
        if prev_state is not None:
            in_specs.append(pl.BlockSpec(memory_space=pl.ANY))
            aliases[len(args)] = 1
            args.append(prev_state)
    n_alias = len(aliases)

    def kern(*refs):
        n_inputs = len(args) - n_alias
        keep = refs[:n_inputs] + refs[n_inputs + n_alias:]
        _scan_kernel(*keep, features=features, n_in=n_in, seq=seq, nseg=nseg, hpb=hpb, dv=dv, with_s0=latent,
                     emit_state=emit_state)

    grid = (nb, heads // hpb) + ((nseg,) if nseg > 1 else ())
    sem = ("parallel", "parallel") + (("arbitrary",) if nseg > 1 else ())
    res = pl.pallas_call(
        kern,
        grid=grid,
        in_specs=in_specs,
        out_specs=out_specs,
        out_shape=out_shape,
        input_output_aliases=aliases,
        scratch_shapes=[pltpu.VMEM((seq, hpb * dv), F32), pltpu.VMEM((2, hpb, dv, dk), F32)],
        compiler_params=_cparams(sem),
        name=name,
    )(*args)
    return (res[0], res[1]) if emit_state else (res[0], None)


def _hgrn(z, lb, gain, s0t, prev_state, layer):
    lb3 = lb.reshape(2, 1, HG_K)
    o = state = None
    for latent in (False, True):
        hpb = _scan_geom(latent)[0]

        def feat_fn(reverse, latent=latent, hpb=hpb):
            zs = functools.partial(_zseg_spec, latent, reverse=reverse)
            return ([zs(OFF_HQ, hpb * HG_DK), zs(OFF_HFF, hpb * HG_DK), zs(OFF_HFB, hpb * HG_DK),
                     zs(OFF_HI, hpb * HG_DV), pl.BlockSpec((2, 1, hpb * HG_DK), lambda b, h, *_: (0, 0, h))],
                    [z, z, z, z, lb3])

        o, st = _scan_call("hgrn_latent" if latent else "hgrn_prompt", _hgrn_features, feat_fn,
                           z, OFF_HG, gain, HG_HEADS, HG_DK, HG_DV, latent, s0t, o, prev_state, layer)
        state = st if st is not None else state
    return o, state


def _gla(z, w_gk, b_gk, gain, s0t, prev_state, layer):
    wpad = jnp.zeros((2, LANE, GLA_HEADS * GLA_DK), F32)
    wpad = wpad.at[0, 0:GLA_RANK].set(w_gk[0]).at[1, GLA_RANK:2 * GLA_RANK].set(w_gk[1]).astype(BF16)
    b3 = b_gk.reshape(2, 1, GLA_HEADS * GLA_DK)
    o = state = None
    for latent in (False, True):
        hpb = _scan_geom(latent)[0]

        def feat_fn(reverse, latent=latent, hpb=hpb):
            zs = functools.partial(_zseg_spec, latent, reverse=reverse)
            return ([zs(OFF_GQ, hpb * GLA_DK), zs(OFF_GK, hpb * GLA_DK), zs(OFF_GV, hpb * GLA_DV),
                     zs(OFF_GLF, LANE, per_head=False),
                     pl.BlockSpec((2, LANE, hpb * GLA_DK), lambda b, h, *_: (0, 0, h)),
                     pl.BlockSpec((2, 1, hpb * GLA_DK), lambda b, h, *_: (0, 0, h))],
                    [z, z, z, z, wpad, b3])

        o, st = _scan_call("gla_latent" if latent else "gla_prompt", _gla_features, feat_fn,
                           z, OFF_GG, gain, GLA_HEADS, GLA_DK, GLA_DV, latent, s0t, o, prev_state, layer)
        state = st if st is not None else state
    return o, state


GQA = N_HEADS // N_KV_HEADS


def _ctx_attn_kernel(q_ref, k_ref, v_ref, sink_ref, o_ref):
    hd = HEAD_DIM
    k = k_ref[...].astype(BF16)
    v = v_ref[...].astype(BF16)
    heads = range(GQA)
    ss = [_dot_nt((q_ref[:, g * hd:(g + 1) * hd] * (hd ** -0.5)).astype(BF16), k) for g in heads]
    ps, dens = [], []
    for g in heads:
        sink = sink_ref[g][:, 0:1]
        m = jnp.maximum(jnp.max(ss[g], axis=-1, keepdims=True), sink)
        p = jnp.exp(ss[g] - m)
        dens.append(jnp.sum(p, axis=-1, keepdims=True) + jnp.exp(sink - m))
        ps.append(p.astype(BF16))
    for g in heads:
        o_ref[:, g * hd:(g + 1) * hd] = (_dot(ps[g], v) / dens[g]).astype(o_ref.dtype)


def _ctx_attention(z, sink3):
    hd = HEAD_DIM
    return pl.pallas_call(
        _ctx_attn_kernel,
        grid=(BATCH, N_KV_HEADS),
        in_specs=[_zspec(False, OFF_AQ, GQA * hd),
                  pl.BlockSpec((SEQ, hd), lambda b, kh: (b, OFF_AK // hd + kh)),
                  pl.BlockSpec((SEQ, hd), lambda b, kh: (b, OFF_AV // hd + kh)),
                  pl.BlockSpec((GQA, 1, LANE), lambda b, kh: (kh, 0, 0))],
        out_specs=pl.BlockSpec((SEQ, GQA * hd), lambda b, kh: (b, kh)),
        out_shape=jax.ShapeDtypeStruct((T, N_HEADS * hd), BF16),
        compiler_params=_cparams(("parallel", "parallel")),
        name="ctx_attention",
    )(z, z, z, sink3)


def _rope(x, cos_f, sin_f):
    return x * cos_f + pltpu.roll(x, HEAD_DIM // 2, 1) * sin_f


def _lat_attn_kernel(q_ref, k_ref, v_ref, ck_ref, cv_ref, cos_ref, sin_ref, sink_ref, o_ref, kr_scr, vr_scr):
    blk, hd, seq = ATT_BLOCK, HEAD_DIM, DEC_SEQ
    nb = seq // blk

    zeros = jnp.zeros((blk, hd), BF16)
    for scr in (kr_scr, vr_scr):
        scr[0:blk, :] = zeros
        scr[blk + seq:2 * blk + seq, :] = zeros

    def fill(n, carry):
        r0 = pl.multiple_of(n * blk, blk)
        rows = pl.ds(r0, blk)
        kr_scr[pl.ds(r0 + blk, blk), :] = _rope(k_ref[rows, :], cos_ref[rows, :], sin_ref[rows, :]).astype(BF16)
        vr_scr[pl.ds(r0 + blk, blk), :] = v_ref[rows, :].astype(BF16)
        return carry

    lax.fori_loop(0, nb, fill, 0)

    ck = ck_ref[...].astype(BF16)
    cv = cv_ref[...].astype(BF16)
    sink = jnp.concatenate([jnp.broadcast_to(sink_ref[g][:, 0:1], (blk, 1)) for g in range(GQA)], axis=0)
    qi = lax.broadcasted_iota(jnp.int32, (GQA * blk, 3 * blk), 0) & (blk - 1)
    kj = lax.broadcasted_iota(jnp.int32, (GQA * blk, 3 * blk), 1)
    window_bias = jnp.where(jnp.abs(kj - qi - blk) <= WINDOW, 0.0, -jnp.inf)

    def qblock(n, carry):
        r0 = pl.multiple_of(n * blk, blk)
        rows = pl.ds(r0, blk)
        cos_f, sin_f = cos_ref[rows, :], sin_ref[rows, :]
        q = jnp.concatenate([(_rope(q_ref[rows, g * hd:(g + 1) * hd], cos_f, sin_f) * (hd ** -0.5)).astype(BF16)
                             for g in range(GQA)], axis=0)
        band = pl.ds(r0, 3 * blk)
        key_pos = kj + (n - 1) * blk
        s_loc = _dot_nt(q, kr_scr[band, :]) + window_bias
        s_loc = jnp.where(key_pos >= 0, s_loc, -jnp.inf)
        s_loc = jnp.where(key_pos < seq, s_loc, -jnp.inf)
        s_ctx = _dot_nt(q, ck)
        m = jnp.maximum(jnp.maximum(jnp.max(s_loc, axis=-1, keepdims=True),
                                    jnp.max(s_ctx, axis=-1, keepdims=True)), sink)
        p_loc = jnp.exp(s_loc - m)
        p_ctx = jnp.exp(s_ctx - m)
        den = (jnp.sum(p_loc, axis=-1, keepdims=True) + jnp.sum(p_ctx, axis=-1, keepdims=True)
               + jnp.exp(sink - m))
        o = (_dot(p_ctx.astype(BF16), cv) + _dot(p_loc.astype(BF16), vr_scr[band, :])) / den
        for g in range(GQA):
            o_ref[rows, g * hd:(g + 1) * hd] = o[g * blk:(g + 1) * blk].astype(o_ref.dtype)
        return carry

    lax.fori_loop(0, nb, qblock, 0)


def _rope_tables():
    n = jnp.arange(DEC_SEQ)
    row = (n // GRID_W).astype(F32)
    col = (n % GRID_W).astype(F32)
    n_freq = HEAD_DIM // 4
    inv = ROPE_THETA ** (-jnp.arange(n_freq, dtype=F32) / n_freq)
    ang = jnp.concatenate([row[:, None] * inv, col[:, None] * inv], axis=-1)
    cos, sin = jnp.cos(ang), jnp.sin(ang)
    return jnp.concatenate([cos, cos], axis=-1), jnp.concatenate([-sin, sin], axis=-1)


def _lat_attention(z, ck, cv, sink3, prev_out):
    hd = HEAD_DIM
    seq, _, blk0 = _seq_geom(True)
    cos_f, sin_f = _rope_tables()
    tab = pl.BlockSpec((seq, hd), lambda b, kh: (0, 0))
    cache = pl.BlockSpec((None, PAST_LEN, hd), lambda b, kh: (b, 0, kh))
    return pl.pallas_call(
        lambda *refs: _lat_attn_kernel(*refs[:8], *refs[9:]),
        grid=(DEC_BATCH, N_KV_HEADS),
        in_specs=[_zspec(True, OFF_AQ, GQA * hd),
                  pl.BlockSpec((seq, hd), lambda b, kh: (blk0 + b, OFF_AK // hd + kh)),
                  pl.BlockSpec((seq, hd), lambda b, kh: (blk0 + b, OFF_AV // hd + kh)),
                  cache, cache, tab, tab,
                  pl.BlockSpec((GQA, 1, LANE), lambda b, kh: (kh, 0, 0)),
                  pl.BlockSpec(memory_space=pl.ANY)],
        out_specs=pl.BlockSpec((seq, GQA * hd), lambda b, kh: (blk0 + b, kh)),
        out_shape=jax.ShapeDtypeStruct((T, N_HEADS * hd), BF16),
        input_output_aliases={8: 0},
        scratch_shapes=[pltpu.VMEM((seq + 2 * ATT_BLOCK, hd), BF16), pltpu.VMEM((seq + 2 * ATT_BLOCK, hd), BF16)],
        compiler_params=_cparams(("parallel", "parallel")),
        name="lat_attention",
    )(z, z, z, ck, cv, cos_f, sin_f, sink3, prev_out)


def kernel(x_prompt, x_sample, cache_k, cache_v, state_hgrn, state_gla, c, c_ctx, w_ada, b_ada, norm_g, w_in,
           w_out, sgu_ln_g, sgu_ln_b, sgu_w, sgu_b, hgrn_lb_logits, hgrn_norm_g, attn_sink, gla_w_gk, gla_b_gk,
           gla_norm_g, ffn_w_up, ffn_conv_w, ffn_conv_b, ffn_w_down):
    assert TP % DEC_SEQ == 0 and DEC_BATCH + 1 <= MOD_ROWS
    lb_all = jnp.cumsum(jax.nn.softmax(hgrn_lb_logits.astype(F32), axis=0), axis=0)
    lb_all = lb_all - lb_all[0:1]

    cvec = jnp.zeros((MOD_ROWS, D_MODEL), F32).at[0].set(c_ctx).at[1:1 + DEC_BATCH].set(c)
    mod = _ada_mod(cvec, w_ada, b_ada).reshape(DEPTH, MOD_ROWS, 1, 6 * D_MODEL)
    SH1, SC1, G1, SH2, SC2, G2 = range(6)

    x = [x_prompt.reshape(TP, D_MODEL), x_sample.reshape(TL, D_MODEL)]
    hg_s0t = jnp.swapaxes(state_hgrn, -1, -2)
    gl_s0t = jnp.swapaxes(state_gla, -1, -2)
    ck_all = cache_k.reshape(DEC_BATCH, DEPTH, PAST_LEN, N_KV_HEADS * HEAD_DIM)
    cv_all = cache_v.reshape(DEC_BATCH, DEPTH, PAST_LEN, N_KV_HEADS * HEAD_DIM)

    h = _norm_mod(x, norm_g[0, 0], mod, 0, SC1, SH1)
    ks_new, vs_new = [], []
    hg_state = gl_state = None
    w_in_b, w_up_b = w_in, ffn_w_up
    w_out_b, w_down_b = w_out.astype(BF16), ffn_w_down.astype(BF16)
    for l in range(DEPTH):
        sink3 = jnp.broadcast_to(attn_sink[l].astype(F32)[:, None, None], (N_HEADS, 1, LANE))

        z = _matmul(h, w_in_b, l, Z_COLS, F32, Z_TILE_N, "in_proj")
        o_sgu = _sgu(z, sgu_ln_g[l], sgu_ln_b[l], sgu_w[l], sgu_b[l])
        o_hg, hg_state = _hgrn(z, lb_all[l], hgrn_norm_g[l], hg_s0t, hg_state, l)
        o_att = _ctx_attention(z, sink3)
        o_att = _lat_attention(z, ck_all[:, l], cv_all[:, l], sink3, o_att)
        o_gl, gl_state = _gla(z, gla_w_gk[l], gla_b_gk[l], gla_norm_g[l], gl_s0t, gl_state, l)
        m = _out_proj([o_sgu, o_hg, o_att, o_gl], w_out_b, l)
        x, h = _resid(x, m, norm_g[l, 1], mod, l, G1, nxt=(norm_g[l, 2], l, SC2, SH2))
        ks_new.append(z[:TP, OFF_AK:OFF_AK + N_KV_HEADS * HEAD_DIM].reshape(BATCH, SEQ, N_KV_HEADS, HEAD_DIM))
        vs_new.append(z[:TP, OFF_AV:OFF_AV + N_KV_HEADS * HEAD_DIM].reshape(BATCH, SEQ, N_KV_HEADS, HEAD_DIM))

        act = _ffn_up(h, _conv_halo(h), w_up_b, ffn_conv_w, ffn_conv_b, l)
        f = _ffn_down(act, w_down_b, l)
        if l + 1 < DEPTH:
            x, h = _resid(x, f, norm_g[l, 3], mod, l, G2, nxt=(norm_g[l + 1, 0], l + 1, SC1, SH1))
        else:
            x, h = _resid(x, f, norm_g[l, 3], mod, l, G2, split_out=True)

    y_prompt = x[0].reshape(BATCH, SEQ, D_MODEL)
    y_sample = x[1].reshape(DEC_BATCH, DEC_SEQ, D_MODEL)
    return (y_prompt, y_sample, jnp.stack(ks_new, axis=1), jnp.stack(vs_new, axis=1), hg_state, gl_state)
```

```python
import functools

import numpy as np
import jax
import jax.numpy as jnp
from jax import lax
from jax.experimental import pallas as pl
from jax.experimental.pallas import tpu as pltpu

D_MODEL = 4096
BATCH = 32
SEQ = 256
DEPTH = 2
DEC_BATCH = 2
DEC_SEQ = 4096
PAST_LEN = 512
GRID_W = 64
GROUP_W = D_MODEL // 4
SGU_CHUNK = 128
SGU_GROUPS = 4
SGU_DIM = GROUP_W
SGU_GCH = SGU_DIM // SGU_GROUPS
HG_HEADS = 8
HG_DK = 128
HG_DV = GROUP_W // HG_HEADS
HG_K = HG_HEADS * HG_DK
N_HEADS = 8
N_KV_HEADS = 2
HEAD_DIM = GROUP_W // N_HEADS
WINDOW = 128
ATT_BLOCK = 128
ROPE_THETA = 10000.0
GLA_HEADS = 4
GLA_DK = 128
GLA_DV = GROUP_W // GLA_HEADS
GLA_RANK = 16
GLA_NORMALIZER = 16.0
D_FF = 11008
CONV_W = 3
EPS = 1e-6

F32 = jnp.float32
BF16 = jnp.bfloat16
LOG2E = 1.4426950408889634

IN_SIZES = (2 * SGU_DIM,
            HG_K, HG_K, HG_K, HG_HEADS * HG_DV, HG_HEADS * HG_DV,
            N_HEADS * HEAD_DIM, N_KV_HEADS * HEAD_DIM, N_KV_HEADS * HEAD_DIM,
            GLA_HEADS * GLA_DK, GLA_HEADS * GLA_DK, GLA_HEADS * GLA_DV, GLA_HEADS * GLA_DV,
            GLA_RANK, GLA_RANK)
D_IN = sum(IN_SIZES)
_OFF = [0] + [int(c) for c in np.cumsum(IN_SIZES)]
(OFF_SGU, OFF_HQ, OFF_HFF, OFF_HFB, OFF_HI, OFF_HG, OFF_AQ, OFF_AK, OFF_AV,
 OFF_GQ, OFF_GK, OFF_GV, OFF_GG, OFF_GLF, OFF_GLB) = _OFF[:-1]

LANE = 128
SUBLANE = 8
VMEM_LIMIT = 56 * 1024 * 1024

Z_TILE_N = 1024
Z_COLS = -(-D_IN // Z_TILE_N) * Z_TILE_N
SCAN_C = 128
MOD_ROWS = 16

TP = BATCH * SEQ
TL = DEC_BATCH * DEC_SEQ
T = TP + TL
PIECES = ((0, TP), (TP, TL))


def _cparams(sem):
    return pltpu.CompilerParams(dimension_semantics=sem, vmem_limit_bytes=VMEM_LIMIT)


def _row_group(row0):
    return jnp.where(row0 < TP, 0, 1 + (row0 - TP) // DEC_SEQ)


def _dot(a, b):
    return jnp.dot(a, b, preferred_element_type=F32)


def _dot_nt(a, b):
    return lax.dot_general(a, b, (((1,), (1,)), ((), ())), preferred_element_type=F32)


def _dot_tn(a, b):
    return lax.dot_general(a, b, (((0,), (0,)), ((), ())), preferred_element_type=F32)


def _sigmoid(x):
    return 1.0 / (1.0 + jnp.exp(-x))


def _silu(x):
    return x * _sigmoid(x)


def _gelu(x):
    return 0.5 * x * (1.0 + lax.erf(x * (2.0 ** -0.5)))


def _rms(x, g):
    return x * lax.rsqrt(jnp.mean(x * x, axis=-1, keepdims=True) + EPS) * g


def _ada_kernel(c_ref, w_ref, b_ref, o_ref):
    c = _silu(c_ref[...]).astype(BF16)
    o_ref[...] = _dot(c, w_ref[...].astype(BF16)) + b_ref[...]


def _ada_mod(cvec, w_ada, b_ada):
    tn = 512
    n = w_ada.shape[-1]
    return pl.pallas_call(
        _ada_kernel,
        grid=(DEPTH, n // tn),
        in_specs=[pl.BlockSpec((MOD_ROWS, D_MODEL), lambda l, j: (0, 0)),
                  pl.BlockSpec((None, D_MODEL, tn), lambda l, j: (l, 0, j)),
                  pl.BlockSpec((None, 1, tn), lambda l, j: (l, 0, j))],
        out_specs=pl.BlockSpec((None, MOD_ROWS, tn), lambda l, j: (l, 0, j)),
        out_shape=jax.ShapeDtypeStruct((DEPTH, MOD_ROWS, n), F32),
        compiler_params=_cparams(("parallel", "parallel")),
        name="ada_mod",
    )(cvec, w_ada, b_ada.reshape(DEPTH, 1, n))


def _mod_spec(layer, chunk, tr, blk0=0):
    return pl.BlockSpec((None, None, 1, D_MODEL),
                        lambda i, *_: (layer, _row_group((blk0 + i) * tr), 0, chunk))


NORM_ROWS = 256
_VEC_SPEC = pl.BlockSpec((1, D_MODEL), lambda i: (0, 0))


def _row_spec(blk0=0):
    return pl.BlockSpec((NORM_ROWS, D_MODEL), lambda i: (blk0 + i, 0))


def _norm_mod_kernel(x_ref, g_ref, sc_ref, sh_ref, h_ref):
    y = _rms(x_ref[...], g_ref[...])
    h_ref[...] = (y * (1.0 + sc_ref[...]) + sh_ref[...]).astype(h_ref.dtype)


def _norm_mod(x_parts, g, mod, layer, sc_chunk, sh_chunk):
    tr = NORM_ROWS
    h = None
    for x, (row0, rows) in zip(x_parts, PIECES):
        blk0 = row0 // tr
        in_specs = [_row_spec(), _VEC_SPEC, _mod_spec(layer, sc_chunk, tr, blk0), _mod_spec(layer, sh_chunk, tr, blk0)]
        args = [x, g.reshape(1, D_MODEL), mod, mod]
        aliases = {}
        if h is not None:
            in_specs.append(pl.BlockSpec(memory_space=pl.ANY))
            aliases[len(args)] = 0
            args.append(h)
        h = pl.pallas_call(
            lambda x_ref, g_ref, sc_ref, sh_ref, *rest: _norm_mod_kernel(x_ref, g_ref, sc_ref, sh_ref, rest[-1]),
            grid=(rows // tr,),
            in_specs=in_specs,
            out_specs=_row_spec(blk0),
            out_shape=jax.ShapeDtypeStruct((T, D_MODEL), BF16),
            input_output_aliases=aliases,
            compiler_params=_cparams(("parallel",)),
            name="norm_mod",
        )(*args)
    return h


def _resid_kernel(x_ref, m_ref, ga_ref, gate_ref, *rest, emit_h):
    x = x_ref[...] + gate_ref[...] * _rms(m_ref[...].astype(F32), ga_ref[...])
    if emit_h:
        gb_ref, sc_ref, sh_ref, xo_ref, h_ref = rest
        xo_ref[...] = x
        y = _rms(x, gb_ref[...])
        h_ref[...] = (y * (1.0 + sc_ref[...]) + sh_ref[...]).astype(h_ref.dtype)
    else:
        (xo_ref,) = rest
        xo_ref[...] = x


def _resid(x, m, ga, mod, layer, gate_chunk, nxt=None, split_out=False):
    tr = NORM_ROWS
    x_parts = x if isinstance(x, (list, tuple)) else None
    pieces = PIECES if (x_parts is not None or split_out) else ((0, T),)
    emit_h = nxt is not None
    assert not (emit_h and split_out)
    x_new, h, outs = None, None, []
    for p, (row0, rows) in enumerate(pieces):
        blk0 = row0 // tr
        in_specs = [_row_spec(0 if x_parts is not None else blk0), _row_spec(blk0), _VEC_SPEC,
                    _mod_spec(layer, gate_chunk, tr, blk0)]
        args = [x_parts[p] if x_parts is not None else x, m, ga.reshape(1, D_MODEL), mod]
        if emit_h:
            gb, layer_b, sc_chunk, sh_chunk = nxt
            in_specs += [_VEC_SPEC, _mod_spec(layer_b, sc_chunk, tr, blk0), _mod_spec(layer_b, sh_chunk, tr, blk0)]
            args += [gb.reshape(1, D_MODEL), mod, mod]
        n_in = len(args)
        if split_out:
            out_specs = [_row_spec()]
            out_shape = [jax.ShapeDtypeStruct((rows, D_MODEL), F32)]
        else:
            out_specs = [_row_spec(blk0)]
            out_shape = [jax.ShapeDtypeStruct((T, D_MODEL), F32)]
        if emit_h:
            out_specs.append(_row_spec(blk0))
            out_shape.append(jax.ShapeDtypeStruct((T, D_MODEL), BF16))
        aliases = {}
        if x_new is not None and not split_out:
            for k, prev in enumerate((x_new, h) if emit_h else (x_new,)):
                in_specs.append(pl.BlockSpec(memory_space=pl.ANY))
                aliases[len(args)] = k
                args.append(prev)
        n_alias = len(aliases)

        def kern(*refs, n_in=n_in, n_alias=n_alias):
            _resid_kernel(*refs[:n_in], *refs[n_in + n_alias:], emit_h=emit_h)

        res = pl.pallas_call(
            kern,
            grid=(rows // tr,),
            in_specs=in_specs,
            out_specs=out_specs,
            out_shape=out_shape,
            input_output_aliases=aliases,
            compiler_params=_cparams(("parallel",)),
            name="resid_norm",
        )(*args)
        x_new = res[0]
        h = res[1] if emit_h else None
        outs.append(res[0])
    return (outs if split_out else x_new), h


MM_TILE_M = 1024


def _mm_kernel(a_ref, b_ref, o_ref, *, n_valid):
    acc = _dot(a_ref[...], b_ref[...])
    tn = o_ref.shape[1]
    if n_valid % tn:
        col = lax.broadcasted_iota(jnp.int32, acc.shape, 1)
        acc = jnp.where(col < n_valid - pl.program_id(1) * tn, acc, 0.0)
    o_ref[...] = acc.astype(o_ref.dtype)


def _matmul(a, w, layer, n_out, out_dtype, tn, name):
    m, k = a.shape
    tm = MM_TILE_M
    return pl.pallas_call(
        functools.partial(_mm_kernel, n_valid=w.shape[2]),
        grid=(m // tm, n_out // tn),
        in_specs=[pl.BlockSpec((tm, k), lambda i, j: (i, 0)),
                  pl.BlockSpec((None, k, tn), lambda i, j: (layer, 0, j))],
        out_specs=pl.BlockSpec((tm, tn), lambda i, j: (i, j)),
        out_shape=jax.ShapeDtypeStruct((m, n_out), out_dtype),
        compiler_params=_cparams(("parallel", "parallel")),
        name=name,
    )(a, w)


def _mm4_kernel(a0_ref, a1_ref, a2_ref, a3_ref, b_ref, o_ref):
    acc = _dot(a0_ref[...], b_ref[0 * GROUP_W:1 * GROUP_W, :])
    acc += _dot(a1_ref[...], b_ref[1 * GROUP_W:2 * GROUP_W, :])
    acc += _dot(a2_ref[...], b_ref[2 * GROUP_W:3 * GROUP_W, :])
    acc += _dot(a3_ref[...], b_ref[3 * GROUP_W:4 * GROUP_W, :])
    o_ref[...] = acc.astype(o_ref.dtype)


def _out_proj(parts, w, layer):
    tm, tn = MM_TILE_M, 1024
    a_spec = pl.BlockSpec((tm, GROUP_W), lambda i, j: (i, 0))
    return pl.pallas_call(
        _mm4_kernel,
        grid=(T // tm, D_MODEL // tn),
        in_specs=[a_spec, a_spec, a_spec, a_spec,
                  pl.BlockSpec((None, 4 * GROUP_W, tn), lambda i, j: (layer, 0, j))],
        out_specs=pl.BlockSpec((tm, tn), lambda i, j: (i, j)),
        out_shape=jax.ShapeDtypeStruct((T, D_MODEL), BF16),
        compiler_params=_cparams(("parallel", "parallel")),
        name="out_proj",
    )(*parts, w)


def _ffn_down(a, w, layer):
    m, kk = a.shape
    n = w.shape[2]
    tm, tn = 512, 512
    return pl.pallas_call(
        functools.partial(_mm_kernel, n_valid=n),
        grid=(m // tm, n // tn),
        in_specs=[pl.BlockSpec((tm, kk), lambda i, j: (i, 0)),
                  pl.BlockSpec((None, kk, tn), lambda i, j: (layer, 0, j))],
        out_specs=pl.BlockSpec((tm, tn), lambda i, j: (i, j)),
        out_shape=jax.ShapeDtypeStruct((m, n), BF16),
        compiler_params=_cparams(("parallel", "parallel")),
        name="ffn_down",
    )(a, w)


FFN_TILE_M = 2048
FFN_TILE_N = 256
FFN_SUB_M = 256
HALO_ROWS = 16
assert D_FF % FFN_TILE_N == 0


def _ffn_up_kernel(h_ref, halo_ref, wg_ref, wv_ref, cw_ref, cb_ref, o_ref):
    tm, tn = o_ref.shape
    sub = FFN_SUB_M
    ns = tm // sub
    wg, wv = wg_ref[...].astype(BF16), wv_ref[...].astype(BF16)
    row0 = pl.program_id(0) * tm
    seq_len = jnp.where(row0 < TP, SEQ, DEC_SEQ)
    r = lax.broadcasted_iota(jnp.int32, (sub, tn), 0)
    g0 = _dot(jnp.concatenate([halo_ref[...], h_ref[0:sub, :]], axis=0), wg)
    gh = g0[0:HALO_ROWS]
    gs = [g0[HALO_ROWS:]] + [_dot(h_ref[s * sub:(s + 1) * sub, :], wg) for s in range(1, ns)]
    for s in range(ns):
        val = _dot(h_ref[s * sub:(s + 1) * sub, :], wv)
        g = gs[s]
        before = gh[0:1, :] if s == 0 else gs[s - 1][sub - 1:sub, :]
        after = gh[1:2, :] if s == ns - 1 else gs[s + 1][0:1, :]
        pos = (row0 + s * sub + r) & (seq_len - 1)
        prev = jnp.where(r == 0, before, pltpu.roll(g, 1, 0))
        prev = jnp.where(pos == 0, 0.0, prev)
        nxt = jnp.where(r == sub - 1, after, pltpu.roll(g, sub - 1, 0))
        nxt = jnp.where(pos == seq_len - 1, 0.0, nxt)
        conv = prev * cw_ref[0:1, :] + g * cw_ref[1:2, :] + nxt * cw_ref[2:3, :] + cb_ref[...]
        o_ref[s * sub:(s + 1) * sub, :] = (_gelu(conv) * val).astype(o_ref.dtype)


def _ffn_up(h, halo, w_up, conv_w, conv_b, layer):
    tm, tn = FFN_TILE_M, FFN_TILE_N
    nj = D_FF // tn
    return pl.pallas_call(
        _ffn_up_kernel,
        grid=(T // tm, nj),
        in_specs=[pl.BlockSpec((tm, D_MODEL), lambda i, j: (i, 0), pipeline_mode=pl.Buffered(1)),
                  pl.BlockSpec((None, HALO_ROWS, D_MODEL), lambda i, j: (i, 0, 0)),
                  pl.BlockSpec((None, D_MODEL, tn), lambda i, j: (layer, 0, j)),
                  pl.BlockSpec((None, D_MODEL, tn), lambda i, j: (layer, 0, nj + j)),
                  pl.BlockSpec((None, CONV_W, tn), lambda i, j: (layer, 0, j)),
                  pl.BlockSpec((None, 1, tn), lambda i, j: (layer, 0, j))],
        out_specs=pl.BlockSpec((tm, tn), lambda i, j: (i, j)),
        out_shape=jax.ShapeDtypeStruct((T, D_FF), BF16),
        compiler_params=_cparams(("parallel", "parallel")),
        name="ffn_up",
    )(h, halo, w_up, w_up, conv_w, conv_b.reshape(DEPTH, 1, D_FF))


def _conv_halo(h):
    tm = FFN_TILE_M
    nt = T // tm
    ht = h.reshape(nt, tm, D_MODEL)
    zero = jnp.zeros((1, D_MODEL), h.dtype)
    before = jnp.concatenate([zero, ht[:-1, tm - 1]], axis=0)
    after = jnp.concatenate([ht[1:, 0], zero], axis=0)
    pad = jnp.zeros((nt, HALO_ROWS - 2, D_MODEL), h.dtype)
    return jnp.concatenate([before[:, None], after[:, None], pad], axis=1)


SGU_CHUNKS_PER_STEP = 4


def _sgu_kernel(u_ref, v_ref, g_ref, b_ref, w_ref, bias_ref, o_ref):
    c = SGU_CHUNK
    for ci in range(SGU_CHUNKS_PER_STEP):
        rows = slice(ci * c, (ci + 1) * c)
        u = _gelu(u_ref[rows, :])
        v = _gelu(v_ref[rows, :])
        vc = v - jnp.mean(v, axis=-1, keepdims=True)
        vn = vc * lax.rsqrt(jnp.mean(vc * vc, axis=-1, keepdims=True) + EPS) * g_ref[...] + b_ref[...]
        vb = vn.astype(BF16)
        for g in range(SGU_GROUPS):
            sl = slice(g * SGU_GCH, (g + 1) * SGU_GCH)
            mixed = _dot(w_ref[g], vb[:, sl]) + bias_ref[:, sl]
            o_ref[rows, sl] = (u[:, sl] * mixed).astype(o_ref.dtype)


def _sgu(z, ln_g, ln_b, w_s, b_s):
    c = SGU_CHUNK
    rows = SGU_CHUNKS_PER_STEP * c
    bias = jnp.repeat(b_s.T, SGU_GCH, axis=1)
    vec = pl.BlockSpec((1, SGU_DIM), lambda i: (0, 0))
    return pl.pallas_call(
        _sgu_kernel,
        grid=(T // rows,),
        in_specs=[pl.BlockSpec((rows, SGU_DIM), lambda i: (i, OFF_SGU // SGU_DIM)),
                  pl.BlockSpec((rows, SGU_DIM), lambda i: (i, OFF_SGU // SGU_DIM + 1)),
                  vec, vec,
                  pl.BlockSpec((SGU_GROUPS, c, c), lambda i: (0, 0, 0)),
                  pl.BlockSpec((c, SGU_DIM), lambda i: (0, 0))],
        out_specs=pl.BlockSpec((rows, SGU_DIM), lambda i: (i, 0)),
        out_shape=jax.ShapeDtypeStruct((T, SGU_DIM), BF16),
        compiler_params=_cparams(("parallel",)),
        name="sgu",
    )(z, z, ln_g.reshape(1, SGU_DIM), ln_b.reshape(1, SGU_DIM), w_s.astype(BF16), bias)


SMALL_LEVELS = tuple(m for m in (1, 2, 4) if m < SUBLANE)
BIG_LEVELS = tuple(m for m in (8, 16, 32, 64) if m < SCAN_C)


def _scan_consts():
    c = SCAN_C
    t = np.arange(c)[:, None]
    s = np.arange(c)[None, :]
    x = t ^ s
    tri = np.stack([s <= t, s >= t]).astype(np.float32)
    pair = [np.stack([x == 0, x == 0])]
    for m in SMALL_LEVELS:
        lvl = (x >= m) & (x < 2 * m)
        pair.append(np.stack([lvl & (t > s), lvl & (t < s)]))
    pair = np.stack(pair).astype(np.float32)
    rowq = np.stack([np.broadcast_to(((np.arange(c) & m) != 0)[:, None], (c, LANE)) for m in SMALL_LEVELS])
    hb = np.arange(c // 2)
    blk = np.stack([(hb[:, None] // m) == (hb[None, :] // m) for m in BIG_LEVELS])
    return (jnp.asarray(tri, BF16), jnp.asarray(pair, F32), jnp.asarray(rowq.astype(np.float32), F32),
            jnp.asarray(blk.astype(np.float32), F32))


def _split2(x):
    hi = x.astype(BF16)
    lo = (x - hi.astype(F32)).astype(BF16)
    return hi, lo


def _halves(a, m, second):
    off = m if second else 0
    return jnp.concatenate([a[j + off:j + off + m] for j in range(0, a.shape[0], 2 * m)], axis=0)


def _chunk_scans(probs, consts):
    tri_ref, pair_ref, rowq_ref, blk_ref = consts
    c, dk = probs[0][0].shape
    dirs = [1 if p[5] else 0 for p in probs]
    vbs = [p[2].astype(BF16) for p in probs]
    diag = [pair_ref[0, d] * _dot_nt(p[0].astype(BF16), p[1].astype(BF16)) for p, d in zip(probs, dirs)]
    bs = []
    for p, d in zip(probs, dirs):
        hi, lo = _split2(p[3])
        tri = tri_ref[d]
        bs.append(_dot(tri, hi) + _dot(tri, lo))
    inter = [_dot_nt((p[0] * jnp.exp2(b)).astype(BF16), p[4].astype(BF16)) for p, b in zip(probs, bs)]

    big = [[] for _ in probs]
    for li, m in enumerate(BIG_LEVELS):
        for pi, ((q, k, _, _, _, reverse), b) in enumerate(zip(probs, bs)):
            q_second = not reverse
            refs = [b[j + m:j + m + 1] if reverse else b[j + m - 1:j + m] for j in range(0, c, 2 * m)]
            ref = jnp.concatenate([jnp.broadcast_to(r, (m, dk)) for r in refs], axis=0)
            qf = _halves(q, m, q_second) * jnp.exp2(_halves(b, m, q_second) - ref)
            kf = _halves(k, m, not q_second) * jnp.exp2(ref - _halves(b, m, not q_second))
            s_l = _dot_nt(qf.astype(BF16), kf.astype(BF16))
            if 2 * m < c:
                s_l = s_l * blk_ref[li]
            big[pi].append(s_l.astype(BF16))

    scores = diag
    es = list(bs)
    for li, m in enumerate(SMALL_LEVELS):
        for pi, ((q, k, _, _, _, reverse), b) in enumerate(zip(probs, bs)):
            is_q = (rowq_ref[li] < 0.5) if reverse else (rowq_ref[li] > 0.5)
            sh_q, sh_e = (c - m, m) if reverse else (m, c - m)
            fac = jnp.exp2(jnp.where(is_q, b - pltpu.roll(es[pi], sh_q, 0), es[pi] - b))
            u = (jnp.where(is_q, q, k) * fac).astype(BF16)
            scores[pi] = scores[pi] + pair_ref[1 + li, dirs[pi]] * _dot_nt(u, u)
            if li + 1 < len(SMALL_LEVELS):
                es[pi] = jnp.where(is_q, es[pi], pltpu.roll(es[pi], sh_e, 0))

    outs = []
    for pi, ((q, k, _, _, st, reverse), b) in enumerate(zip(probs, bs)):
        o = inter[pi] + _dot(scores[pi].astype(BF16), vbs[pi])
        parts = [o[g:g + SUBLANE] for g in range(0, c, SUBLANE)]
        q_second = not reverse
        for li, m in enumerate(BIG_LEVELS):
            oc = _dot(big[pi][li], _halves(probs[pi][2], m, not q_second).astype(BF16))
            for jj, j in enumerate(range(0, c, 2 * m)):
                for g in range(0, m, SUBLANE):
                    dst = (j + (m if q_second else 0) + g) // SUBLANE
                    parts[dst] = parts[dst] + oc[jj * m + g:jj * m + g + SUBLANE]
        bl = b[0:1, :] if reverse else b[c - 1:c, :]
        kd = (k * jnp.exp2(bl - b)).astype(BF16)
        st_new = jnp.exp2(bl) * st + _dot_tn(vbs[pi], kd)
        outs.append((jnp.concatenate(parts, axis=0), st_new))
    return outs


def _scan_kernel(*refs, features, n_in, seq, nseg, hpb, dv, with_s0, emit_state):
    consts = refs[:4]
    pos = 4
    dir_refs = []
    for _ in range(1 if nseg == 1 else 2):
        dir_refs.append((refs[pos:pos + n_in], refs[pos + n_in]))
        pos += n_in + 1
    dir_refs = dir_refs * 2 if nseg == 1 else dir_refs
    gain_ref = refs[pos]
    pos += 1
    s0_ref = None
    if with_s0:
        s0_ref = refs[pos]
        pos += 1
    o_ref = refs[pos]
    pos += 1
    sf_ref = None
    if emit_state:
        sf_ref = refs[pos]
        pos += 1
    o_scr, st_scr = refs[pos:pos + 2]
    c = SCAN_C
    n = seq // c
    cps = n // nseg
    g = pl.program_id(2) if nseg > 1 else 0
    work = [(hh, reverse) for hh in range(hpb) for reverse in (False, True)]

    def init():
        st_scr[...] = s0_ref[...] if with_s0 else jnp.zeros(st_scr.shape, F32)

    def step(i, finalize):
        loc = {False: pl.multiple_of(i * c, c), True: pl.multiple_of((cps - 1 - i) * c, c)}
        pair = g * cps + i
        glob = {False: pl.multiple_of(pair * c, c), True: pl.multiple_of((n - 1 - pair) * c, c)}
        probs = [features(dir_refs[rev][0], loc[rev], hh, rev) + (st_scr[1 if rev else 0, hh], rev)
                 for hh, rev in work]
        for (hh, rev), (o, st) in zip(work, _chunk_scans(probs, consts)):
            st_scr[1 if rev else 0, hh] = st
            rows, cols = pl.ds(glob[rev], c), slice(hh * dv, (hh + 1) * dv)
            if finalize:
                gate = dir_refs[rev][1][pl.ds(loc[rev], c), cols]
                y = _rms(o + o_scr[rows, cols], gain_ref[...]) * _silu(gate)
                o_ref[rows, cols] = y.astype(o_ref.dtype)
            else:
                o_scr[rows, cols] = o

    def loop(lo, hi, finalize):
        lax.fori_loop(lo, hi, lambda i, carry: (step(i, finalize), carry)[1], 0)

    if nseg == 1:
        init()
        loop(0, n // 2, False)
        loop(n // 2, n, True)
    else:
        pl.when(g == 0)(init)
        pl.when(g < nseg // 2)(lambda: loop(0, cps, False))
        pl.when(g >= nseg // 2)(lambda: loop(0, cps, True))
    if emit_state:
        assert nseg == 1
        for d in range(2):
            for hh in range(hpb):
                sf_ref[d, hh] = st_scr[d, hh].T


def _hgrn_features(in_refs, r0, head, reverse):
    hq_ref, hf_ref, hb_ref, hi_ref, lb_ref = in_refs
    rows, cols = pl.ds(r0, SCAN_C), slice(head * HG_DK, (head + 1) * HG_DK)
    q = _silu(hq_ref[rows, cols]) * (HG_DK ** -0.5)
    lb = lb_ref[1 if reverse else 0][:, cols]
    f = lb + (1.0 - lb) * _sigmoid((hb_ref if reverse else hf_ref)[rows, cols])
    return q, 1.0 - f, hi_ref[rows, head * HG_DV:(head + 1) * HG_DV], jnp.log(f) * LOG2E


def _gla_features(in_refs, r0, head, reverse):
    gq_ref, gk_ref, gv_ref, low_ref, wgk_ref, bgk_ref = in_refs
    rows, cols = pl.ds(r0, SCAN_C), slice(head * GLA_DK, (head + 1) * GLA_DK)
    d = 1 if reverse else 0
    g = _dot(low_ref[rows, :].astype(BF16), wgk_ref[d][:, cols]) + bgk_ref[d][:, cols]
    la = (jnp.minimum(g, 0.0) - jnp.log(1.0 + jnp.exp(-jnp.abs(g)))) * (LOG2E / GLA_NORMALIZER)
    return (gq_ref[rows, cols] * (GLA_DK ** -0.5), gk_ref[rows, cols],
            gv_ref[rows, head * GLA_DV:(head + 1) * GLA_DV], la)


def _seq_geom(latent):
    seq = DEC_SEQ if latent else SEQ
    return seq, (DEC_BATCH if latent else BATCH), (TP // seq if latent else 0)


def _zspec(latent, off, width, per_head=True):
    seq, _, blk0 = _seq_geom(latent)
    assert off % width == 0
    if per_head:
        return pl.BlockSpec((seq, width), lambda b, h, *_: (blk0 + b, off // width + h))
    return pl.BlockSpec((seq, width), lambda b, h, *_: (blk0 + b, off // width))


def _const_spec(a):
    return pl.BlockSpec(a.shape, lambda *_, nd=a.ndim: (0,) * nd)


def _scan_geom(latent):
    return (2, 8) if latent else (2, 1)


def _zseg_spec(latent, off, width, reverse, per_head=True):
    seq, _, blk0 = _seq_geom(latent)
    nseg = _scan_geom(latent)[1]
    assert off % width == 0
    col0 = off // width

    def index(b, h, g=0):
        seg = (nseg - 1 - g) if reverse else g
        return ((blk0 + b) * nseg + seg, col0 + (h if per_head else 0))

    return pl.BlockSpec((seq // nseg, width), index)


def _scan_call(name, features, feat_fn, z, gate_off, gain, heads, dk, dv,
               latent, s0t, prev_out, prev_state, layer):
    seq, nb, row_blk0 = _seq_geom(latent)
    hpb, nseg = _scan_geom(latent)
    consts = _scan_consts()
    in_specs = [_const_spec(a) for a in consts]
    args = list(consts)
    for reverse in ((False,) if nseg == 1 else (False, True)):
        specs, fargs = feat_fn(reverse)
        n_in = len(specs)
        in_specs += list(specs) + [_zseg_spec(latent, gate_off, hpb * dv, reverse)]
        args += list(fargs) + [z]
    in_specs.append(pl.BlockSpec((1, dv), lambda *_: (0, 0)))
    args.append(gain.reshape(1, dv))
    if latent:
        in_specs.append(pl.BlockSpec((None, None, 2, hpb, dv, dk), lambda b, h, *_: (b, layer, 0, h, 0, 0)))
        args.append(s0t)
    out_specs = [pl.BlockSpec((seq, hpb * dv), lambda b, h, *_: (row_blk0 + b, h))]
    out_shape = [jax.ShapeDtypeStruct((T, heads * dv), BF16)]
    aliases = {}
    if prev_out is not None:
        in_specs.append(pl.BlockSpec(memory_space=pl.ANY))
        aliases[len(args)] = 0
        args.append(prev_out)
    emit_state = not latent
    if emit_state:
        out_specs.append(pl.BlockSpec((None, None, 2, hpb, dk, dv), lambda b, h, *_: (b, layer, 0, h, 0, 0)))
        out_shape.append(jax.ShapeDtypeStruct((BATCH, DEPTH, 2, heads, dk, dv), F32))
        if prev_state is not None:
            in_specs.append(pl.BlockSpec(memory_space=pl.ANY))
            aliases[len(args)] = 1
            args.append(prev_state)
    n_alias = len(aliases)

    def kern(*refs):
        n_inputs = len(args) - n_alias
        keep = refs[:n_inputs] + refs[n_inputs + n_alias:]
        _scan_kernel(*keep, features=features, n_in=n_in, seq=seq, nseg=nseg, hpb=hpb, dv=dv, with_s0=latent,
                     emit_state=emit_state)

    grid = (nb, heads // hpb) + ((nseg,) if nseg > 1 else ())
    sem = ("parallel", "parallel") + (("arbitrary",) if nseg > 1 else ())
    res = pl.pallas_call(
        kern,
        grid=grid,
        in_specs=in_specs,
        out_specs=out_specs,
        out_shape=out_shape,
        input_output_aliases=aliases,
        scratch_shapes=[pltpu.VMEM((seq, hpb * dv), F32), pltpu.VMEM((2, hpb, dv, dk), F32)],
        compiler_params=_cparams(sem),
        name=name,
    )(*args)
    return (res[0], res[1]) if emit_state else (res[0], None)


def _hgrn(z, lb, gain, s0t, prev_state, layer):
    lb3 = lb.reshape(2, 1, HG_K)
    o = state = None
    for latent in (False, True):
        hpb = _scan_geom(latent)[0]

        def feat_fn(reverse, latent=latent, hpb=hpb):
            zs = functools.partial(_zseg_spec, latent, reverse=reverse)
            return ([zs(OFF_HQ, hpb * HG_DK), zs(OFF_HFF, hpb * HG_DK), zs(OFF_HFB, hpb * HG_DK),
                     zs(OFF_HI, hpb * HG_DV), pl.BlockSpec((2, 1, hpb * HG_DK), lambda b, h, *_: (0, 0, h))],
                    [z, z, z, z, lb3])

        o, st = _scan_call("hgrn_latent" if latent else "hgrn_prompt", _hgrn_features, feat_fn,
                           z, OFF_HG, gain, HG_HEADS, HG_DK, HG_DV, latent, s0t, o, prev_state, layer)
        state = st if st is not None else state
    return o, state


def _gla(z, w_gk, b_gk, gain, s0t, prev_state, layer):
    wpad = jnp.zeros((2, LANE, GLA_HEADS * GLA_DK), F32)
    wpad = wpad.at[0, 0:GLA_RANK].set(w_gk[0]).at[1, GLA_RANK:2 * GLA_RANK].set(w_gk[1]).astype(BF16)
    b3 = b_gk.reshape(2, 1, GLA_HEADS * GLA_DK)
    o = state = None
    for latent in (False, True):
        hpb = _scan_geom(latent)[0]

        def feat_fn(reverse, latent=latent, hpb=hpb):
            zs = functools.partial(_zseg_spec, latent, reverse=reverse)
            return ([zs(OFF_GQ, hpb * GLA_DK), zs(OFF_GK, hpb * GLA_DK), zs(OFF_GV, hpb * GLA_DV),
                     zs(OFF_GLF, LANE, per_head=False),
                     pl.BlockSpec((2, LANE, hpb * GLA_DK), lambda b, h, *_: (0, 0, h)),
                     pl.BlockSpec((2, 1, hpb * GLA_DK), lambda b, h, *_: (0, 0, h))],
                    [z, z, z, z, wpad, b3])

        o, st = _scan_call("gla_latent" if latent else "gla_prompt", _gla_features, feat_fn,
                           z, OFF_GG, gain, GLA_HEADS, GLA_DK, GLA_DV, latent, s0t, o, prev_state, layer)
        state = st if st is not None else state
    return o, state


GQA = N_HEADS // N_KV_HEADS


def _ctx_attn_kernel(q_ref, k_ref, v_ref, sink_ref, o_ref):
    hd = HEAD_DIM
    k = k_ref[...].astype(BF16)
    v = v_ref[...].astype(BF16)
    heads = range(GQA)
    ss = [_dot_nt((q_ref[:, g * hd:(g + 1) * hd] * (hd ** -0.5)).astype(BF16), k) for g in heads]
    ps, dens = [], []
    for g in heads:
        sink = sink_ref[g][:, 0:1]
        m = jnp.maximum(jnp.max(ss[g], axis=-1, keepdims=True), sink)
        p = jnp.exp(ss[g] - m)
        dens.append(jnp.sum(p, axis=-1, keepdims=True) + jnp.exp(sink - m))
        ps.append(p.astype(BF16))
    for g in heads:
        o_ref[:, g * hd:(g + 1) * hd] = (_dot(ps[g], v) / dens[g]).astype(o_ref.dtype)


def _ctx_attention(z, sink3):
    hd = HEAD_DIM
    return pl.pallas_call(
        _ctx_attn_kernel,
        grid=(BATCH, N_KV_HEADS),
        in_specs=[_zspec(False, OFF_AQ, GQA * hd),
                  pl.BlockSpec((SEQ, hd), lambda b, kh: (b, OFF_AK // hd + kh)),
                  pl.BlockSpec((SEQ, hd), lambda b, kh: (b, OFF_AV // hd + kh)),
                  pl.BlockSpec((GQA, 1, LANE), lambda b, kh: (kh, 0, 0))],
        out_specs=pl.BlockSpec((SEQ, GQA * hd), lambda b, kh: (b, kh)),
        out_shape=jax.ShapeDtypeStruct((T, N_HEADS * hd), BF16),
        compiler_params=_cparams(("parallel", "parallel")),
        name="ctx_attention",
    )(z, z, z, sink3)


def _rope(x, cos_f, sin_f):
    return x * cos_f + pltpu.roll(x, HEAD_DIM // 2, 1) * sin_f


def _lat_attn_kernel(q_ref, k_ref, v_ref, ck_ref, cv_ref, cos_ref, sin_ref, sink_ref, o_ref, kr_scr, vr_scr):
    blk, hd, seq = ATT_BLOCK, HEAD_DIM, DEC_SEQ
    nb = seq // blk

    zeros = jnp.zeros((blk, hd), BF16)
    for scr in (kr_scr, vr_scr):
        scr[0:blk, :] = zeros
        scr[blk + seq:2 * blk + seq, :] = zeros

    def fill(n, carry):
        r0 = pl.multiple_of(n * blk, blk)
        rows = pl.ds(r0, blk)
        kr_scr[pl.ds(r0 + blk, blk), :] = _rope(k_ref[rows, :], cos_ref[rows, :], sin_ref[rows, :]).astype(BF16)
        vr_scr[pl.ds(r0 + blk, blk), :] = v_ref[rows, :].astype(BF16)
        return carry

    lax.fori_loop(0, nb, fill, 0)

    ck = ck_ref[...].astype(BF16)
    cv = cv_ref[...].astype(BF16)
    sink = jnp.concatenate([jnp.broadcast_to(sink_ref[g][:, 0:1], (blk, 1)) for g in range(GQA)], axis=0)
    qi = lax.broadcasted_iota(jnp.int32, (GQA * blk, 3 * blk), 0) & (blk - 1)
    kj = lax.broadcasted_iota(jnp.int32, (GQA * blk, 3 * blk), 1)
    window_bias = jnp.where(jnp.abs(kj - qi - blk) <= WINDOW, 0.0, -jnp.inf)

    per_iter = 2

    def qblocks(it, carry):
        ns = [it * per_iter + u for u in range(per_iter)]
        r0s = [pl.multiple_of(n * blk, blk) for n in ns]
        qs = []
        for r0 in r0s:
            rows = pl.ds(r0, blk)
            cos_f, sin_f = cos_ref[rows, :], sin_ref[rows, :]
            qs.append(jnp.concatenate(
                [(_rope(q_ref[rows, g * hd:(g + 1) * hd], cos_f, sin_f) * (hd ** -0.5)).astype(BF16)
                 for g in range(GQA)], axis=0))
        bands = [pl.ds(r0, 3 * blk) for r0 in r0s]
        s_locs = [_dot_nt(q, kr_scr[band, :]) for q, band in zip(qs, bands)]
        s_ctxs = [_dot_nt(q, ck) for q in qs]
        ps = []
        for n, s_loc, s_ctx in zip(ns, s_locs, s_ctxs):
            key_pos = kj + (n - 1) * blk
            s_loc = s_loc + window_bias
            s_loc = jnp.where(key_pos >= 0, s_loc, -jnp.inf)
            s_loc = jnp.where(key_pos < seq, s_loc, -jnp.inf)
            m = jnp.maximum(jnp.maximum(jnp.max(s_loc, axis=-1, keepdims=True),
                                        jnp.max(s_ctx, axis=-1, keepdims=True)), sink)
            p_loc = jnp.exp(s_loc - m)
            p_ctx = jnp.exp(s_ctx - m)
            den = (jnp.sum(p_loc, axis=-1, keepdims=True) + jnp.sum(p_ctx, axis=-1, keepdims=True)
                   + jnp.exp(sink - m))
            ps.append((p_ctx.astype(BF16), p_loc.astype(BF16), den))
        for r0, band, (p_ctx, p_loc, den) in zip(r0s, bands, ps):
            o = (_dot(p_ctx, cv) + _dot(p_loc, vr_scr[band, :])) / den
            for g in range(GQA):
                o_ref[pl.ds(r0, blk), g * hd:(g + 1) * hd] = o[g * blk:(g + 1) * blk].astype(o_ref.dtype)
        return carry

    lax.fori_loop(0, nb // per_iter, qblocks, 0)


def _rope_tables():
    n = jnp.arange(DEC_SEQ)
    row = (n // GRID_W).astype(F32)
    col = (n % GRID_W).astype(F32)
    n_freq = HEAD_DIM // 4
    inv = ROPE_THETA ** (-jnp.arange(n_freq, dtype=F32) / n_freq)
    ang = jnp.concatenate([row[:, None] * inv, col[:, None] * inv], axis=-1)
    cos, sin = jnp.cos(ang), jnp.sin(ang)
    return jnp.concatenate([cos, cos], axis=-1), jnp.concatenate([-sin, sin], axis=-1)


def _lat_attention(z, ck, cv, sink3, prev_out):
    hd = HEAD_DIM
    seq, _, blk0 = _seq_geom(True)
    cos_f, sin_f = _rope_tables()
    tab = pl.BlockSpec((seq, hd), lambda b, kh: (0, 0))
    cache = pl.BlockSpec((None, PAST_LEN, hd), lambda b, kh: (b, 0, kh))
    return pl.pallas_call(
        lambda *refs: _lat_attn_kernel(*refs[:8], *refs[9:]),
        grid=(DEC_BATCH, N_KV_HEADS),
        in_specs=[_zspec(True, OFF_AQ, GQA * hd),
                  pl.BlockSpec((seq, hd), lambda b, kh: (blk0 + b, OFF_AK // hd + kh)),
                  pl.BlockSpec((seq, hd), lambda b, kh: (blk0 + b, OFF_AV // hd + kh)),
                  cache, cache, tab, tab,
                  pl.BlockSpec((GQA, 1, LANE), lambda b, kh: (kh, 0, 0)),
                  pl.BlockSpec(memory_space=pl.ANY)],
        out_specs=pl.BlockSpec((seq, GQA * hd), lambda b, kh: (blk0 + b, kh)),
        out_shape=jax.ShapeDtypeStruct((T, N_HEADS * hd), BF16),
        input_output_aliases={8: 0},
        scratch_shapes=[pltpu.VMEM((seq + 2 * ATT_BLOCK, hd), BF16), pltpu.VMEM((seq + 2 * ATT_BLOCK, hd), BF16)],
        compiler_params=_cparams(("parallel", "parallel")),
        name="lat_attention",
    )(z, z, z, ck, cv, cos_f, sin_f, sink3, prev_out)


def kernel(x_prompt, x_sample, cache_k, cache_v, state_hgrn, state_gla, c, c_ctx, w_ada, b_ada, norm_g, w_in,
           w_out, sgu_ln_g, sgu_ln_b, sgu_w, sgu_b, hgrn_lb_logits, hgrn_norm_g, attn_sink, gla_w_gk, gla_b_gk,
           gla_norm_g, ffn_w_up, ffn_conv_w, ffn_conv_b, ffn_w_down):
    assert TP % DEC_SEQ == 0 and DEC_BATCH + 1 <= MOD_ROWS
    lb_all = jnp.cumsum(jax.nn.softmax(hgrn_lb_logits.astype(F32), axis=0), axis=0)
    lb_all = lb_all - lb_all[0:1]

    cvec = jnp.zeros((MOD_ROWS, D_MODEL), F32).at[0].set(c_ctx).at[1:1 + DEC_BATCH].set(c)
    mod = _ada_mod(cvec, w_ada, b_ada).reshape(DEPTH, MOD_ROWS, 1, 6 * D_MODEL)
    SH1, SC1, G1, SH2, SC2, G2 = range(6)

    x = [x_prompt.reshape(TP, D_MODEL), x_sample.reshape(TL, D_MODEL)]
    hg_s0t = jnp.swapaxes(state_hgrn, -1, -2)
    gl_s0t = jnp.swapaxes(state_gla, -1, -2)
    ck_all = cache_k.reshape(DEC_BATCH, DEPTH, PAST_LEN, N_KV_HEADS * HEAD_DIM)
    cv_all = cache_v.reshape(DEC_BATCH, DEPTH, PAST_LEN, N_KV_HEADS * HEAD_DIM)

    h = _norm_mod(x, norm_g[0, 0], mod, 0, SC1, SH1)
    ks_new, vs_new = [], []
    hg_state = gl_state = None
    w_in_b, w_up_b = w_in.astype(BF16), ffn_w_up
    w_out_b, w_down_b = w_out.astype(BF16), ffn_w_down.astype(BF16)
    for l in range(DEPTH):
        sink3 = jnp.broadcast_to(attn_sink[l].astype(F32)[:, None, None], (N_HEADS, 1, LANE))

        z = _matmul(h, w_in_b, l, Z_COLS, F32, Z_TILE_N, "in_proj")
        o_sgu = _sgu(z, sgu_ln_g[l], sgu_ln_b[l], sgu_w[l], sgu_b[l])
        o_hg, hg_state = _hgrn(z, lb_all[l], hgrn_norm_g[l], hg_s0t, hg_state, l)
        o_att = _ctx_attention(z, sink3)
        o_att = _lat_attention(z, ck_all[:, l], cv_all[:, l], sink3, o_att)
        o_gl, gl_state = _gla(z, gla_w_gk[l], gla_b_gk[l], gla_norm_g[l], gl_s0t, gl_state, l)
        m = _out_proj([o_sgu, o_hg, o_att, o_gl], w_out_b, l)
        x, h = _resid(x, m, norm_g[l, 1], mod, l, G1, nxt=(norm_g[l, 2], l, SC2, SH2))
        ks_new.append(z[:TP, OFF_AK:OFF_AK + N_KV_HEADS * HEAD_DIM].reshape(BATCH, SEQ, N_KV_HEADS, HEAD_DIM))
        vs_new.append(z[:TP, OFF_AV:OFF_AV + N_KV_HEADS * HEAD_DIM].reshape(BATCH, SEQ, N_KV_HEADS, HEAD_DIM))

        act = _ffn_up(h, _conv_halo(h), w_up_b, ffn_conv_w, ffn_conv_b, l)
        f = _ffn_down(act, w_down_b, l)
        if l + 1 < DEPTH:
            x, h = _resid(x, f, norm_g[l, 3], mod, l, G2, nxt=(norm_g[l + 1, 0], l + 1, SC1, SH1))
        else:
            x, h = _resid(x, f, norm_g[l, 3], mod, l, G2, split_out=True)

    y_prompt = x[0].reshape(BATCH, SEQ, D_MODEL)
    y_sample = x[1].reshape(DEC_BATCH, DEC_SEQ, D_MODEL)
    return (y_prompt, y_sample, jnp.stack(ks_new, axis=1), jnp.stack(vs_new, axis=1), hg_state, gl_state)
```

```python
import functools

import numpy as np
import jax
import jax.numpy as jnp
from jax import lax
from jax.experimental import pallas as pl
from jax.experimental.pallas import tpu as pltpu

D_MODEL = 4096
BATCH = 32
SEQ = 256
DEPTH = 2
DEC_BATCH = 2
DEC_SEQ = 4096
PAST_LEN = 512
GRID_W = 64
GROUP_W = D_MODEL // 4
SGU_CHUNK = 128
SGU_GROUPS = 4
SGU_DIM = GROUP_W
SGU_GCH = SGU_DIM // SGU_GROUPS
HG_HEADS = 8
HG_DK = 128
HG_DV = GROUP_W // HG_HEADS
HG_K = HG_HEADS * HG_DK
N_HEADS = 8
N_KV_HEADS = 2
HEAD_DIM = GROUP_W // N_HEADS
WINDOW = 128
ATT_BLOCK = 128
ROPE_THETA = 10000.0
GLA_HEADS = 4
GLA_DK = 128
GLA_DV = GROUP_W // GLA_HEADS
GLA_RANK = 16
GLA_NORMALIZER = 16.0
D_FF = 11008
CONV_W = 3
EPS = 1e-6

F32 = jnp.float32
BF16 = jnp.bfloat16
LOG2E = 1.4426950408889634

IN_SIZES = (2 * SGU_DIM,
            HG_K, HG_K, HG_K, HG_HEADS * HG_DV, HG_HEADS * HG_DV,
            N_HEADS * HEAD_DIM, N_KV_HEADS * HEAD_DIM, N_KV_HEADS * HEAD_DIM,
            GLA_HEADS * GLA_DK, GLA_HEADS * GLA_DK, GLA_HEADS * GLA_DV, GLA_HEADS * GLA_DV,
            GLA_RANK, GLA_RANK)
D_IN = sum(IN_SIZES)
_OFF = [0] + [int(c) for c in np.cumsum(IN_SIZES)]
(OFF_SGU, OFF_HQ, OFF_HFF, OFF_HFB, OFF_HI, OFF_HG, OFF_AQ, OFF_AK, OFF_AV,
 OFF_GQ, OFF_GK, OFF_GV, OFF_GG, OFF_GLF, OFF_GLB) = _OFF[:-1]

LANE = 128
SUBLANE = 8
VMEM_LIMIT = 56 * 1024 * 1024

Z_TILE_N = 1024
Z_COLS = -(-D_IN // Z_TILE_N) * Z_TILE_N
SCAN_C = 128
MOD_ROWS = 16

TP = BATCH * SEQ
TL = DEC_BATCH * DEC_SEQ
T = TP + TL
PIECES = ((0, TP), (TP, TL))


def _cparams(sem):
    return pltpu.CompilerParams(dimension_semantics=sem, vmem_limit_bytes=VMEM_LIMIT)


def _row_group(row0):
    return jnp.where(row0 < TP, 0, 1 + (row0 - TP) // DEC_SEQ)


def _dot(a, b):
    return jnp.dot(a, b, preferred_element_type=F32)


def _dot_nt(a, b):
    return lax.dot_general(a, b, (((1,), (1,)), ((), ())), preferred_element_type=F32)


def _dot_tn(a, b):
    return lax.dot_general(a, b, (((0,), (0,)), ((), ())), preferred_element_type=F32)


def _sigmoid(x):
    return 1.0 / (1.0 + jnp.exp(-x))


def _silu(x):
    return x * _sigmoid(x)


def _gelu(x):
    return 0.5 * x * (1.0 + lax.erf(x * (2.0 ** -0.5)))


def _rms(x, g):
    return x * lax.rsqrt(jnp.mean(x * x, axis=-1, keepdims=True) + EPS) * g


def _ada_kernel(c_ref, w_ref, b_ref, o_ref):
    c = _silu(c_ref[...]).astype(BF16)
    o_ref[...] = _dot(c, w_ref[...].astype(BF16)) + b_ref[...]


def _ada_mod(cvec, w_ada, b_ada):
    tn = 512
    n = w_ada.shape[-1]
    return pl.pallas_call(
        _ada_kernel,
        grid=(DEPTH, n // tn),
        in_specs=[pl.BlockSpec((MOD_ROWS, D_MODEL), lambda l, j: (0, 0)),
                  pl.BlockSpec((None, D_MODEL, tn), lambda l, j: (l, 0, j)),
                  pl.BlockSpec((None, 1, tn), lambda l, j: (l, 0, j))],
        out_specs=pl.BlockSpec((None, MOD_ROWS, tn), lambda l, j: (l, 0, j)),
        out_shape=jax.ShapeDtypeStruct((DEPTH, MOD_ROWS, n), F32),
        compiler_params=_cparams(("parallel", "parallel")),
        name="ada_mod",
    )(cvec, w_ada, b_ada.reshape(DEPTH, 1, n))


def _mod_spec(layer, chunk, tr, blk0=0):
    return pl.BlockSpec((None, None, 1, D_MODEL),
                        lambda i, *_: (layer, _row_group((blk0 + i) * tr), 0, chunk))


NORM_ROWS = 256
_VEC_SPEC = pl.BlockSpec((1, D_MODEL), lambda i: (0, 0))


def _row_spec(blk0=0):
    return pl.BlockSpec((NORM_ROWS, D_MODEL), lambda i: (blk0 + i, 0))


def _norm_mod_kernel(x_ref, g_ref, sc_ref, sh_ref, h_ref):
    y = _rms(x_ref[...], g_ref[...])
    h_ref[...] = (y * (1.0 + sc_ref[...]) + sh_ref[...]).astype(h_ref.dtype)


def _norm_mod(x_parts, g, mod, layer, sc_chunk, sh_chunk):
    tr = NORM_ROWS
    h = None
    for x, (row0, rows) in zip(x_parts, PIECES):
        blk0 = row0 // tr
        in_specs = [_row_spec(), _VEC_SPEC, _mod_spec(layer, sc_chunk, tr, blk0), _mod_spec(layer, sh_chunk, tr, blk0)]
        args = [x, g.reshape(1, D_MODEL), mod, mod]
        aliases = {}
        if h is not None:
            in_specs.append(pl.BlockSpec(memory_space=pl.ANY))
            aliases[len(args)] = 0
            args.append(h)
        h = pl.pallas_call(
            lambda x_ref, g_ref, sc_ref, sh_ref, *rest: _norm_mod_kernel(x_ref, g_ref, sc_ref, sh_ref, rest[-1]),
            grid=(rows // tr,),
            in_specs=in_specs,
            out_specs=_row_spec(blk0),
            out_shape=jax.ShapeDtypeStruct((T, D_MODEL), BF16),
            input_output_aliases=aliases,
            compiler_params=_cparams(("parallel",)),
            name="norm_mod",
        )(*args)
    return h


def _resid_kernel(x_ref, m_ref, ga_ref, gate_ref, *rest, emit_h):
    x = x_ref[...] + gate_ref[...] * _rms(m_ref[...].astype(F32), ga_ref[...])
    if emit_h:
        gb_ref, sc_ref, sh_ref, xo_ref, h_ref = rest
        xo_ref[...] = x
        y = _rms(x, gb_ref[...])
        h_ref[...] = (y * (1.0 + sc_ref[...]) + sh_ref[...]).astype(h_ref.dtype)
    else:
        (xo_ref,) = rest
        xo_ref[...] = x


def _resid(x, m, ga, mod, layer, gate_chunk, nxt=None, split_out=False):
    tr = NORM_ROWS
    x_parts = x if isinstance(x, (list, tuple)) else None
    pieces = PIECES if (x_parts is not None or split_out) else ((0, T),)
    emit_h = nxt is not None
    assert not (emit_h and split_out)
    x_new, h, outs = None, None, []
    for p, (row0, rows) in enumerate(pieces):
        blk0 = row0 // tr
        in_specs = [_row_spec(0 if x_parts is not None else blk0), _row_spec(blk0), _VEC_SPEC,
                    _mod_spec(layer, gate_chunk, tr, blk0)]
        args = [x_parts[p] if x_parts is not None else x, m, ga.reshape(1, D_MODEL), mod]
        if emit_h:
            gb, layer_b, sc_chunk, sh_chunk = nxt
            in_specs += [_VEC_SPEC, _mod_spec(layer_b, sc_chunk, tr, blk0), _mod_spec(layer_b, sh_chunk, tr, blk0)]
            args += [gb.reshape(1, D_MODEL), mod, mod]
        n_in = len(args)
        if split_out:
            out_specs = [_row_spec()]
            out_shape = [jax.ShapeDtypeStruct((rows, D_MODEL), F32)]
        else:
            out_specs = [_row_spec(blk0)]
            out_shape = [jax.ShapeDtypeStruct((T, D_MODEL), F32)]
        if emit_h:
            out_specs.append(_row_spec(blk0))
            out_shape.append(jax.ShapeDtypeStruct((T, D_MODEL), BF16))
        aliases = {}
        if x_new is not None and not split_out:
            for k, prev in enumerate((x_new, h) if emit_h else (x_new,)):
                in_specs.append(pl.BlockSpec(memory_space=pl.ANY))
                aliases[len(args)] = k
                args.append(prev)
        n_alias = len(aliases)

        def kern(*refs, n_in=n_in, n_alias=n_alias):
            _resid_kernel(*refs[:n_in], *refs[n_in + n_alias:], emit_h=emit_h)

        res = pl.pallas_call(
            kern,
            grid=(rows // tr,),
            in_specs=in_specs,
            out_specs=out_specs,
            out_shape=out_shape,
            input_output_aliases=aliases,
            compiler_params=_cparams(("parallel",)),
            name="resid_norm",
        )(*args)
        x_new = res[0]
        h = res[1] if emit_h else None
        outs.append(res[0])
    return (outs if split_out else x_new), h


MM_TILE_M = 1024


def _mm_kernel(a_ref, b_ref, o_ref, *, n_valid):
    acc = _dot(a_ref[...], b_ref[...])
    tn = o_ref.shape[1]
    if n_valid % tn:
        col = lax.broadcasted_iota(jnp.int32, acc.shape, 1)
        acc = jnp.where(col < n_valid - pl.program_id(1) * tn, acc, 0.0)
    o_ref[...] = acc.astype(o_ref.dtype)


def _matmul(a, w, layer, n_out, out_dtype, tn, name):
    m, k = a.shape
    tm = MM_TILE_M
    return pl.pallas_call(
        functools.partial(_mm_kernel, n_valid=w.shape[2]),
        grid=(m // tm, n_out // tn),
        in_specs=[pl.BlockSpec((tm, k), lambda i, j: (i, 0)),
                  pl.BlockSpec((None, k, tn), lambda i, j: (layer, 0, j))],
        out_specs=pl.BlockSpec((tm, tn), lambda i, j: (i, j)),
        out_shape=jax.ShapeDtypeStruct((m, n_out), out_dtype),
        compiler_params=_cparams(("parallel", "parallel")),
        name=name,
    )(a, w)


def _mm4_kernel(a0_ref, a1_ref, a2_ref, a3_ref, b_ref, o_ref):
    acc = _dot(a0_ref[...], b_ref[0 * GROUP_W:1 * GROUP_W, :])
    acc += _dot(a1_ref[...], b_ref[1 * GROUP_W:2 * GROUP_W, :])
    acc += _dot(a2_ref[...], b_ref[2 * GROUP_W:3 * GROUP_W, :])
    acc += _dot(a3_ref[...], b_ref[3 * GROUP_W:4 * GROUP_W, :])
    o_ref[...] = acc.astype(o_ref.dtype)


def _out_proj(parts, w, layer):
    tm, tn = MM_TILE_M, 1024
    a_spec = pl.BlockSpec((tm, GROUP_W), lambda i, j: (i, 0))
    return pl.pallas_call(
        _mm4_kernel,
        grid=(T // tm, D_MODEL // tn),
        in_specs=[a_spec, a_spec, a_spec, a_spec,
                  pl.BlockSpec((None, 4 * GROUP_W, tn), lambda i, j: (layer, 0, j))],
        out_specs=pl.BlockSpec((tm, tn), lambda i, j: (i, j)),
        out_shape=jax.ShapeDtypeStruct((T, D_MODEL), BF16),
        compiler_params=_cparams(("parallel", "parallel")),
        name="out_proj",
    )(*parts, w)


def _ffn_down(a, w, layer):
    m, kk = a.shape
    n = w.shape[2]
    tm, tn = 512, 512
    return pl.pallas_call(
        functools.partial(_mm_kernel, n_valid=n),
        grid=(m // tm, n // tn),
        in_specs=[pl.BlockSpec((tm, kk), lambda i, j: (i, 0)),
                  pl.BlockSpec((None, kk, tn), lambda i, j: (layer, 0, j))],
        out_specs=pl.BlockSpec((tm, tn), lambda i, j: (i, j)),
        out_shape=jax.ShapeDtypeStruct((m, n), BF16),
        compiler_params=_cparams(("parallel", "parallel")),
        name="ffn_down",
    )(a, w)


FFN_TILE_M = 2048
FFN_TILE_N = 256
FFN_SUB_M = 256
HALO_ROWS = 16
assert D_FF % FFN_TILE_N == 0


def _ffn_up_kernel(h_ref, halo_ref, wg_ref, wv_ref, cw_ref, cb_ref, wd_ref, o_ref, wdb_ref):
    wdb_ref[...] = wd_ref[...].astype(wdb_ref.dtype)
    tm, tn = o_ref.shape
    sub = FFN_SUB_M
    ns = tm // sub
    wg, wv = wg_ref[...].astype(BF16), wv_ref[...].astype(BF16)
    row0 = pl.program_id(0) * tm
    seq_len = jnp.where(row0 < TP, SEQ, DEC_SEQ)
    r = lax.broadcasted_iota(jnp.int32, (sub, tn), 0)
    g0 = _dot(jnp.concatenate([halo_ref[...], h_ref[0:sub, :]], axis=0), wg)
    gh = g0[0:HALO_ROWS]
    gs = [g0[HALO_ROWS:]] + [_dot(h_ref[s * sub:(s + 1) * sub, :], wg) for s in range(1, ns)]
    for s in range(ns):
        val = _dot(h_ref[s * sub:(s + 1) * sub, :], wv)
        g = gs[s]
        before = gh[0:1, :] if s == 0 else gs[s - 1][sub - 1:sub, :]
        after = gh[1:2, :] if s == ns - 1 else gs[s + 1][0:1, :]
        pos = (row0 + s * sub + r) & (seq_len - 1)
        prev = jnp.where(r == 0, before, pltpu.roll(g, 1, 0))
        prev = jnp.where(pos == 0, 0.0, prev)
        nxt = jnp.where(r == sub - 1, after, pltpu.roll(g, sub - 1, 0))
        nxt = jnp.where(pos == seq_len - 1, 0.0, nxt)
        conv = prev * cw_ref[0:1, :] + g * cw_ref[1:2, :] + nxt * cw_ref[2:3, :] + cb_ref[...]
        o_ref[s * sub:(s + 1) * sub, :] = (_gelu(conv) * val).astype(o_ref.dtype)


def _ffn_up(h, halo, w_up, conv_w, conv_b, w_down, layer):
    tm, tn = FFN_TILE_M, FFN_TILE_N
    nj = D_FF // tn
    steps = (T // tm) * nj
    wd_rows = D_FF // steps
    assert D_FF % steps == 0 and wd_rows % 16 == 0
    wd_in = pl.BlockSpec((None, wd_rows, D_MODEL), lambda i, j: (layer, i * nj + j, 0))
    wd_out = pl.BlockSpec((None, wd_rows, D_MODEL), lambda i, j: (0, i * nj + j, 0))
    return pl.pallas_call(
        _ffn_up_kernel,
        grid=(T // tm, nj),
        in_specs=[pl.BlockSpec((tm, D_MODEL), lambda i, j: (i, 0), pipeline_mode=pl.Buffered(1)),
                  pl.BlockSpec((None, HALO_ROWS, D_MODEL), lambda i, j: (i, 0, 0)),
                  pl.BlockSpec((None, D_MODEL, tn), lambda i, j: (layer, 0, j)),
                  pl.BlockSpec((None, D_MODEL, tn), lambda i, j: (layer, 0, nj + j)),
                  pl.BlockSpec((None, CONV_W, tn), lambda i, j: (layer, 0, j)),
                  pl.BlockSpec((None, 1, tn), lambda i, j: (layer, 0, j)),
                  wd_in],
        out_specs=[pl.BlockSpec((tm, tn), lambda i, j: (i, j)), wd_out],
        out_shape=[jax.ShapeDtypeStruct((T, D_FF), BF16), jax.ShapeDtypeStruct((1, D_FF, D_MODEL), BF16)],
        compiler_params=_cparams(("parallel", "parallel")),
        name="ffn_up",
    )(h, halo, w_up, w_up, conv_w, conv_b.reshape(DEPTH, 1, D_FF), w_down)


def _conv_halo(h):
    tm = FFN_TILE_M
    nt = T // tm
    ht = h.reshape(nt, tm, D_MODEL)
    zero = jnp.zeros((1, D_MODEL), h.dtype)
    before = jnp.concatenate([zero, ht[:-1, tm - 1]], axis=0)
    after = jnp.concatenate([ht[1:, 0], zero], axis=0)
    pad = jnp.zeros((nt, HALO_ROWS - 2, D_MODEL), h.dtype)
    return jnp.concatenate([before[:, None], after[:, None], pad], axis=1)


SGU_CHUNKS_PER_STEP = 4


def _sgu_kernel(u_ref, v_ref, g_ref, b_ref, w_ref, bias_ref, o_ref):
    c = SGU_CHUNK
    for ci in range(SGU_CHUNKS_PER_STEP):
        rows = slice(ci * c, (ci + 1) * c)
        u = _gelu(u_ref[rows, :])
        v = _gelu(v_ref[rows, :])
        vc = v - jnp.mean(v, axis=-1, keepdims=True)
        vn = vc * lax.rsqrt(jnp.mean(vc * vc, axis=-1, keepdims=True) + EPS) * g_ref[...] + b_ref[...]
        vb = vn.astype(BF16)
        for g in range(SGU_GROUPS):
            sl = slice(g * SGU_GCH, (g + 1) * SGU_GCH)
            mixed = _dot(w_ref[g], vb[:, sl]) + bias_ref[:, sl]
            o_ref[rows, sl] = (u[:, sl] * mixed).astype(o_ref.dtype)


def _sgu(z, ln_g, ln_b, w_s, b_s):
    c = SGU_CHUNK
    rows = SGU_CHUNKS_PER_STEP * c
    bias = jnp.repeat(b_s.T, SGU_GCH, axis=1)
    vec = pl.BlockSpec((1, SGU_DIM), lambda i: (0, 0))
    return pl.pallas_call(
        _sgu_kernel,
        grid=(T // rows,),
        in_specs=[pl.BlockSpec((rows, SGU_DIM), lambda i: (i, OFF_SGU // SGU_DIM)),
                  pl.BlockSpec((rows, SGU_DIM), lambda i: (i, OFF_SGU // SGU_DIM + 1)),
                  vec, vec,
                  pl.BlockSpec((SGU_GROUPS, c, c), lambda i: (0, 0, 0)),
                  pl.BlockSpec((c, SGU_DIM), lambda i: (0, 0))],
        out_specs=pl.BlockSpec((rows, SGU_DIM), lambda i: (i, 0)),
        out_shape=jax.ShapeDtypeStruct((T, SGU_DIM), BF16),
        compiler_params=_cparams(("parallel",)),
        name="sgu",
    )(z, z, ln_g.reshape(1, SGU_DIM), ln_b.reshape(1, SGU_DIM), w_s.astype(BF16), bias)


SMALL_LEVELS = tuple(m for m in (1, 2, 4) if m < SUBLANE)
BIG_LEVELS = tuple(m for m in (8, 16, 32, 64) if m < SCAN_C)


def _scan_consts():
    c = SCAN_C
    t = np.arange(c)[:, None]
    s = np.arange(c)[None, :]
    x = t ^ s
    tri = np.stack([s <= t, s >= t]).astype(np.float32)
    pair = [np.stack([x == 0, x == 0])]
    for m in SMALL_LEVELS:
        lvl = (x >= m) & (x < 2 * m)
        pair.append(np.stack([lvl & (t > s), lvl & (t < s)]))
    pair = np.stack(pair).astype(np.float32)
    rowq = np.stack([np.broadcast_to(((np.arange(c) & m) != 0)[:, None], (c, LANE)) for m in SMALL_LEVELS])
    hb = np.arange(c // 2)
    blk = np.stack([(hb[:, None] // m) == (hb[None, :] // m) for m in BIG_LEVELS])
    return (jnp.asarray(tri, BF16), jnp.asarray(pair, F32), jnp.asarray(rowq.astype(np.float32), F32),
            jnp.asarray(blk.astype(np.float32), F32))


def _split2(x):
    hi = x.astype(BF16)
    lo = (x - hi.astype(F32)).astype(BF16)
    return hi, lo


def _halves(a, m, second):
    off = m if second else 0
    return jnp.concatenate([a[j + off:j + off + m] for j in range(0, a.shape[0], 2 * m)], axis=0)


def _chunk_scans(probs, consts):
    tri_ref, pair_ref, rowq_ref, blk_ref = consts
    c, dk = probs[0][0].shape
    dirs = [1 if p[5] else 0 for p in probs]
    vbs = [p[2].astype(BF16) for p in probs]
    diag = [pair_ref[0, d] * _dot_nt(p[0].astype(BF16), p[1].astype(BF16)) for p, d in zip(probs, dirs)]
    bs = []
    for p, d in zip(probs, dirs):
        hi, lo = _split2(p[3])
        tri = tri_ref[d]
        bs.append(_dot(tri, hi) + _dot(tri, lo))
    inter = [_dot_nt((p[0] * jnp.exp2(b)).astype(BF16), p[4].astype(BF16)) for p, b in zip(probs, bs)]

    big = [[] for _ in probs]
    for li, m in enumerate(BIG_LEVELS):
        for pi, ((q, k, _, _, _, reverse), b) in enumerate(zip(probs, bs)):
            q_second = not reverse
            refs = [b[j + m:j + m + 1] if reverse else b[j + m - 1:j + m] for j in range(0, c, 2 * m)]
            ref = jnp.concatenate([jnp.broadcast_to(r, (m, dk)) for r in refs], axis=0)
            qf = _halves(q, m, q_second) * jnp.exp2(_halves(b, m, q_second) - ref)
            kf = _halves(k, m, not q_second) * jnp.exp2(ref - _halves(b, m, not q_second))
            s_l = _dot_nt(qf.astype(BF16), kf.astype(BF16))
            if 2 * m < c:
                s_l = s_l * blk_ref[li]
            big[pi].append(s_l.astype(BF16))

    scores = diag
    es = list(bs)
    for li, m in enumerate(SMALL_LEVELS):
        for pi, ((q, k, _, _, _, reverse), b) in enumerate(zip(probs, bs)):
            is_q = (rowq_ref[li] < 0.5) if reverse else (rowq_ref[li] > 0.5)
            sh_q, sh_e = (c - m, m) if reverse else (m, c - m)
            fac = jnp.exp2(jnp.where(is_q, b - pltpu.roll(es[pi], sh_q, 0), es[pi] - b))
            u = (jnp.where(is_q, q, k) * fac).astype(BF16)
            scores[pi] = scores[pi] + pair_ref[1 + li, dirs[pi]] * _dot_nt(u, u)
            if li + 1 < len(SMALL_LEVELS):
                es[pi] = jnp.where(is_q, es[pi], pltpu.roll(es[pi], sh_e, 0))

    outs = []
    for pi, ((q, k, _, _, st, reverse), b) in enumerate(zip(probs, bs)):
        o = inter[pi] + _dot(scores[pi].astype(BF16), vbs[pi])
        parts = [o[g:g + SUBLANE] for g in range(0, c, SUBLANE)]
        q_second = not reverse
        for li, m in enumerate(BIG_LEVELS):
            oc = _dot(big[pi][li], _halves(probs[pi][2], m, not q_second).astype(BF16))
            for jj, j in enumerate(range(0, c, 2 * m)):
                for g in range(0, m, SUBLANE):
                    dst = (j + (m if q_second else 0) + g) // SUBLANE
                    parts[dst] = parts[dst] + oc[jj * m + g:jj * m + g + SUBLANE]
        bl = b[0:1, :] if reverse else b[c - 1:c, :]
        kd = (k * jnp.exp2(bl - b)).astype(BF16)
        st_new = jnp.exp2(bl) * st + _dot_tn(vbs[pi], kd)
        outs.append((jnp.concatenate(parts, axis=0), st_new))
    return outs


def _scan_kernel(*refs, features, n_in, seq, nseg, hpb, dv, with_s0, emit_state):
    consts = refs[:4]
    pos = 4
    dir_refs = []
    for _ in range(1 if nseg == 1 else 2):
        dir_refs.append((refs[pos:pos + n_in], refs[pos + n_in]))
        pos += n_in + 1
    dir_refs = dir_refs * 2 if nseg == 1 else dir_refs
    gain_ref = refs[pos]
    pos += 1
    s0_ref = None
    if with_s0:
        s0_ref = refs[pos]
        pos += 1
    o_ref = refs[pos]
    pos += 1
    sf_ref = None
    if emit_state:
        sf_ref = refs[pos]
        pos += 1
    o_scr, st_scr = refs[pos:pos + 2]
    c = SCAN_C
    n = seq // c
    cps = n // nseg
    g = pl.program_id(2) if nseg > 1 else 0
    work = [(hh, reverse) for hh in range(hpb) for reverse in (False, True)]

    def init():
        st_scr[...] = s0_ref[...] if with_s0 else jnp.zeros(st_scr.shape, F32)

    def step(i, finalize):
        loc = {False: pl.multiple_of(i * c, c), True: pl.multiple_of((cps - 1 - i) * c, c)}
        pair = g * cps + i
        glob = {False: pl.multiple_of(pair * c, c), True: pl.multiple_of((n - 1 - pair) * c, c)}
        probs = [features(dir_refs[rev][0], loc[rev], hh, rev) + (st_scr[1 if rev else 0, hh], rev)
                 for hh, rev in work]
        for (hh, rev), (o, st) in zip(work, _chunk_scans(probs, consts)):
            st_scr[1 if rev else 0, hh] = st
            rows, cols = pl.ds(glob[rev], c), slice(hh * dv, (hh + 1) * dv)
            if finalize:
                gate = dir_refs[rev][1][pl.ds(loc[rev], c), cols]
                y = _rms(o + o_scr[rows, cols], gain_ref[...]) * _silu(gate)
                o_ref[rows, cols] = y.astype(o_ref.dtype)
            else:
                o_scr[rows, cols] = o

    def loop(lo, hi, finalize):
        lax.fori_loop(lo, hi, lambda i, carry: (step(i, finalize), carry)[1], 0)

    if nseg == 1:
        init()
        loop(0, n // 2, False)
        loop(n // 2, n, True)
    else:
        pl.when(g == 0)(init)
        pl.when(g < nseg // 2)(lambda: loop(0, cps, False))
        pl.when(g >= nseg // 2)(lambda: loop(0, cps, True))
    if emit_state:
        assert nseg == 1
        for d in range(2):
            for hh in range(hpb):
                sf_ref[d, hh] = st_scr[d, hh].T


def _hgrn_features(in_refs, r0, head, reverse):
    hq_ref, hf_ref, hb_ref, hi_ref, lb_ref = in_refs
    rows, cols = pl.ds(r0, SCAN_C), slice(head * HG_DK, (head + 1) * HG_DK)
    q = _silu(hq_ref[rows, cols]) * (HG_DK ** -0.5)
    lb = lb_ref[1 if reverse else 0][:, cols]
    f = lb + (1.0 - lb) * _sigmoid((hb_ref if reverse else hf_ref)[rows, cols])
    return q, 1.0 - f, hi_ref[rows, head * HG_DV:(head + 1) * HG_DV], jnp.log(f) * LOG2E


def _gla_features(in_refs, r0, head, reverse):
    gq_ref, gk_ref, gv_ref, low_ref, wgk_ref, bgk_ref = in_refs
    rows, cols = pl.ds(r0, SCAN_C), slice(head * GLA_DK, (head + 1) * GLA_DK)
    d = 1 if reverse else 0
    g = _dot(low_ref[rows, :].astype(BF16), wgk_ref[d][:, cols]) + bgk_ref[d][:, cols]
    la = (jnp.minimum(g, 0.0) - jnp.log(1.0 + jnp.exp(-jnp.abs(g)))) * (LOG2E / GLA_NORMALIZER)
    return (gq_ref[rows, cols] * (GLA_DK ** -0.5), gk_ref[rows, cols],
            gv_ref[rows, head * GLA_DV:(head + 1) * GLA_DV], la)


def _seq_geom(latent):
    seq = DEC_SEQ if latent else SEQ
    return seq, (DEC_BATCH if latent else BATCH), (TP // seq if latent else 0)


def _zspec(latent, off, width, per_head=True):
    seq, _, blk0 = _seq_geom(latent)
    assert off % width == 0
    if per_head:
        return pl.BlockSpec((seq, width), lambda b, h, *_: (blk0 + b, off // width + h))
    return pl.BlockSpec((seq, width), lambda b, h, *_: (blk0 + b, off // width))


def _const_spec(a):
    return pl.BlockSpec(a.shape, lambda *_, nd=a.ndim: (0,) * nd)


def _scan_geom(latent, heads=2):
    return (min(4, heads // 2), 8 if latent else 1)


def _zseg_spec(latent, off, width, reverse, per_head=True):
    seq, _, blk0 = _seq_geom(latent)
    nseg = _scan_geom(latent)[1]
    assert off % width == 0
    col0 = off // width

    def index(b, h, g=0):
        seg = (nseg - 1 - g) if reverse else g
        return ((blk0 + b) * nseg + seg, col0 + (h if per_head else 0))

    return pl.BlockSpec((seq // nseg, width), index)


def _scan_call(name, features, feat_fn, z, gate_off, gain, heads, dk, dv,
               latent, s0t, prev_out, prev_state, layer):
    seq, nb, row_blk0 = _seq_geom(latent)
    hpb, nseg = _scan_geom(latent, heads)
    consts = _scan_consts()
    in_specs = [_const_spec(a) for a in consts]
    args = list(consts)
    for reverse in ((False,) if nseg == 1 else (False, True)):
        specs, fargs = feat_fn(reverse)
        n_in = len(specs)
        in_specs += list(specs) + [_zseg_spec(latent, gate_off, hpb * dv, reverse)]
        args += list(fargs) + [z]
    in_specs.append(pl.BlockSpec((1, dv), lambda *_: (0, 0)))
    args.append(gain.reshape(1, dv))
    if latent:
        in_specs.append(pl.BlockSpec((None, None, 2, hpb, dv, dk), lambda b, h, *_: (b, layer, 0, h, 0, 0)))
        args.append(s0t)
    out_specs = [pl.BlockSpec((seq, hpb * dv), lambda b, h, *_: (row_blk0 + b, h))]
    out_shape = [jax.ShapeDtypeStruct((T, heads * dv), BF16)]
    aliases = {}
    if prev_out is not None:
        in_specs.append(pl.BlockSpec(memory_space=pl.ANY))
        aliases[len(args)] = 0
        args.append(prev_out)
    emit_state = not latent
    if emit_state:
        out_specs.append(pl.BlockSpec((None, None, 2, hpb, dk, dv), lambda b, h, *_: (b, layer, 0, h, 0, 0)))
        out_shape.append(jax.ShapeDtypeStruct((BATCH, DEPTH, 2, heads, dk, dv), F32))
        if prev_state is not None:
            in_specs.append(pl.BlockSpec(memory_space=pl.ANY))
            aliases[len(args)] = 1
            args.append(prev_state)
    n_alias = len(aliases)

    def kern(*refs):
        n_inputs = len(args) - n_alias
        keep = refs[:n_inputs] + refs[n_inputs + n_alias:]
        _scan_kernel(*keep, features=features, n_in=n_in, seq=seq, nseg=nseg, hpb=hpb, dv=dv, with_s0=latent,
                     emit_state=emit_state)

    grid = (nb, heads // hpb) + ((nseg,) if nseg > 1 else ())
    sem = ("parallel", "parallel") + (("arbitrary",) if nseg > 1 else ())
    res = pl.pallas_call(
        kern,
        grid=grid,
        in_specs=in_specs,
        out_specs=out_specs,
        out_shape=out_shape,
        input_output_aliases=aliases,
        scratch_shapes=[pltpu.VMEM((seq, hpb * dv), F32), pltpu.VMEM((2, hpb, dv, dk), F32)],
        compiler_params=_cparams(sem),
        name=name,
    )(*args)
    return (res[0], res[1]) if emit_state else (res[0], None)


def _hgrn(z, lb, gain, s0t, prev_state, layer):
    lb3 = lb.reshape(2, 1, HG_K)
    o = state = None
    for latent in (False, True):
        hpb = _scan_geom(latent, HG_HEADS)[0]

        def feat_fn(reverse, latent=latent, hpb=hpb):
            zs = functools.partial(_zseg_spec, latent, reverse=reverse)
            return ([zs(OFF_HQ, hpb * HG_DK), zs(OFF_HFF, hpb * HG_DK), zs(OFF_HFB, hpb * HG_DK),
                     zs(OFF_HI, hpb * HG_DV), pl.BlockSpec((2, 1, hpb * HG_DK), lambda b, h, *_: (0, 0, h))],
                    [z, z, z, z, lb3])

        o, st = _scan_call("hgrn_latent" if latent else "hgrn_prompt", _hgrn_features, feat_fn,
                           z, OFF_HG, gain, HG_HEADS, HG_DK, HG_DV, latent, s0t, o, prev_state, layer)
        state = st if st is not None else state
    return o, state


def _gla(z, w_gk, b_gk, gain, s0t, prev_state, layer):
    wpad = jnp.zeros((2, LANE, GLA_HEADS * GLA_DK), F32)
    wpad = wpad.at[0, 0:GLA_RANK].set(w_gk[0]).at[1, GLA_RANK:2 * GLA_RANK].set(w_gk[1]).astype(BF16)
    b3 = b_gk.reshape(2, 1, GLA_HEADS * GLA_DK)
    o = state = None
    for latent in (False, True):
        hpb = _scan_geom(latent, GLA_HEADS)[0]

        def feat_fn(reverse, latent=latent, hpb=hpb):
            zs = functools.partial(_zseg_spec, latent, reverse=reverse)
            return ([zs(OFF_GQ, hpb * GLA_DK), zs(OFF_GK, hpb * GLA_DK), zs(OFF_GV, hpb * GLA_DV),
                     zs(OFF_GLF, LANE, per_head=False),
                     pl.BlockSpec((2, LANE, hpb * GLA_DK), lambda b, h, *_: (0, 0, h)),
                     pl.BlockSpec((2, 1, hpb * GLA_DK), lambda b, h, *_: (0, 0, h))],
                    [z, z, z, z, wpad, b3])

        o, st = _scan_call("gla_latent" if latent else "gla_prompt", _gla_features, feat_fn,
                           z, OFF_GG, gain, GLA_HEADS, GLA_DK, GLA_DV, latent, s0t, o, prev_state, layer)
        state = st if st is not None else state
    return o, state


GQA = N_HEADS // N_KV_HEADS


def _ctx_attn_kernel(q_ref, k_ref, v_ref, sink_ref, o_ref):
    hd = HEAD_DIM
    k = k_ref[...].astype(BF16)
    v = v_ref[...].astype(BF16)
    heads = range(GQA)
    ss = [_dot_nt((q_ref[:, g * hd:(g + 1) * hd] * (hd ** -0.5)).astype(BF16), k) for g in heads]
    ps, dens = [], []
    for g in heads:
        sink = sink_ref[g][:, 0:1]
        m = jnp.maximum(jnp.max(ss[g], axis=-1, keepdims=True), sink)
        p = jnp.exp(ss[g] - m)
        dens.append(jnp.sum(p, axis=-1, keepdims=True) + jnp.exp(sink - m))
        ps.append(p.astype(BF16))
    for g in heads:
        o_ref[:, g * hd:(g + 1) * hd] = (_dot(ps[g], v) / dens[g]).astype(o_ref.dtype)


def _ctx_attention(z, sink3):
    hd = HEAD_DIM
    return pl.pallas_call(
        _ctx_attn_kernel,
        grid=(BATCH, N_KV_HEADS),
        in_specs=[_zspec(False, OFF_AQ, GQA * hd),
                  pl.BlockSpec((SEQ, hd), lambda b, kh: (b, OFF_AK // hd + kh)),
                  pl.BlockSpec((SEQ, hd), lambda b, kh: (b, OFF_AV // hd + kh)),
                  pl.BlockSpec((GQA, 1, LANE), lambda b, kh: (kh, 0, 0))],
        out_specs=pl.BlockSpec((SEQ, GQA * hd), lambda b, kh: (b, kh)),
        out_shape=jax.ShapeDtypeStruct((T, N_HEADS * hd), BF16),
        compiler_params=_cparams(("parallel", "parallel")),
        name="ctx_attention",
    )(z, z, z, sink3)


def _rope(x, cos_f, sin_f):
    return x * cos_f + pltpu.roll(x, HEAD_DIM // 2, 1) * sin_f


def _lat_attn_kernel(q_ref, k_ref, v_ref, ck_ref, cv_ref, cos_ref, sin_ref, sink_ref, o_ref, kr_scr, vr_scr):
    blk, hd, seq = ATT_BLOCK, HEAD_DIM, DEC_SEQ
    nb = seq // blk

    zeros = jnp.zeros((blk, hd), BF16)
    for scr in (kr_scr, vr_scr):
        scr[0:blk, :] = zeros
        scr[blk + seq:2 * blk + seq, :] = zeros

    def fill(n, carry):
        r0 = pl.multiple_of(n * blk, blk)
        rows = pl.ds(r0, blk)
        kr_scr[pl.ds(r0 + blk, blk), :] = _rope(k_ref[rows, :], cos_ref[rows, :], sin_ref[rows, :]).astype(BF16)
        vr_scr[pl.ds(r0 + blk, blk), :] = v_ref[rows, :].astype(BF16)
        return carry

    lax.fori_loop(0, nb, fill, 0)

    ck = ck_ref[...].astype(BF16)
    cv = cv_ref[...].astype(BF16)
    sink = jnp.concatenate([jnp.broadcast_to(sink_ref[g][:, 0:1], (blk, 1)) for g in range(GQA)], axis=0)
    qi = lax.broadcasted_iota(jnp.int32, (GQA * blk, 3 * blk), 0) & (blk - 1)
    kj = lax.broadcasted_iota(jnp.int32, (GQA * blk, 3 * blk), 1)
    window_bias = jnp.where(jnp.abs(kj - qi - blk) <= WINDOW, 0.0, -jnp.inf)

    per_iter = 2

    def qblocks(it, carry):
        ns = [it * per_iter + u for u in range(per_iter)]
        r0s = [pl.multiple_of(n * blk, blk) for n in ns]
        qs = []
        for r0 in r0s:
            rows = pl.ds(r0, blk)
            cos_f, sin_f = cos_ref[rows, :], sin_ref[rows, :]
            qs.append(jnp.concatenate(
                [(_rope(q_ref[rows, g * hd:(g + 1) * hd], cos_f, sin_f) * (hd ** -0.5)).astype(BF16)
                 for g in range(GQA)], axis=0))
        bands = [pl.ds(r0, 3 * blk) for r0 in r0s]
        s_locs = [_dot_nt(q, kr_scr[band, :]) for q, band in zip(qs, bands)]
        s_ctxs = [_dot_nt(q, ck) for q in qs]
        ps = []
        for n, s_loc, s_ctx in zip(ns, s_locs, s_ctxs):
            key_pos = kj + (n - 1) * blk
            s_loc = s_loc + window_bias
            s_loc = jnp.where(key_pos >= 0, s_loc, -jnp.inf)
            s_loc = jnp.where(key_pos < seq, s_loc, -jnp.inf)
            m = jnp.maximum(jnp.maximum(jnp.max(s_loc, axis=-1, keepdims=True),
                                        jnp.max(s_ctx, axis=-1, keepdims=True)), sink)
            p_loc = jnp.exp(s_loc - m)
            p_ctx = jnp.exp(s_ctx - m)
            den = (jnp.sum(p_loc, axis=-1, keepdims=True) + jnp.sum(p_ctx, axis=-1, keepdims=True)
                   + jnp.exp(sink - m))
            ps.append((p_ctx.astype(BF16), p_loc.astype(BF16), den))
        for r0, band, (p_ctx, p_loc, den) in zip(r0s, bands, ps):
            o = (_dot(p_ctx, cv) + _dot(p_loc, vr_scr[band, :])) / den
            for g in range(GQA):
                o_ref[pl.ds(r0, blk), g * hd:(g + 1) * hd] = o[g * blk:(g + 1) * blk].astype(o_ref.dtype)
        return carry

    lax.fori_loop(0, nb // per_iter, qblocks, 0)


def _rope_tables():
    n = jnp.arange(DEC_SEQ)
    row = (n // GRID_W).astype(F32)
    col = (n % GRID_W).astype(F32)
    n_freq = HEAD_DIM // 4
    inv = ROPE_THETA ** (-jnp.arange(n_freq, dtype=F32) / n_freq)
    ang = jnp.concatenate([row[:, None] * inv, col[:, None] * inv], axis=-1)
    cos, sin = jnp.cos(ang), jnp.sin(ang)
    return jnp.concatenate([cos, cos], axis=-1), jnp.concatenate([-sin, sin], axis=-1)


def _lat_attention(z, ck, cv, sink3, prev_out):
    hd = HEAD_DIM
    seq, _, blk0 = _seq_geom(True)
    cos_f, sin_f = _rope_tables()
    tab = pl.BlockSpec((seq, hd), lambda b, kh: (0, 0))
    cache = pl.BlockSpec((None, PAST_LEN, hd), lambda b, kh: (b, 0, kh))
    return pl.pallas_call(
        lambda *refs: _lat_attn_kernel(*refs[:8], *refs[9:]),
        grid=(DEC_BATCH, N_KV_HEADS),
        in_specs=[_zspec(True, OFF_AQ, GQA * hd),
                  pl.BlockSpec((seq, hd), lambda b, kh: (blk0 + b, OFF_AK // hd + kh)),
                  pl.BlockSpec((seq, hd), lambda b, kh: (blk0 + b, OFF_AV // hd + kh)),
                  cache, cache, tab, tab,
                  pl.BlockSpec((GQA, 1, LANE), lambda b, kh: (kh, 0, 0)),
                  pl.BlockSpec(memory_space=pl.ANY)],
        out_specs=pl.BlockSpec((seq, GQA * hd), lambda b, kh: (blk0 + b, kh)),
        out_shape=jax.ShapeDtypeStruct((T, N_HEADS * hd), BF16),
        input_output_aliases={8: 0},
        scratch_shapes=[pltpu.VMEM((seq + 2 * ATT_BLOCK, hd), BF16), pltpu.VMEM((seq + 2 * ATT_BLOCK, hd), BF16)],
        compiler_params=_cparams(("parallel", "parallel")),
        name="lat_attention",
    )(z, z, z, ck, cv, cos_f, sin_f, sink3, prev_out)


def kernel(x_prompt, x_sample, cache_k, cache_v, state_hgrn, state_gla, c, c_ctx, w_ada, b_ada, norm_g, w_in,
           w_out, sgu_ln_g, sgu_ln_b, sgu_w, sgu_b, hgrn_lb_logits, hgrn_norm_g, attn_sink, gla_w_gk, gla_b_gk,
           gla_norm_g, ffn_w_up, ffn_conv_w, ffn_conv_b, ffn_w_down):
    assert TP % DEC_SEQ == 0 and DEC_BATCH + 1 <= MOD_ROWS
    lb_all = jnp.cumsum(jax.nn.softmax(hgrn_lb_logits.astype(F32), axis=0), axis=0)
    lb_all = lb_all - lb_all[0:1]

    cvec = jnp.zeros((MOD_ROWS, D_MODEL), F32).at[0].set(c_ctx).at[1:1 + DEC_BATCH].set(c)
    mod = _ada_mod(cvec, w_ada, b_ada).reshape(DEPTH, MOD_ROWS, 1, 6 * D_MODEL)
    SH1, SC1, G1, SH2, SC2, G2 = range(6)

    x = [x_prompt.reshape(TP, D_MODEL), x_sample.reshape(TL, D_MODEL)]
    hg_s0t = jnp.swapaxes(state_hgrn, -1, -2)
    gl_s0t = jnp.swapaxes(state_gla, -1, -2)
    ck_all = cache_k.reshape(DEC_BATCH, DEPTH, PAST_LEN, N_KV_HEADS * HEAD_DIM)
    cv_all = cache_v.reshape(DEC_BATCH, DEPTH, PAST_LEN, N_KV_HEADS * HEAD_DIM)

    h = _norm_mod(x, norm_g[0, 0], mod, 0, SC1, SH1)
    ks_new, vs_new = [], []
    hg_state = gl_state = None
    w_in_b, w_up_b = w_in.astype(BF16), ffn_w_up
    w_out_b = w_out.astype(BF16)
    for l in range(DEPTH):
        sink3 = jnp.broadcast_to(attn_sink[l].astype(F32)[:, None, None], (N_HEADS, 1, LANE))

        z = _matmul(h, w_in_b, l, Z_COLS, F32, Z_TILE_N, "in_proj")
        o_sgu = _sgu(z, sgu_ln_g[l], sgu_ln_b[l], sgu_w[l], sgu_b[l])
        o_hg, hg_state = _hgrn(z, lb_all[l], hgrn_norm_g[l], hg_s0t, hg_state, l)
        o_att = _ctx_attention(z, sink3)
        o_att = _lat_attention(z, ck_all[:, l], cv_all[:, l], sink3, o_att)
        o_gl, gl_state = _gla(z, gla_w_gk[l], gla_b_gk[l], gla_norm_g[l], gl_s0t, gl_state, l)
        m = _out_proj([o_sgu, o_hg, o_att, o_gl], w_out_b, l)
        x, h = _resid(x, m, norm_g[l, 1], mod, l, G1, nxt=(norm_g[l, 2], l, SC2, SH2))
        ks_new.append(z[:TP, OFF_AK:OFF_AK + N_KV_HEADS * HEAD_DIM].reshape(BATCH, SEQ, N_KV_HEADS, HEAD_DIM))
        vs_new.append(z[:TP, OFF_AV:OFF_AV + N_KV_HEADS * HEAD_DIM].reshape(BATCH, SEQ, N_KV_HEADS, HEAD_DIM))

        act, w_down_b = _ffn_up(h, _conv_halo(h), w_up_b, ffn_conv_w, ffn_conv_b, ffn_w_down, l)
        f = _ffn_down(act, w_down_b, 0)
        if l + 1 < DEPTH:
            x, h = _resid(x, f, norm_g[l, 3], mod, l, G2, nxt=(norm_g[l + 1, 0], l + 1, SC1, SH1))
        else:
            x, h = _resid(x, f, norm_g[l, 3], mod, l, G2, split_out=True)

    y_prompt = x[0].reshape(BATCH, SEQ, D_MODEL)
    y_sample = x[1].reshape(DEC_BATCH, DEC_SEQ, D_MODEL)
    return (y_prompt, y_sample, jnp.stack(ks_new, axis=1), jnp.stack(vs_new, axis=1), hg_state, gl_state)
```

```python
import functools

import numpy as np
import jax
import jax.numpy as jnp
from jax import lax
from jax.experimental import pallas as pl
from jax.experimental.pallas import tpu as pltpu

D_MODEL = 4096
BATCH = 32
SEQ = 256
DEPTH = 2
DEC_BATCH = 2
DEC_SEQ = 4096
PAST_LEN = 512
GRID_W = 64
GROUP_W = D_MODEL // 4
SGU_CHUNK = 128
SGU_GROUPS = 4
SGU_DIM = GROUP_W
SGU_GCH = SGU_DIM // SGU_GROUPS
HG_HEADS = 8
HG_DK = 128
HG_DV = GROUP_W // HG_HEADS
HG_K = HG_HEADS * HG_DK
N_HEADS = 8
N_KV_HEADS = 2
HEAD_DIM = GROUP_W // N_HEADS
WINDOW = 128
ATT_BLOCK = 128
ROPE_THETA = 10000.0
GLA_HEADS = 4
GLA_DK = 128
GLA_DV = GROUP_W // GLA_HEADS
GLA_RANK = 16
GLA_NORMALIZER = 16.0
D_FF = 11008
CONV_W = 3
EPS = 1e-6

F32 = jnp.float32
BF16 = jnp.bfloat16
LOG2E = 1.4426950408889634

IN_SIZES = (2 * SGU_DIM,
            HG_K, HG_K, HG_K, HG_HEADS * HG_DV, HG_HEADS * HG_DV,
            N_HEADS * HEAD_DIM, N_KV_HEADS * HEAD_DIM, N_KV_HEADS * HEAD_DIM,
            GLA_HEADS * GLA_DK, GLA_HEADS * GLA_DK, GLA_HEADS * GLA_DV, GLA_HEADS * GLA_DV,
            GLA_RANK, GLA_RANK)
D_IN = sum(IN_SIZES)
_OFF = [0] + [int(c) for c in np.cumsum(IN_SIZES)]
(OFF_SGU, OFF_HQ, OFF_HFF, OFF_HFB, OFF_HI, OFF_HG, OFF_AQ, OFF_AK, OFF_AV,
 OFF_GQ, OFF_GK, OFF_GV, OFF_GG, OFF_GLF, OFF_GLB) = _OFF[:-1]

LANE = 128
SUBLANE = 8
VMEM_LIMIT = 56 * 1024 * 1024

Z_TILE_N = 1024
Z_COLS = -(-D_IN // Z_TILE_N) * Z_TILE_N
SCAN_C = 128
MOD_ROWS = 16

TP = BATCH * SEQ
TL = DEC_BATCH * DEC_SEQ
T = TP + TL
PIECES = ((0, TP), (TP, TL))


def _cparams(sem):
    return pltpu.CompilerParams(dimension_semantics=sem, vmem_limit_bytes=VMEM_LIMIT)


def _row_group(row0):
    return jnp.where(row0 < TP, 0, 1 + (row0 - TP) // DEC_SEQ)


def _dot(a, b):
    return jnp.dot(a, b, preferred_element_type=F32)


def _dot_nt(a, b):
    return lax.dot_general(a, b, (((1,), (1,)), ((), ())), preferred_element_type=F32)


def _dot_tn(a, b):
    return lax.dot_general(a, b, (((0,), (0,)), ((), ())), preferred_element_type=F32)


def _sigmoid(x):
    return 1.0 / (1.0 + jnp.exp(-x))


def _silu(x):
    return x * _sigmoid(x)


def _gelu(x):
    return 0.5 * x * (1.0 + lax.erf(x * (2.0 ** -0.5)))


def _rms(x, g):
    return x * lax.rsqrt(jnp.mean(x * x, axis=-1, keepdims=True) + EPS) * g


def _ada_kernel(c_ref, w_ref, b_ref, o_ref):
    c = _silu(c_ref[...]).astype(BF16)
    o_ref[...] = _dot(c, w_ref[...].astype(BF16)) + b_ref[...]


def _ada_mod(cvec, w_ada, b_ada):
    tn = 512
    n = w_ada.shape[-1]
    return pl.pallas_call(
        _ada_kernel,
        grid=(DEPTH, n // tn),
        in_specs=[pl.BlockSpec((MOD_ROWS, D_MODEL), lambda l, j: (0, 0)),
                  pl.BlockSpec((None, D_MODEL, tn), lambda l, j: (l, 0, j)),
                  pl.BlockSpec((None, 1, tn), lambda l, j: (l, 0, j))],
        out_specs=pl.BlockSpec((None, MOD_ROWS, tn), lambda l, j: (l, 0, j)),
        out_shape=jax.ShapeDtypeStruct((DEPTH, MOD_ROWS, n), F32),
        compiler_params=_cparams(("parallel", "parallel")),
        name="ada_mod",
    )(cvec, w_ada, b_ada.reshape(DEPTH, 1, n))


def _mod_spec(layer, chunk, tr, blk0=0):
    return pl.BlockSpec((None, None, 1, D_MODEL),
                        lambda i, *_: (layer, _row_group((blk0 + i) * tr), 0, chunk))


NORM_ROWS = 256
_VEC_SPEC = pl.BlockSpec((1, D_MODEL), lambda i: (0, 0))


def _row_spec(blk0=0):
    return pl.BlockSpec((NORM_ROWS, D_MODEL), lambda i: (blk0 + i, 0))


def _norm_mod_kernel(x_ref, g_ref, sc_ref, sh_ref, h_ref):
    y = _rms(x_ref[...], g_ref[...])
    h_ref[...] = (y * (1.0 + sc_ref[...]) + sh_ref[...]).astype(h_ref.dtype)


def _norm_mod(x_parts, g, mod, layer, sc_chunk, sh_chunk):
    tr = NORM_ROWS
    h = None
    for x, (row0, rows) in zip(x_parts, PIECES):
        blk0 = row0 // tr
        in_specs = [_row_spec(), _VEC_SPEC, _mod_spec(layer, sc_chunk, tr, blk0), _mod_spec(layer, sh_chunk, tr, blk0)]
        args = [x, g.reshape(1, D_MODEL), mod, mod]
        aliases = {}
        if h is not None:
            in_specs.append(pl.BlockSpec(memory_space=pl.ANY))
            aliases[len(args)] = 0
            args.append(h)
        h = pl.pallas_call(
            lambda x_ref, g_ref, sc_ref, sh_ref, *rest: _norm_mod_kernel(x_ref, g_ref, sc_ref, sh_ref, rest[-1]),
            grid=(rows // tr,),
            in_specs=in_specs,
            out_specs=_row_spec(blk0),
            out_shape=jax.ShapeDtypeStruct((T, D_MODEL), BF16),
            input_output_aliases=aliases,
            compiler_params=_cparams(("parallel",)),
            name="norm_mod",
        )(*args)
    return h


def _resid_kernel(x_ref, m_ref, ga_ref, gate_ref, *rest, emit_h):
    x = x_ref[...] + gate_ref[...] * _rms(m_ref[...].astype(F32), ga_ref[...])
    if emit_h:
        gb_ref, sc_ref, sh_ref, xo_ref, h_ref = rest
        xo_ref[...] = x
        y = _rms(x, gb_ref[...])
        h_ref[...] = (y * (1.0 + sc_ref[...]) + sh_ref[...]).astype(h_ref.dtype)
    else:
        (xo_ref,) = rest
        xo_ref[...] = x


def _resid(x, m, ga, mod, layer, gate_chunk, nxt=None, split_out=False):
    tr = NORM_ROWS
    x_parts = x if isinstance(x, (list, tuple)) else None
    pieces = PIECES if (x_parts is not None or split_out) else ((0, T),)
    emit_h = nxt is not None
    assert not (emit_h and split_out)
    x_new, h, outs = None, None, []
    for p, (row0, rows) in enumerate(pieces):
        blk0 = row0 // tr
        in_specs = [_row_spec(0 if x_parts is not None else blk0), _row_spec(blk0), _VEC_SPEC,
                    _mod_spec(layer, gate_chunk, tr, blk0)]
        args = [x_parts[p] if x_parts is not None else x, m, ga.reshape(1, D_MODEL), mod]
        if emit_h:
            gb, layer_b, sc_chunk, sh_chunk = nxt
            in_specs += [_VEC_SPEC, _mod_spec(layer_b, sc_chunk, tr, blk0), _mod_spec(layer_b, sh_chunk, tr, blk0)]
            args += [gb.reshape(1, D_MODEL), mod, mod]
        n_in = len(args)
        if split_out:
            out_specs = [_row_spec()]
            out_shape = [jax.ShapeDtypeStruct((rows, D_MODEL), F32)]
        else:
            out_specs = [_row_spec(blk0)]
            out_shape = [jax.ShapeDtypeStruct((T, D_MODEL), F32)]
        if emit_h:
            out_specs.append(_row_spec(blk0))
            out_shape.append(jax.ShapeDtypeStruct((T, D_MODEL), BF16))
        aliases = {}
        if x_new is not None and not split_out:
            for k, prev in enumerate((x_new, h) if emit_h else (x_new,)):
                in_specs.append(pl.BlockSpec(memory_space=pl.ANY))
                aliases[len(args)] = k
                args.append(prev)
        n_alias = len(aliases)

        def kern(*refs, n_in=n_in, n_alias=n_alias):
            _resid_kernel(*refs[:n_in], *refs[n_in + n_alias:], emit_h=emit_h)

        res = pl.pallas_call(
            kern,
            grid=(rows // tr,),
            in_specs=in_specs,
            out_specs=out_specs,
            out_shape=out_shape,
            input_output_aliases=aliases,
            compiler_params=_cparams(("parallel",)),
            name="resid_norm",
        )(*args)
        x_new = res[0]
        h = res[1] if emit_h else None
        outs.append(res[0])
    return (outs if split_out else x_new), h


MM_TILE_M = 1024


def _mm_kernel(a_ref, b_ref, o_ref, *, n_valid):
    acc = _dot(a_ref[...], b_ref[...])
    tn = o_ref.shape[1]
    if n_valid % tn:
        col = lax.broadcasted_iota(jnp.int32, acc.shape, 1)
        acc = jnp.where(col < n_valid - pl.program_id(1) * tn, acc, 0.0)
    o_ref[...] = acc.astype(o_ref.dtype)


def _matmul(a, w, layer, n_out, out_dtype, tn, name):
    m, k = a.shape
    tm = MM_TILE_M
    return pl.pallas_call(
        functools.partial(_mm_kernel, n_valid=w.shape[2]),
        grid=(m // tm, n_out // tn),
        in_specs=[pl.BlockSpec((tm, k), lambda i, j: (i, 0)),
                  pl.BlockSpec((None, k, tn), lambda i, j: (layer, 0, j))],
        out_specs=pl.BlockSpec((tm, tn), lambda i, j: (i, j)),
        out_shape=jax.ShapeDtypeStruct((m, n_out), out_dtype),
        compiler_params=_cparams(("parallel", "parallel")),
        name=name,
    )(a, w)


def _mm4_kernel(a0_ref, a1_ref, a2_ref, a3_ref, b_ref, o_ref):
    acc = _dot(a0_ref[...], b_ref[0 * GROUP_W:1 * GROUP_W, :])
    acc += _dot(a1_ref[...], b_ref[1 * GROUP_W:2 * GROUP_W, :])
    acc += _dot(a2_ref[...], b_ref[2 * GROUP_W:3 * GROUP_W, :])
    acc += _dot(a3_ref[...], b_ref[3 * GROUP_W:4 * GROUP_W, :])
    o_ref[...] = acc.astype(o_ref.dtype)


def _mm4_cast_kernel(a0_ref, a1_ref, a2_ref, a3_ref, b_ref, wu_ref, o_ref, wub_ref):
    wub_ref[...] = wu_ref[...].astype(wub_ref.dtype)
    _mm4_kernel(a0_ref, a1_ref, a2_ref, a3_ref, b_ref, o_ref)


def _out_proj(parts, w, w_up, layer):
    tm, tn = MM_TILE_M, 512
    ni, nj = T // tm, D_MODEL // tn
    k_up, n_up = w_up.shape[1:]
    up_rows = k_up // (ni * nj)
    assert k_up % (ni * nj) == 0 and up_rows % 16 == 0
    a_spec = pl.BlockSpec((tm, GROUP_W), lambda i, j: (i, 0))
    return pl.pallas_call(
        _mm4_cast_kernel,
        grid=(ni, nj),
        in_specs=[a_spec, a_spec, a_spec, a_spec,
                  pl.BlockSpec((None, 4 * GROUP_W, tn), lambda i, j: (layer, 0, j)),
                  pl.BlockSpec((None, up_rows, n_up), lambda i, j: (layer, i * nj + j, 0))],
        out_specs=[pl.BlockSpec((tm, tn), lambda i, j: (i, j)),
                   pl.BlockSpec((None, up_rows, n_up), lambda i, j: (0, i * nj + j, 0))],
        out_shape=[jax.ShapeDtypeStruct((T, D_MODEL), BF16), jax.ShapeDtypeStruct((1, k_up, n_up), BF16)],
        compiler_params=_cparams(("parallel", "parallel")),
        name="out_proj",
    )(*parts, w, w_up)


def _ffn_down(a, w, layer):
    m, kk = a.shape
    n = w.shape[2]
    tm, tn = 512, 512
    return pl.pallas_call(
        functools.partial(_mm_kernel, n_valid=n),
        grid=(m // tm, n // tn),
        in_specs=[pl.BlockSpec((tm, kk), lambda i, j: (i, 0)),
                  pl.BlockSpec((None, kk, tn), lambda i, j: (layer, 0, j))],
        out_specs=pl.BlockSpec((tm, tn), lambda i, j: (i, j)),
        out_shape=jax.ShapeDtypeStruct((m, n), BF16),
        compiler_params=_cparams(("parallel", "parallel")),
        name="ffn_down",
    )(a, w)


FFN_TILE_M = 2048
FFN_TILE_N = 256
FFN_SUB_M = 256
HALO_ROWS = 16
assert D_FF % FFN_TILE_N == 0


def _ffn_up_kernel(h_ref, halo_ref, wg_ref, wv_ref, cw_ref, cb_ref, wd_ref, o_ref, wdb_ref):
    wdb_ref[...] = wd_ref[...].astype(wdb_ref.dtype)
    tm, tn = o_ref.shape
    sub = FFN_SUB_M
    ns = tm // sub
    wg, wv = wg_ref[...].astype(BF16), wv_ref[...].astype(BF16)
    row0 = pl.program_id(0) * tm
    seq_len = jnp.where(row0 < TP, SEQ, DEC_SEQ)
    r = lax.broadcasted_iota(jnp.int32, (sub, tn), 0)
    g0 = _dot(jnp.concatenate([halo_ref[...], h_ref[0:sub, :]], axis=0), wg)
    gh = g0[0:HALO_ROWS]
    gs = [g0[HALO_ROWS:]] + [_dot(h_ref[s * sub:(s + 1) * sub, :], wg) for s in range(1, ns)]
    for s in range(ns):
        val = _dot(h_ref[s * sub:(s + 1) * sub, :], wv)
        g = gs[s]
        before = gh[0:1, :] if s == 0 else gs[s - 1][sub - 1:sub, :]
        after = gh[1:2, :] if s == ns - 1 else gs[s + 1][0:1, :]
        pos = (row0 + s * sub + r) & (seq_len - 1)
        prev = jnp.where(r == 0, before, pltpu.roll(g, 1, 0))
        prev = jnp.where(pos == 0, 0.0, prev)
        nxt = jnp.where(r == sub - 1, after, pltpu.roll(g, sub - 1, 0))
        nxt = jnp.where(pos == seq_len - 1, 0.0, nxt)
        conv = prev * cw_ref[0:1, :] + g * cw_ref[1:2, :] + nxt * cw_ref[2:3, :] + cb_ref[...]
        o_ref[s * sub:(s + 1) * sub, :] = (_gelu(conv) * val).astype(o_ref.dtype)


def _ffn_up(h, halo, w_up, conv_w, conv_b, w_down, layer):
    tm, tn = FFN_TILE_M, FFN_TILE_N
    nj = D_FF // tn
    steps = (T // tm) * nj
    wd_rows = D_FF // steps
    assert D_FF % steps == 0 and wd_rows % 16 == 0
    wd_in = pl.BlockSpec((None, wd_rows, D_MODEL), lambda i, j: (layer, i * nj + j, 0))
    wd_out = pl.BlockSpec((None, wd_rows, D_MODEL), lambda i, j: (0, i * nj + j, 0))
    return pl.pallas_call(
        _ffn_up_kernel,
        grid=(T // tm, nj),
        in_specs=[pl.BlockSpec((tm, D_MODEL), lambda i, j: (i, 0), pipeline_mode=pl.Buffered(1)),
                  pl.BlockSpec((None, HALO_ROWS, D_MODEL), lambda i, j: (i, 0, 0)),
                  pl.BlockSpec((None, D_MODEL, tn), lambda i, j: (0, 0, j)),
                  pl.BlockSpec((None, D_MODEL, tn), lambda i, j: (0, 0, nj + j)),
                  pl.BlockSpec((None, CONV_W, tn), lambda i, j: (layer, 0, j)),
                  pl.BlockSpec((None, 1, tn), lambda i, j: (layer, 0, j)),
                  wd_in],
        out_specs=[pl.BlockSpec((tm, tn), lambda i, j: (i, j)), wd_out],
        out_shape=[jax.ShapeDtypeStruct((T, D_FF), BF16), jax.ShapeDtypeStruct((1, D_FF, D_MODEL), BF16)],
        compiler_params=_cparams(("parallel", "parallel")),
        name="ffn_up",
    )(h, halo, w_up, w_up, conv_w, conv_b.reshape(DEPTH, 1, D_FF), w_down)


def _conv_halo(h):
    tm = FFN_TILE_M
    nt = T // tm
    ht = h.reshape(nt, tm, D_MODEL)
    zero = jnp.zeros((1, D_MODEL), h.dtype)
    before = jnp.concatenate([zero, ht[:-1, tm - 1]], axis=0)
    after = jnp.concatenate([ht[1:, 0], zero], axis=0)
    pad = jnp.zeros((nt, HALO_ROWS - 2, D_MODEL), h.dtype)
    return jnp.concatenate([before[:, None], after[:, None], pad], axis=1)


SGU_CHUNKS_PER_STEP = 4


def _sgu_kernel(u_ref, v_ref, g_ref, b_ref, w_ref, bias_ref, o_ref):
    c = SGU_CHUNK
    for ci in range(SGU_CHUNKS_PER_STEP):
        rows = slice(ci * c, (ci + 1) * c)
        u = _gelu(u_ref[rows, :])
        v = _gelu(v_ref[rows, :])
        vc = v - jnp.mean(v, axis=-1, keepdims=True)
        vn = vc * lax.rsqrt(jnp.mean(vc * vc, axis=-1, keepdims=True) + EPS) * g_ref[...] + b_ref[...]
        vb = vn.astype(BF16)
        for g in range(SGU_GROUPS):
            sl = slice(g * SGU_GCH, (g + 1) * SGU_GCH)
            mixed = _dot(w_ref[g], vb[:, sl]) + bias_ref[:, sl]
            o_ref[rows, sl] = (u[:, sl] * mixed).astype(o_ref.dtype)


def _sgu(z, ln_g, ln_b, w_s, b_s):
    c = SGU_CHUNK
    rows = SGU_CHUNKS_PER_STEP * c
    bias = jnp.repeat(b_s.T, SGU_GCH, axis=1)
    vec = pl.BlockSpec((1, SGU_DIM), lambda i: (0, 0))
    return pl.pallas_call(
        _sgu_kernel,
        grid=(T // rows,),
        in_specs=[pl.BlockSpec((rows, SGU_DIM), lambda i: (i, OFF_SGU // SGU_DIM)),
                  pl.BlockSpec((rows, SGU_DIM), lambda i: (i, OFF_SGU // SGU_DIM + 1)),
                  vec, vec,
                  pl.BlockSpec((SGU_GROUPS, c, c), lambda i: (0, 0, 0)),
                  pl.BlockSpec((c, SGU_DIM), lambda i: (0, 0))],
        out_specs=pl.BlockSpec((rows, SGU_DIM), lambda i: (i, 0)),
        out_shape=jax.ShapeDtypeStruct((T, SGU_DIM), BF16),
        compiler_params=_cparams(("parallel",)),
        name="sgu",
    )(z, z, ln_g.reshape(1, SGU_DIM), ln_b.reshape(1, SGU_DIM), w_s.astype(BF16), bias)


SMALL_LEVELS = tuple(m for m in (1, 2, 4) if m < SUBLANE)
BIG_LEVELS = tuple(m for m in (8, 16, 32, 64) if m < SCAN_C)


def _scan_consts():
    c = SCAN_C
    t = np.arange(c)[:, None]
    s = np.arange(c)[None, :]
    x = t ^ s
    tri = np.stack([s <= t, s >= t]).astype(np.float32)
    pair = [np.stack([x == 0, x == 0])]
    for m in SMALL_LEVELS:
        lvl = (x >= m) & (x < 2 * m)
        pair.append(np.stack([lvl & (t > s), lvl & (t < s)]))
    pair = np.stack(pair).astype(np.float32)
    rowq = np.stack([np.broadcast_to(((np.arange(c) & m) != 0)[:, None], (c, LANE)) for m in SMALL_LEVELS])
    hb = np.arange(c // 2)
    blk = np.stack([(hb[:, None] // m) == (hb[None, :] // m) for m in BIG_LEVELS])
    return (jnp.asarray(tri, BF16), jnp.asarray(pair, F32), jnp.asarray(rowq.astype(np.float32), F32),
            jnp.asarray(blk.astype(np.float32), F32))


def _split2(x):
    hi = x.astype(BF16)
    lo = (x - hi.astype(F32)).astype(BF16)
    return hi, lo


def _halves(a, m, second):
    off = m if second else 0
    return jnp.concatenate([a[j + off:j + off + m] for j in range(0, a.shape[0], 2 * m)], axis=0)


def _chunk_scans(probs, consts):
    tri_ref, pair_ref, rowq_ref, blk_ref = consts
    c, dk = probs[0][0].shape
    dirs = [1 if p[5] else 0 for p in probs]
    vbs = [p[2].astype(BF16) for p in probs]
    diag = [pair_ref[0, d] * _dot_nt(p[0].astype(BF16), p[1].astype(BF16)) for p, d in zip(probs, dirs)]
    bs = []
    for p, d in zip(probs, dirs):
        hi, lo = _split2(p[3])
        tri = tri_ref[d]
        bs.append(_dot(tri, hi) + _dot(tri, lo))
    inter = [_dot_nt((p[0] * jnp.exp2(b)).astype(BF16), p[4].astype(BF16)) for p, b in zip(probs, bs)]

    big = [[] for _ in probs]
    for li, m in enumerate(BIG_LEVELS):
        for pi, ((q, k, _, _, _, reverse), b) in enumerate(zip(probs, bs)):
            q_second = not reverse
            refs = [b[j + m:j + m + 1] if reverse else b[j + m - 1:j + m] for j in range(0, c, 2 * m)]
            ref = jnp.concatenate([jnp.broadcast_to(r, (m, dk)) for r in refs], axis=0)
            qf = _halves(q, m, q_second) * jnp.exp2(_halves(b, m, q_second) - ref)
            kf = _halves(k, m, not q_second) * jnp.exp2(ref - _halves(b, m, not q_second))
            s_l = _dot_nt(qf.astype(BF16), kf.astype(BF16))
            if 2 * m < c:
                s_l = s_l * blk_ref[li]
            big[pi].append(s_l.astype(BF16))

    scores = diag
    es = list(bs)
    for li, m in enumerate(SMALL_LEVELS):
        for pi, ((q, k, _, _, _, reverse), b) in enumerate(zip(probs, bs)):
            is_q = (rowq_ref[li] < 0.5) if reverse else (rowq_ref[li] > 0.5)
            sh_q, sh_e = (c - m, m) if reverse else (m, c - m)
            fac = jnp.exp2(jnp.where(is_q, b - pltpu.roll(es[pi], sh_q, 0), es[pi] - b))
            u = (jnp.where(is_q, q, k) * fac).astype(BF16)
            scores[pi] = scores[pi] + pair_ref[1 + li, dirs[pi]] * _dot_nt(u, u)
            if li + 1 < len(SMALL_LEVELS):
                es[pi] = jnp.where(is_q, es[pi], pltpu.roll(es[pi], sh_e, 0))

    outs = []
    for pi, ((q, k, _, _, st, reverse), b) in enumerate(zip(probs, bs)):
        o = inter[pi] + _dot(scores[pi].astype(BF16), vbs[pi])
        parts = [o[g:g + SUBLANE] for g in range(0, c, SUBLANE)]
        q_second = not reverse
        for li, m in enumerate(BIG_LEVELS):
            oc = _dot(big[pi][li], _halves(probs[pi][2], m, not q_second).astype(BF16))
            for jj, j in enumerate(range(0, c, 2 * m)):
                for g in range(0, m, SUBLANE):
                    dst = (j + (m if q_second else 0) + g) // SUBLANE
                    parts[dst] = parts[dst] + oc[jj * m + g:jj * m + g + SUBLANE]
        bl = b[0:1, :] if reverse else b[c - 1:c, :]
        kd = (k * jnp.exp2(bl - b)).astype(BF16)
        st_new = jnp.exp2(bl) * st + _dot_tn(vbs[pi], kd)
        outs.append((jnp.concatenate(parts, axis=0), st_new))
    return outs


def _scan_kernel(*refs, features, n_in, seq, nseg, hpb, dv, with_s0, emit_state):
    consts = refs[:4]
    pos = 4
    dir_refs = []
    for _ in range(1 if nseg == 1 else 2):
        dir_refs.append((refs[pos:pos + n_in], refs[pos + n_in]))
        pos += n_in + 1
    dir_refs = dir_refs * 2 if nseg == 1 else dir_refs
    gain_ref = refs[pos]
    pos += 1
    s0_ref = None
    if with_s0:
        s0_ref = refs[pos]
        pos += 1
    o_ref = refs[pos]
    pos += 1
    sf_ref = None
    if emit_state:
        sf_ref = refs[pos]
        pos += 1
    o_scr, st_scr = refs[pos:pos + 2]
    c = SCAN_C
    n = seq // c
    cps = n // nseg
    g = pl.program_id(2) if nseg > 1 else 0
    work = [(hh, reverse) for hh in range(hpb) for reverse in (False, True)]

    def init():
        st_scr[...] = s0_ref[...] if with_s0 else jnp.zeros(st_scr.shape, F32)

    def step(i, finalize):
        loc = {False: pl.multiple_of(i * c, c), True: pl.multiple_of((cps - 1 - i) * c, c)}
        pair = g * cps + i
        glob = {False: pl.multiple_of(pair * c, c), True: pl.multiple_of((n - 1 - pair) * c, c)}
        probs = [features(dir_refs[rev][0], loc[rev], hh, rev) + (st_scr[1 if rev else 0, hh], rev)
                 for hh, rev in work]
        for (hh, rev), (o, st) in zip(work, _chunk_scans(probs, consts)):
            st_scr[1 if rev else 0, hh] = st
            rows, cols = pl.ds(glob[rev], c), slice(hh * dv, (hh + 1) * dv)
            if finalize:
                gate = dir_refs[rev][1][pl.ds(loc[rev], c), cols]
                y = _rms(o + o_scr[rows, cols], gain_ref[...]) * _silu(gate)
                o_ref[rows, cols] = y.astype(o_ref.dtype)
            else:
                o_scr[rows, cols] = o

    def loop(lo, hi, finalize):
        lax.fori_loop(lo, hi, lambda i, carry: (step(i, finalize), carry)[1], 0)

    if nseg == 1:
        init()
        loop(0, n // 2, False)
        loop(n // 2, n, True)
    else:
        pl.when(g == 0)(init)
        pl.when(g < nseg // 2)(lambda: loop(0, cps, False))
        pl.when(g >= nseg // 2)(lambda: loop(0, cps, True))
    if emit_state:
        assert nseg == 1
        for d in range(2):
            for hh in range(hpb):
                sf_ref[d, hh] = st_scr[d, hh].T


def _hgrn_features(in_refs, r0, head, reverse):
    hq_ref, hf_ref, hb_ref, hi_ref, lb_ref = in_refs
    rows, cols = pl.ds(r0, SCAN_C), slice(head * HG_DK, (head + 1) * HG_DK)
    q = _silu(hq_ref[rows, cols]) * (HG_DK ** -0.5)
    lb = lb_ref[1 if reverse else 0][:, cols]
    f = lb + (1.0 - lb) * _sigmoid((hb_ref if reverse else hf_ref)[rows, cols])
    return q, 1.0 - f, hi_ref[rows, head * HG_DV:(head + 1) * HG_DV], jnp.log(f) * LOG2E


def _gla_features(in_refs, r0, head, reverse):
    gq_ref, gk_ref, gv_ref, low_ref, wgk_ref, bgk_ref = in_refs
    rows, cols = pl.ds(r0, SCAN_C), slice(head * GLA_DK, (head + 1) * GLA_DK)
    d = 1 if reverse else 0
    g = _dot(low_ref[rows, :].astype(BF16), wgk_ref[d][:, cols]) + bgk_ref[d][:, cols]
    la = (jnp.minimum(g, 0.0) - jnp.log(1.0 + jnp.exp(-jnp.abs(g)))) * (LOG2E / GLA_NORMALIZER)
    return (gq_ref[rows, cols] * (GLA_DK ** -0.5), gk_ref[rows, cols],
            gv_ref[rows, head * GLA_DV:(head + 1) * GLA_DV], la)


def _seq_geom(latent):
    seq = DEC_SEQ if latent else SEQ
    return seq, (DEC_BATCH if latent else BATCH), (TP // seq if latent else 0)


def _zspec(latent, off, width, per_head=True):
    seq, _, blk0 = _seq_geom(latent)
    assert off % width == 0
    if per_head:
        return pl.BlockSpec((seq, width), lambda b, h, *_: (blk0 + b, off // width + h))
    return pl.BlockSpec((seq, width), lambda b, h, *_: (blk0 + b, off // width))


def _const_spec(a):
    return pl.BlockSpec(a.shape, lambda *_, nd=a.ndim: (0,) * nd)


def _scan_geom(latent, heads=2):
    return (min(4, heads // 2), 8 if latent else 1)


def _zseg_spec(latent, off, width, reverse, per_head=True):
    seq, _, blk0 = _seq_geom(latent)
    nseg = _scan_geom(latent)[1]
    assert off % width == 0
    col0 = off // width

    def index(b, h, g=0):
        seg = (nseg - 1 - g) if reverse else g
        return ((blk0 + b) * nseg + seg, col0 + (h if per_head else 0))

    return pl.BlockSpec((seq // nseg, width), index)


def _scan_call(name, features, feat_fn, z, gate_off, gain, heads, dk, dv,
               latent, s0t, prev_out, prev_state, layer):
    seq, nb, row_blk0 = _seq_geom(latent)
    hpb, nseg = _scan_geom(latent, heads)
    consts = _scan_consts()
    in_specs = [_const_spec(a) for a in consts]
    args = list(consts)
    for reverse in ((False,) if nseg == 1 else (False, True)):
        specs, fargs = feat_fn(reverse)
        n_in = len(specs)
        in_specs += list(specs) + [_zseg_spec(latent, gate_off, hpb * dv, reverse)]
        args += list(fargs) + [z]
    in_specs.append(pl.BlockSpec((1, dv), lambda *_: (0, 0)))
    args.append(gain.reshape(1, dv))
    if latent:
        in_specs.append(pl.BlockSpec((None, None, 2, hpb, dv, dk), lambda b, h, *_: (b, layer, 0, h, 0, 0)))
        args.append(s0t)
    out_specs = [pl.BlockSpec((seq, hpb * dv), lambda b, h, *_: (row_blk0 + b, h))]
    out_shape = [jax.ShapeDtypeStruct((T, heads * dv), BF16)]
    aliases = {}
    if prev_out is not None:
        in_specs.append(pl.BlockSpec(memory_space=pl.ANY))
        aliases[len(args)] = 0
        args.append(prev_out)
    emit_state = not latent
    if emit_state:
        out_specs.append(pl.BlockSpec((None, None, 2, hpb, dk, dv), lambda b, h, *_: (b, layer, 0, h, 0, 0)))
        out_shape.append(jax.ShapeDtypeStruct((BATCH, DEPTH, 2, heads, dk, dv), F32))
        if prev_state is not None:
            in_specs.append(pl.BlockSpec(memory_space=pl.ANY))
            aliases[len(args)] = 1
            args.append(prev_state)
    n_alias = len(aliases)

    def kern(*refs):
        n_inputs = len(args) - n_alias
        keep = refs[:n_inputs] + refs[n_inputs + n_alias:]
        _scan_kernel(*keep, features=features, n_in=n_in, seq=seq, nseg=nseg, hpb=hpb, dv=dv, with_s0=latent,
                     emit_state=emit_state)

    grid = (nb, heads // hpb) + ((nseg,) if nseg > 1 else ())
    sem = ("parallel", "parallel") + (("arbitrary",) if nseg > 1 else ())
    res = pl.pallas_call(
        kern,
        grid=grid,
        in_specs=in_specs,
        out_specs=out_specs,
        out_shape=out_shape,
        input_output_aliases=aliases,
        scratch_shapes=[pltpu.VMEM((seq, hpb * dv), F32), pltpu.VMEM((2, hpb, dv, dk), F32)],
        compiler_params=_cparams(sem),
        name=name,
    )(*args)
    return (res[0], res[1]) if emit_state else (res[0], None)


def _hgrn(z, lb, gain, s0t, prev_state, layer):
    lb3 = lb.reshape(2, 1, HG_K)
    o = state = None
    for latent in (False, True):
        hpb = _scan_geom(latent, HG_HEADS)[0]

        def feat_fn(reverse, latent=latent, hpb=hpb):
            zs = functools.partial(_zseg_spec, latent, reverse=reverse)
            return ([zs(OFF_HQ, hpb * HG_DK), zs(OFF_HFF, hpb * HG_DK), zs(OFF_HFB, hpb * HG_DK),
                     zs(OFF_HI, hpb * HG_DV), pl.BlockSpec((2, 1, hpb * HG_DK), lambda b, h, *_: (0, 0, h))],
                    [z, z, z, z, lb3])

        o, st = _scan_call("hgrn_latent" if latent else "hgrn_prompt", _hgrn_features, feat_fn,
                           z, OFF_HG, gain, HG_HEADS, HG_DK, HG_DV, latent, s0t, o, prev_state, layer)
        state = st if st is not None else state
    return o, state


def _gla(z, w_gk, b_gk, gain, s0t, prev_state, layer):
    wpad = jnp.zeros((2, LANE, GLA_HEADS * GLA_DK), F32)
    wpad = wpad.at[0, 0:GLA_RANK].set(w_gk[0]).at[1, GLA_RANK:2 * GLA_RANK].set(w_gk[1]).astype(BF16)
    b3 = b_gk.reshape(2, 1, GLA_HEADS * GLA_DK)
    o = state = None
    for latent in (False, True):
        hpb = _scan_geom(latent, GLA_HEADS)[0]

        def feat_fn(reverse, latent=latent, hpb=hpb):
            zs = functools.partial(_zseg_spec, latent, reverse=reverse)
            return ([zs(OFF_GQ, hpb * GLA_DK), zs(OFF_GK, hpb * GLA_DK), zs(OFF_GV, hpb * GLA_DV),
                     zs(OFF_GLF, LANE, per_head=False),
                     pl.BlockSpec((2, LANE, hpb * GLA_DK), lambda b, h, *_: (0, 0, h)),
                     pl.BlockSpec((2, 1, hpb * GLA_DK), lambda b, h, *_: (0, 0, h))],
                    [z, z, z, z, wpad, b3])

        o, st = _scan_call("gla_latent" if latent else "gla_prompt", _gla_features, feat_fn,
                           z, OFF_GG, gain, GLA_HEADS, GLA_DK, GLA_DV, latent, s0t, o, prev_state, layer)
        state = st if st is not None else state
    return o, state


GQA = N_HEADS // N_KV_HEADS


def _ctx_attn_kernel(q_ref, k_ref, v_ref, sink_ref, o_ref):
    hd = HEAD_DIM
    k = k_ref[...].astype(BF16)
    v = v_ref[...].astype(BF16)
    heads = range(GQA)
    ss = [_dot_nt((q_ref[:, g * hd:(g + 1) * hd] * (hd ** -0.5)).astype(BF16), k) for g in heads]
    ps, dens = [], []
    for g in heads:
        sink = sink_ref[g][:, 0:1]
        m = jnp.maximum(jnp.max(ss[g], axis=-1, keepdims=True), sink)
        p = jnp.exp(ss[g] - m)
        dens.append(jnp.sum(p, axis=-1, keepdims=True) + jnp.exp(sink - m))
        ps.append(p.astype(BF16))
    for g in heads:
        o_ref[:, g * hd:(g + 1) * hd] = (_dot(ps[g], v) / dens[g]).astype(o_ref.dtype)


def _ctx_attention(z, sink3):
    hd = HEAD_DIM
    return pl.pallas_call(
        _ctx_attn_kernel,
        grid=(BATCH, N_KV_HEADS),
        in_specs=[_zspec(False, OFF_AQ, GQA * hd),
                  pl.BlockSpec((SEQ, hd), lambda b, kh: (b, OFF_AK // hd + kh)),
                  pl.BlockSpec((SEQ, hd), lambda b, kh: (b, OFF_AV // hd + kh)),
                  pl.BlockSpec((GQA, 1, LANE), lambda b, kh: (kh, 0, 0))],
        out_specs=pl.BlockSpec((SEQ, GQA * hd), lambda b, kh: (b, kh)),
        out_shape=jax.ShapeDtypeStruct((T, N_HEADS * hd), BF16),
        compiler_params=_cparams(("parallel", "parallel")),
        name="ctx_attention",
    )(z, z, z, sink3)


def _rope(x, cos_f, sin_f):
    return x * cos_f + pltpu.roll(x, HEAD_DIM // 2, 1) * sin_f


def _lat_attn_kernel(q_ref, k_ref, v_ref, ck_ref, cv_ref, cos_ref, sin_ref, sink_ref, o_ref, kr_scr, vr_scr):
    blk, hd, seq = ATT_BLOCK, HEAD_DIM, DEC_SEQ
    nb = seq // blk

    zeros = jnp.zeros((blk, hd), BF16)
    for scr in (kr_scr, vr_scr):
        scr[0:blk, :] = zeros
        scr[blk + seq:2 * blk + seq, :] = zeros

    def fill(n, carry):
        r0 = pl.multiple_of(n * blk, blk)
        rows = pl.ds(r0, blk)
        kr_scr[pl.ds(r0 + blk, blk), :] = _rope(k_ref[rows, :], cos_ref[rows, :], sin_ref[rows, :]).astype(BF16)
        vr_scr[pl.ds(r0 + blk, blk), :] = v_ref[rows, :].astype(BF16)
        return carry

    lax.fori_loop(0, nb, fill, 0)

    ck = ck_ref[...].astype(BF16)
    cv = cv_ref[...].astype(BF16)
    sink = jnp.concatenate([jnp.broadcast_to(sink_ref[g][:, 0:1], (blk, 1)) for g in range(GQA)], axis=0)
    qi = lax.broadcasted_iota(jnp.int32, (GQA * blk, 3 * blk), 0) & (blk - 1)
    kj = lax.broadcasted_iota(jnp.int32, (GQA * blk, 3 * blk), 1)
    window_bias = jnp.where(jnp.abs(kj - qi - blk) <= WINDOW, 0.0, -jnp.inf)

    per_iter = 2

    def qblocks(it, carry):
        ns = [it * per_iter + u for u in range(per_iter)]
        r0s = [pl.multiple_of(n * blk, blk) for n in ns]
        qs = []
        for r0 in r0s:
            rows = pl.ds(r0, blk)
            cos_f, sin_f = cos_ref[rows, :], sin_ref[rows, :]
            qs.append(jnp.concatenate(
                [(_rope(q_ref[rows, g * hd:(g + 1) * hd], cos_f, sin_f) * (hd ** -0.5)).astype(BF16)
                 for g in range(GQA)], axis=0))
        bands = [pl.ds(r0, 3 * blk) for r0 in r0s]
        s_locs = [_dot_nt(q, kr_scr[band, :]) for q, band in zip(qs, bands)]
        s_ctxs = [_dot_nt(q, ck) for q in qs]
        ps = []
        for n, s_loc, s_ctx in zip(ns, s_locs, s_ctxs):
            key_pos = kj + (n - 1) * blk
            s_loc = s_loc + window_bias
            s_loc = jnp.where(key_pos >= 0, s_loc, -jnp.inf)
            s_loc = jnp.where(key_pos < seq, s_loc, -jnp.inf)
            m = jnp.maximum(jnp.maximum(jnp.max(s_loc, axis=-1, keepdims=True),
                                        jnp.max(s_ctx, axis=-1, keepdims=True)), sink)
            p_loc = jnp.exp(s_loc - m)
            p_ctx = jnp.exp(s_ctx - m)
            den = (jnp.sum(p_loc, axis=-1, keepdims=True) + jnp.sum(p_ctx, axis=-1, keepdims=True)
                   + jnp.exp(sink - m))
            ps.append((p_ctx.astype(BF16), p_loc.astype(BF16), den))
        for r0, band, (p_ctx, p_loc, den) in zip(r0s, bands, ps):
            o = (_dot(p_ctx, cv) + _dot(p_loc, vr_scr[band, :])) / den
            for g in range(GQA):
                o_ref[pl.ds(r0, blk), g * hd:(g + 1) * hd] = o[g * blk:(g + 1) * blk].astype(o_ref.dtype)
        return carry

    lax.fori_loop(0, nb // per_iter, qblocks, 0)


def _rope_tables():
    n = jnp.arange(DEC_SEQ)
    row = (n // GRID_W).astype(F32)
    col = (n % GRID_W).astype(F32)
    n_freq = HEAD_DIM // 4
    inv = ROPE_THETA ** (-jnp.arange(n_freq, dtype=F32) / n_freq)
    ang = jnp.concatenate([row[:, None] * inv, col[:, None] * inv], axis=-1)
    cos, sin = jnp.cos(ang), jnp.sin(ang)
    return jnp.concatenate([cos, cos], axis=-1), jnp.concatenate([-sin, sin], axis=-1)


def _lat_attention(z, ck, cv, sink3, prev_out):
    hd = HEAD_DIM
    seq, _, blk0 = _seq_geom(True)
    cos_f, sin_f = _rope_tables()
    tab = pl.BlockSpec((seq, hd), lambda b, kh: (0, 0))
    cache = pl.BlockSpec((None, PAST_LEN, hd), lambda b, kh: (b, 0, kh))
    return pl.pallas_call(
        lambda *refs: _lat_attn_kernel(*refs[:8], *refs[9:]),
        grid=(DEC_BATCH, N_KV_HEADS),
        in_specs=[_zspec(True, OFF_AQ, GQA * hd),
                  pl.BlockSpec((seq, hd), lambda b, kh: (blk0 + b, OFF_AK // hd + kh)),
                  pl.BlockSpec((seq, hd), lambda b, kh: (blk0 + b, OFF_AV // hd + kh)),
                  cache, cache, tab, tab,
                  pl.BlockSpec((GQA, 1, LANE), lambda b, kh: (kh, 0, 0)),
                  pl.BlockSpec(memory_space=pl.ANY)],
        out_specs=pl.BlockSpec((seq, GQA * hd), lambda b, kh: (blk0 + b, kh)),
        out_shape=jax.ShapeDtypeStruct((T, N_HEADS * hd), BF16),
        input_output_aliases={8: 0},
        scratch_shapes=[pltpu.VMEM((seq + 2 * ATT_BLOCK, hd), BF16), pltpu.VMEM((seq + 2 * ATT_BLOCK, hd), BF16)],
        compiler_params=_cparams(("parallel", "parallel")),
        name="lat_attention",
    )(z, z, z, ck, cv, cos_f, sin_f, sink3, prev_out)


def kernel(x_prompt, x_sample, cache_k, cache_v, state_hgrn, state_gla, c, c_ctx, w_ada, b_ada, norm_g, w_in,
           w_out, sgu_ln_g, sgu_ln_b, sgu_w, sgu_b, hgrn_lb_logits, hgrn_norm_g, attn_sink, gla_w_gk, gla_b_gk,
           gla_norm_g, ffn_w_up, ffn_conv_w, ffn_conv_b, ffn_w_down):
    assert TP % DEC_SEQ == 0 and DEC_BATCH + 1 <= MOD_ROWS
    lb_all = jnp.cumsum(jax.nn.softmax(hgrn_lb_logits.astype(F32), axis=0), axis=0)
    lb_all = lb_all - lb_all[0:1]

    cvec = jnp.zeros((MOD_ROWS, D_MODEL), F32).at[0].set(c_ctx).at[1:1 + DEC_BATCH].set(c)
    mod = _ada_mod(cvec, w_ada, b_ada).reshape(DEPTH, MOD_ROWS, 1, 6 * D_MODEL)
    SH1, SC1, G1, SH2, SC2, G2 = range(6)

    x = [x_prompt.reshape(TP, D_MODEL), x_sample.reshape(TL, D_MODEL)]
    hg_s0t = jnp.swapaxes(state_hgrn, -1, -2)
    gl_s0t = jnp.swapaxes(state_gla, -1, -2)
    ck_all = cache_k.reshape(DEC_BATCH, DEPTH, PAST_LEN, N_KV_HEADS * HEAD_DIM)
    cv_all = cache_v.reshape(DEC_BATCH, DEPTH, PAST_LEN, N_KV_HEADS * HEAD_DIM)

    h = _norm_mod(x, norm_g[0, 0], mod, 0, SC1, SH1)
    ks_new, vs_new = [], []
    hg_state = gl_state = None
    w_in_b, w_out_b = w_in.astype(BF16), w_out.astype(BF16)
    for l in range(DEPTH):
        sink3 = jnp.broadcast_to(attn_sink[l].astype(F32)[:, None, None], (N_HEADS, 1, LANE))

        z = _matmul(h, w_in_b, l, Z_COLS, F32, Z_TILE_N, "in_proj")
        o_sgu = _sgu(z, sgu_ln_g[l], sgu_ln_b[l], sgu_w[l], sgu_b[l])
        o_hg, hg_state = _hgrn(z, lb_all[l], hgrn_norm_g[l], hg_s0t, hg_state, l)
        o_att = _ctx_attention(z, sink3)
        o_att = _lat_attention(z, ck_all[:, l], cv_all[:, l], sink3, o_att)
        o_gl, gl_state = _gla(z, gla_w_gk[l], gla_b_gk[l], gla_norm_g[l], gl_s0t, gl_state, l)
        m, w_up_b = _out_proj([o_sgu, o_hg, o_att, o_gl], w_out_b, ffn_w_up, l)
        x, h = _resid(x, m, norm_g[l, 1], mod, l, G1, nxt=(norm_g[l, 2], l, SC2, SH2))
        ks_new.append(z[:TP, OFF_AK:OFF_AK + N_KV_HEADS * HEAD_DIM].reshape(BATCH, SEQ, N_KV_HEADS, HEAD_DIM))
        vs_new.append(z[:TP, OFF_AV:OFF_AV + N_KV_HEADS * HEAD_DIM].reshape(BATCH, SEQ, N_KV_HEADS, HEAD_DIM))

        act, w_down_b = _ffn_up(h, _conv_halo(h), w_up_b, ffn_conv_w, ffn_conv_b, ffn_w_down, l)
        f = _ffn_down(act, w_down_b, 0)
        if l + 1 < DEPTH:
            x, h = _resid(x, f, norm_g[l, 3], mod, l, G2, nxt=(norm_g[l + 1, 0], l + 1, SC1, SH1))
        else:
            x, h = _resid(x, f, norm_g[l, 3], mod, l, G2, split_out=True)

    y_prompt = x[0].reshape(BATCH, SEQ, D_MODEL)
    y_sample = x[1].reshape(DEC_BATCH, DEC_SEQ, D_MODEL)
    return (y_prompt, y_sample, jnp.stack(ks_new, axis=1), jnp.stack(vs_new, axis=1), hg_state, gl_state)
```

```python
import functools

import numpy as np
import jax
import jax.numpy as jnp
from jax import lax
from jax.experimental import pallas as pl
from jax.experimental.pallas import tpu as pltpu

D_MODEL = 4096
BATCH = 32
SEQ = 256
DEPTH = 2
DEC_BATCH = 2
DEC_SEQ = 4096
PAST_LEN = 512
GRID_W = 64
GROUP_W = D_MODEL // 4
SGU_CHUNK = 128
SGU_GROUPS = 4
SGU_DIM = GROUP_W
SGU_GCH = SGU_DIM // SGU_GROUPS
HG_HEADS = 8
HG_DK = 128
HG_DV = GROUP_W // HG_HEADS
HG_K = HG_HEADS * HG_DK
N_HEADS = 8
N_KV_HEADS = 2
HEAD_DIM = GROUP_W // N_HEADS
WINDOW = 128
ATT_BLOCK = 128
ROPE_THETA = 10000.0
GLA_HEADS = 4
GLA_DK = 128
GLA_DV = GROUP_W // GLA_HEADS
GLA_RANK = 16
GLA_NORMALIZER = 16.0
D_FF = 11008
CONV_W = 3
EPS = 1e-6

F32 = jnp.float32
BF16 = jnp.bfloat16
LOG2E = 1.4426950408889634

IN_SIZES = (2 * SGU_DIM,
            HG_K, HG_K, HG_K, HG_HEADS * HG_DV, HG_HEADS * HG_DV,
            N_HEADS * HEAD_DIM, N_KV_HEADS * HEAD_DIM, N_KV_HEADS * HEAD_DIM,
            GLA_HEADS * GLA_DK, GLA_HEADS * GLA_DK, GLA_HEADS * GLA_DV, GLA_HEADS * GLA_DV,
            GLA_RANK, GLA_RANK)
D_IN = sum(IN_SIZES)
_OFF = [0] + [int(c) for c in np.cumsum(IN_SIZES)]
(OFF_SGU, OFF_HQ, OFF_HFF, OFF_HFB, OFF_HI, OFF_HG, OFF_AQ, OFF_AK, OFF_AV,
 OFF_GQ, OFF_GK, OFF_GV, OFF_GG, OFF_GLF, OFF_GLB) = _OFF[:-1]

LANE = 128
SUBLANE = 8
VMEM_LIMIT = 56 * 1024 * 1024

Z_TILE_N = 1024
Z_COLS = -(-D_IN // Z_TILE_N) * Z_TILE_N
SCAN_C = 128
MOD_ROWS = 16

TP = BATCH * SEQ
TL = DEC_BATCH * DEC_SEQ
T = TP + TL
PIECES = ((0, TP), (TP, TL))


def _cparams(sem):
    return pltpu.CompilerParams(dimension_semantics=sem, vmem_limit_bytes=VMEM_LIMIT)


def _row_group(row0):
    return jnp.where(row0 < TP, 0, 1 + (row0 - TP) // DEC_SEQ)


def _dot(a, b):
    return jnp.dot(a, b, preferred_element_type=F32)


def _dot_nt(a, b):
    return lax.dot_general(a, b, (((1,), (1,)), ((), ())), preferred_element_type=F32)


def _dot_tn(a, b):
    return lax.dot_general(a, b, (((0,), (0,)), ((), ())), preferred_element_type=F32)


def _sigmoid(x):
    return 1.0 / (1.0 + jnp.exp(-x))


def _silu(x):
    return x * _sigmoid(x)


def _gelu(x):
    return 0.5 * x * (1.0 + lax.erf(x * (2.0 ** -0.5)))


def _rms(x, g):
    return x * lax.rsqrt(jnp.mean(x * x, axis=-1, keepdims=True) + EPS) * g


def _ada_kernel(c_ref, w_ref, b_ref, o_ref):
    c = _silu(c_ref[...]).astype(BF16)
    o_ref[...] = _dot(c, w_ref[...].astype(BF16)) + b_ref[...]


def _ada_mod(cvec, w_ada, b_ada):
    tn = 512
    n = w_ada.shape[-1]
    return pl.pallas_call(
        _ada_kernel,
        grid=(DEPTH, n // tn),
        in_specs=[pl.BlockSpec((MOD_ROWS, D_MODEL), lambda l, j: (0, 0)),
                  pl.BlockSpec((None, D_MODEL, tn), lambda l, j: (l, 0, j)),
                  pl.BlockSpec((None, 1, tn), lambda l, j: (l, 0, j))],
        out_specs=pl.BlockSpec((None, MOD_ROWS, tn), lambda l, j: (l, 0, j)),
        out_shape=jax.ShapeDtypeStruct((DEPTH, MOD_ROWS, n), F32),
        compiler_params=_cparams(("parallel", "parallel")),
        name="ada_mod",
    )(cvec, w_ada, b_ada.reshape(DEPTH, 1, n))


def _mod_spec(layer, chunk, tr, blk0=0):
    return pl.BlockSpec((None, None, 1, D_MODEL),
                        lambda i, *_: (layer, _row_group((blk0 + i) * tr), 0, chunk))


NORM_ROWS = 256
_VEC_SPEC = pl.BlockSpec((1, D_MODEL), lambda i: (0, 0))


def _row_spec(blk0=0):
    return pl.BlockSpec((NORM_ROWS, D_MODEL), lambda i: (blk0 + i, 0))


def _norm_mod_kernel(x_ref, g_ref, sc_ref, sh_ref, h_ref):
    y = _rms(x_ref[...], g_ref[...])
    h_ref[...] = (y * (1.0 + sc_ref[...]) + sh_ref[...]).astype(h_ref.dtype)


def _norm_mod(x_parts, g, mod, layer, sc_chunk, sh_chunk):
    tr = NORM_ROWS
    h = None
    for x, (row0, rows) in zip(x_parts, PIECES):
        blk0 = row0 // tr
        in_specs = [_row_spec(), _VEC_SPEC, _mod_spec(layer, sc_chunk, tr, blk0), _mod_spec(layer, sh_chunk, tr, blk0)]
        args = [x, g.reshape(1, D_MODEL), mod, mod]
        aliases = {}
        if h is not None:
            in_specs.append(pl.BlockSpec(memory_space=pl.ANY))
            aliases[len(args)] = 0
            args.append(h)
        h = pl.pallas_call(
            lambda x_ref, g_ref, sc_ref, sh_ref, *rest: _norm_mod_kernel(x_ref, g_ref, sc_ref, sh_ref, rest[-1]),
            grid=(rows // tr,),
            in_specs=in_specs,
            out_specs=_row_spec(blk0),
            out_shape=jax.ShapeDtypeStruct((T, D_MODEL), BF16),
            input_output_aliases=aliases,
            compiler_params=_cparams(("parallel",)),
            name="norm_mod",
        )(*args)
    return h


def _resid_kernel(x_ref, m_ref, ga_ref, gate_ref, *rest, emit_h):
    x = x_ref[...] + gate_ref[...] * _rms(m_ref[...].astype(F32), ga_ref[...])
    if emit_h:
        gb_ref, sc_ref, sh_ref, xo_ref, h_ref = rest
        xo_ref[...] = x
        y = _rms(x, gb_ref[...])
        h_ref[...] = (y * (1.0 + sc_ref[...]) + sh_ref[...]).astype(h_ref.dtype)
    else:
        (xo_ref,) = rest
        xo_ref[...] = x


def _resid(x, m, ga, mod, layer, gate_chunk, nxt=None, split_out=False):
    tr = NORM_ROWS
    x_parts = x if isinstance(x, (list, tuple)) else None
    pieces = PIECES if (x_parts is not None or split_out) else ((0, T),)
    emit_h = nxt is not None
    assert not (emit_h and split_out)
    x_new, h, outs = None, None, []
    for p, (row0, rows) in enumerate(pieces):
        blk0 = row0 // tr
        in_specs = [_row_spec(0 if x_parts is not None else blk0), _row_spec(blk0), _VEC_SPEC,
                    _mod_spec(layer, gate_chunk, tr, blk0)]
        args = [x_parts[p] if x_parts is not None else x, m, ga.reshape(1, D_MODEL), mod]
        if emit_h:
            gb, layer_b, sc_chunk, sh_chunk = nxt
            in_specs += [_VEC_SPEC, _mod_spec(layer_b, sc_chunk, tr, blk0), _mod_spec(layer_b, sh_chunk, tr, blk0)]
            args += [gb.reshape(1, D_MODEL), mod, mod]
        n_in = len(args)
        if split_out:
            out_specs = [_row_spec()]
            out_shape = [jax.ShapeDtypeStruct((rows, D_MODEL), F32)]
        else:
            out_specs = [_row_spec(blk0)]
            out_shape = [jax.ShapeDtypeStruct((T, D_MODEL), F32)]
        if emit_h:
            out_specs.append(_row_spec(blk0))
            out_shape.append(jax.ShapeDtypeStruct((T, D_MODEL), BF16))
        aliases = {}
        if x_new is not None and not split_out:
            for k, prev in enumerate((x_new, h) if emit_h else (x_new,)):
                in_specs.append(pl.BlockSpec(memory_space=pl.ANY))
                aliases[len(args)] = k
                args.append(prev)
        n_alias = len(aliases)

        def kern(*refs, n_in=n_in, n_alias=n_alias):
            _resid_kernel(*refs[:n_in], *refs[n_in + n_alias:], emit_h=emit_h)

        res = pl.pallas_call(
            kern,
            grid=(rows // tr,),
            in_specs=in_specs,
            out_specs=out_specs,
            out_shape=out_shape,
            input_output_aliases=aliases,
            compiler_params=_cparams(("parallel",)),
            name="resid_norm",
        )(*args)
        x_new = res[0]
        h = res[1] if emit_h else None
        outs.append(res[0])
    return (outs if split_out else x_new), h


MM_TILE_M = 1024


def _mm_kernel(a_ref, b_ref, o_ref, *, n_valid):
    acc = _dot(a_ref[...], b_ref[...])
    tn = o_ref.shape[1]
    if n_valid % tn:
        col = lax.broadcasted_iota(jnp.int32, acc.shape, 1)
        acc = jnp.where(col < n_valid - pl.program_id(1) * tn, acc, 0.0)
    o_ref[...] = acc.astype(o_ref.dtype)


def _matmul(a, w, layer, n_out, out_dtype, tn, name):
    m, k = a.shape
    tm = MM_TILE_M
    return pl.pallas_call(
        functools.partial(_mm_kernel, n_valid=w.shape[2]),
        grid=(m // tm, n_out // tn),
        in_specs=[pl.BlockSpec((tm, k), lambda i, j: (i, 0)),
                  pl.BlockSpec((None, k, tn), lambda i, j: (layer, 0, j))],
        out_specs=pl.BlockSpec((tm, tn), lambda i, j: (i, j)),
        out_shape=jax.ShapeDtypeStruct((m, n_out), out_dtype),
        compiler_params=_cparams(("parallel", "parallel")),
        name=name,
    )(a, w)


def _mm4_kernel(a0_ref, a1_ref, a2_ref, a3_ref, b_ref, o_ref):
    acc = _dot(a0_ref[...], b_ref[0 * GROUP_W:1 * GROUP_W, :])
    acc += _dot(a1_ref[...], b_ref[1 * GROUP_W:2 * GROUP_W, :])
    acc += _dot(a2_ref[...], b_ref[2 * GROUP_W:3 * GROUP_W, :])
    acc += _dot(a3_ref[...], b_ref[3 * GROUP_W:4 * GROUP_W, :])
    o_ref[...] = acc.astype(o_ref.dtype)


def _out_proj(parts, w, layer):
    tm, tn = MM_TILE_M, 1024
    a_spec = pl.BlockSpec((tm, GROUP_W), lambda i, j: (i, 0))
    return pl.pallas_call(
        _mm4_kernel,
        grid=(T // tm, D_MODEL // tn),
        in_specs=[a_spec, a_spec, a_spec, a_spec,
                  pl.BlockSpec((None, 4 * GROUP_W, tn), lambda i, j: (layer, 0, j))],
        out_specs=pl.BlockSpec((tm, tn), lambda i, j: (i, j)),
        out_shape=jax.ShapeDtypeStruct((T, D_MODEL), BF16),
        compiler_params=_cparams(("parallel", "parallel")),
        name="out_proj",
    )(*parts, w)


def _ffn_down(a, w, layer):
    m, kk = a.shape
    n = w.shape[2]
    tm, tn = 512, 512
    return pl.pallas_call(
        functools.partial(_mm_kernel, n_valid=n),
        grid=(m // tm, n // tn),
        in_specs=[pl.BlockSpec((tm, kk), lambda i, j: (i, 0)),
                  pl.BlockSpec((None, kk, tn), lambda i, j: (layer, 0, j))],
        out_specs=pl.BlockSpec((tm, tn), lambda i, j: (i, j)),
        out_shape=jax.ShapeDtypeStruct((m, n), BF16),
        compiler_params=_cparams(("parallel", "parallel")),
        name="ffn_down",
    )(a, w)


FFN_TILE_M = 2048
FFN_TILE_N = 256
FFN_SUB_M = 256
HALO_ROWS = 16
assert D_FF % FFN_TILE_N == 0


def _ffn_up_kernel(h_ref, halo_ref, wg_ref, wv_ref, cw_ref, cb_ref, wd_ref, o_ref, wdb_ref):
    wdb_ref[...] = wd_ref[...].astype(wdb_ref.dtype)
    tm, tn = o_ref.shape
    sub = FFN_SUB_M
    ns = tm // sub
    wg, wv = wg_ref[...].astype(BF16), wv_ref[...].astype(BF16)
    row0 = pl.program_id(0) * tm
    seq_len = jnp.where(row0 < TP, SEQ, DEC_SEQ)
    r = lax.broadcasted_iota(jnp.int32, (sub, tn), 0)
    g0 = _dot(jnp.concatenate([halo_ref[...], h_ref[0:sub, :]], axis=0), wg)
    gh = g0[0:HALO_ROWS]
    gs = [g0[HALO_ROWS:]] + [_dot(h_ref[s * sub:(s + 1) * sub, :], wg) for s in range(1, ns)]
    for s in range(ns):
        val = _dot(h_ref[s * sub:(s + 1) * sub, :], wv)
        g = gs[s]
        before = gh[0:1, :] if s == 0 else gs[s - 1][sub - 1:sub, :]
        after = gh[1:2, :] if s == ns - 1 else gs[s + 1][0:1, :]
        pos = (row0 + s * sub + r) & (seq_len - 1)
        prev = jnp.where(r == 0, before, pltpu.roll(g, 1, 0))
        prev = jnp.where(pos == 0, 0.0, prev)
        nxt = jnp.where(r == sub - 1, after, pltpu.roll(g, sub - 1, 0))
        nxt = jnp.where(pos == seq_len - 1, 0.0, nxt)
        conv = prev * cw_ref[0:1, :] + g * cw_ref[1:2, :] + nxt * cw_ref[2:3, :] + cb_ref[...]
        o_ref[s * sub:(s + 1) * sub, :] = (_gelu(conv) * val).astype(o_ref.dtype)


def _ffn_up(h, halo, w_up, conv_w, conv_b, w_down, layer):
    tm, tn = FFN_TILE_M, FFN_TILE_N
    nj = D_FF // tn
    steps = (T // tm) * nj
    wd_rows = D_FF // steps
    assert D_FF % steps == 0 and wd_rows % 16 == 0
    wd_in = pl.BlockSpec((None, wd_rows, D_MODEL), lambda i, j: (layer, i * nj + j, 0))
    wd_out = pl.BlockSpec((None, wd_rows, D_MODEL), lambda i, j: (0, i * nj + j, 0))
    return pl.pallas_call(
        _ffn_up_kernel,
        grid=(T // tm, nj),
        in_specs=[pl.BlockSpec((tm, D_MODEL), lambda i, j: (i, 0), pipeline_mode=pl.Buffered(1)),
                  pl.BlockSpec((None, HALO_ROWS, D_MODEL), lambda i, j: (i, 0, 0)),
                  pl.BlockSpec((None, D_MODEL, tn), lambda i, j: (layer, 0, j)),
                  pl.BlockSpec((None, D_MODEL, tn), lambda i, j: (layer, 0, nj + j)),
                  pl.BlockSpec((None, CONV_W, tn), lambda i, j: (layer, 0, j)),
                  pl.BlockSpec((None, 1, tn), lambda i, j: (layer, 0, j)),
                  wd_in],
        out_specs=[pl.BlockSpec((tm, tn), lambda i, j: (i, j)), wd_out],
        out_shape=[jax.ShapeDtypeStruct((T, D_FF), BF16), jax.ShapeDtypeStruct((1, D_FF, D_MODEL), BF16)],
        compiler_params=_cparams(("parallel", "parallel")),
        name="ffn_up",
    )(h, halo, w_up, w_up, conv_w, conv_b.reshape(DEPTH, 1, D_FF), w_down)


def _conv_halo(h):
    tm = FFN_TILE_M
    nt = T // tm
    ht = h.reshape(nt, tm, D_MODEL)
    zero = jnp.zeros((1, D_MODEL), h.dtype)
    before = jnp.concatenate([zero, ht[:-1, tm - 1]], axis=0)
    after = jnp.concatenate([ht[1:, 0], zero], axis=0)
    pad = jnp.zeros((nt, HALO_ROWS - 2, D_MODEL), h.dtype)
    return jnp.concatenate([before[:, None], after[:, None], pad], axis=1)


SGU_CHUNKS_PER_STEP = 4


def _sgu_kernel(u_ref, v_ref, g_ref, b_ref, w_ref, bias_ref, o_ref):
    c = SGU_CHUNK
    for ci in range(SGU_CHUNKS_PER_STEP):
        rows = slice(ci * c, (ci + 1) * c)
        u = _gelu(u_ref[rows, :])
        v = _gelu(v_ref[rows, :])
        vc = v - jnp.mean(v, axis=-1, keepdims=True)
        vn = vc * lax.rsqrt(jnp.mean(vc * vc, axis=-1, keepdims=True) + EPS) * g_ref[...] + b_ref[...]
        vb = vn.astype(BF16)
        for g in range(SGU_GROUPS):
            sl = slice(g * SGU_GCH, (g + 1) * SGU_GCH)
            mixed = _dot(w_ref[g], vb[:, sl]) + bias_ref[:, sl]
            o_ref[rows, sl] = (u[:, sl] * mixed).astype(o_ref.dtype)


def _sgu(z, ln_g, ln_b, w_s, b_s):
    c = SGU_CHUNK
    rows = SGU_CHUNKS_PER_STEP * c
    bias = jnp.repeat(b_s.T, SGU_GCH, axis=1)
    vec = pl.BlockSpec((1, SGU_DIM), lambda i: (0, 0))
    return pl.pallas_call(
        _sgu_kernel,
        grid=(T // rows,),
        in_specs=[pl.BlockSpec((rows, SGU_DIM), lambda i: (i, OFF_SGU // SGU_DIM)),
                  pl.BlockSpec((rows, SGU_DIM), lambda i: (i, OFF_SGU // SGU_DIM + 1)),
                  vec, vec,
                  pl.BlockSpec((SGU_GROUPS, c, c), lambda i: (0, 0, 0)),
                  pl.BlockSpec((c, SGU_DIM), lambda i: (0, 0))],
        out_specs=pl.BlockSpec((rows, SGU_DIM), lambda i: (i, 0)),
        out_shape=jax.ShapeDtypeStruct((T, SGU_DIM), BF16),
        compiler_params=_cparams(("parallel",)),
        name="sgu",
    )(z, z, ln_g.reshape(1, SGU_DIM), ln_b.reshape(1, SGU_DIM), w_s.astype(BF16), bias)


SMALL_LEVELS = tuple(m for m in (1, 2, 4) if m < SUBLANE)
BIG_LEVELS = tuple(m for m in (8, 16, 32, 64) if m < SCAN_C)


def _scan_consts():
    c = SCAN_C
    t = np.arange(c)[:, None]
    s = np.arange(c)[None, :]
    tri = np.stack([s <= t, s >= t]).astype(np.float32)
    diag = (t == s).astype(np.float32)
    hb = np.arange(c // 2)
    same = []
    for m in SMALL_LEVELS:
        nblk = c // (2 * m)
        same.append((hb[:, None] % nblk) == (hb[None, :] % nblk))
    for m in BIG_LEVELS:
        same.append((hb[:, None] // m) == (hb[None, :] // m))
    same = np.stack(same).astype(np.float32)
    return jnp.asarray(tri, BF16), jnp.asarray(diag, F32), jnp.asarray(same, F32)


def _split2(x):
    hi = x.astype(BF16)
    lo = (x - hi.astype(F32)).astype(BF16)
    return hi, lo


def _halves(a, m, second):
    off = m if second else 0
    return jnp.concatenate([a[j + off:j + off + m] for j in range(0, a.shape[0], 2 * m)], axis=0)


def _strided_halves(ref, m, second):
    c = ref.shape[0]
    off = m if second else 0
    return jnp.concatenate([ref[pl.ds(off + r, c // (2 * m), stride=2 * m), :] for r in range(m)], axis=0)


def _chunk_scans(probs, consts, scratch):
    tri_ref, diag_ref, same_ref = consts
    qkb_scr, vo_scr = scratch
    c, dk = probs[0][0].shape
    n_cg = vo_scr.shape[2]
    n_small = len(SMALL_LEVELS)
    dirs = [1 if p[5] else 0 for p in probs]
    vbs = [p[2].astype(BF16) for p in probs]
    diag = [diag_ref[...] * _dot_nt(p[0].astype(BF16), p[1].astype(BF16)) for p in probs]
    bs = []
    for p, d in zip(probs, dirs):
        hi, lo = _split2(p[3])
        tri = tri_ref[d]
        bs.append(_dot(tri, hi) + _dot(tri, lo))
    inter = [_dot_nt((p[0] * jnp.exp2(b)).astype(BF16), p[4].astype(BF16)) for p, b in zip(probs, bs)]
    for pi, (p, b) in enumerate(zip(probs, bs)):
        qkb_scr[pi, 0] = p[0]
        qkb_scr[pi, 1] = p[1]
        qkb_scr[pi, 2] = b
        for cg in range(n_cg):
            vo_scr[pi, 0, cg] = p[2][:, cg * LANE:(cg + 1) * LANE]

    big = [[] for _ in probs]
    for li, m in enumerate(BIG_LEVELS):
        for pi, ((q, k, _, _, _, reverse), b) in enumerate(zip(probs, bs)):
            q_second = not reverse
            refs = [b[j + m:j + m + 1] if reverse else b[j + m - 1:j + m] for j in range(0, c, 2 * m)]
            ref = jnp.concatenate([jnp.broadcast_to(r, (m, dk)) for r in refs], axis=0)
            qf = _halves(q, m, q_second) * jnp.exp2(_halves(b, m, q_second) - ref)
            kf = _halves(k, m, not q_second) * jnp.exp2(ref - _halves(b, m, not q_second))
            s_l = _dot_nt(qf.astype(BF16), kf.astype(BF16))
            if 2 * m < c:
                s_l = s_l * same_ref[n_small + li]
            big[pi].append(s_l.astype(BF16))

    small = [[] for _ in probs]
    for li, m in enumerate(SMALL_LEVELS):
        for pi, (_, _, _, _, _, reverse) in enumerate(probs):
            q_second = not reverse
            q_ref, k_ref, b_ref = qkb_scr.at[pi, 0], qkb_scr.at[pi, 1], qkb_scr.at[pi, 2]
            nblk = c // (2 * m)
            bk = _strided_halves(b_ref, m, not q_second)
            qg = _strided_halves(q_ref, m, q_second)
            kg = _strided_halves(k_ref, m, not q_second)
            if m == 1:
                ref, kf = bk, kg
            else:
                refrow = b_ref[pl.ds(m if reverse else m - 1, nblk, stride=2 * m), :]
                ref = jnp.concatenate([refrow] * m, axis=0)
                kf = kg * jnp.exp2(ref - bk)
            qf = qg * jnp.exp2(_strided_halves(b_ref, m, q_second) - ref)
            s_l = _dot_nt(qf.astype(BF16), kf.astype(BF16)) * same_ref[li]
            small[pi].append(s_l.astype(BF16))

    outs = []
    for pi, ((q, k, v, _, st, reverse), b) in enumerate(zip(probs, bs)):
        o = inter[pi] + _dot(diag[pi].astype(BF16), vbs[pi])
        parts = [o[g:g + SUBLANE] for g in range(0, c, SUBLANE)]
        q_second = not reverse
        for li, m in enumerate(BIG_LEVELS):
            oc = _dot(big[pi][li], _halves(v, m, not q_second).astype(BF16))
            for jj, j in enumerate(range(0, c, 2 * m)):
                for g in range(0, m, SUBLANE):
                    dst = (j + (m if q_second else 0) + g) // SUBLANE
                    parts[dst] = parts[dst] + oc[jj * m + g:jj * m + g + SUBLANE]
        o = jnp.concatenate(parts, axis=0)
        for cg in range(n_cg):
            vo_scr[pi, 1, cg] = o[:, cg * LANE:(cg + 1) * LANE]
        for li, m in enumerate(SMALL_LEVELS):
            nblk = c // (2 * m)
            vk = jnp.concatenate([_strided_halves(vo_scr.at[pi, 0, cg], m, not q_second) for cg in range(n_cg)],
                                 axis=1)
            oc = _dot(small[pi][li], vk.astype(BF16))
            for r in range(m):
                rows = pl.ds((m if q_second else 0) + r, nblk, stride=2 * m)
                for cg in range(n_cg):
                    vo_scr[pi, 1, cg, rows, :] = (vo_scr[pi, 1, cg, rows, :]
                                                  + oc[r * nblk:(r + 1) * nblk, cg * LANE:(cg + 1) * LANE])
        o = jnp.concatenate([vo_scr[pi, 1, cg] for cg in range(n_cg)], axis=1)
        bl = b[0:1, :] if reverse else b[c - 1:c, :]
        kd = (k * jnp.exp2(bl - b)).astype(BF16)
        st_new = jnp.exp2(bl) * st + _dot_tn(vbs[pi], kd)
        outs.append((o, st_new))
    return outs


def _scan_kernel(*refs, features, n_in, seq, nseg, hpb, dv, with_s0, emit_state):
    consts = refs[:3]
    pos = 3
    dir_refs = []
    for _ in range(1 if nseg == 1 else 2):
        dir_refs.append((refs[pos:pos + n_in], refs[pos + n_in]))
        pos += n_in + 1
    dir_refs = dir_refs * 2 if nseg == 1 else dir_refs
    gain_ref = refs[pos]
    pos += 1
    s0_ref = None
    if with_s0:
        s0_ref = refs[pos]
        pos += 1
    o_ref = refs[pos]
    pos += 1
    sf_ref = None
    if emit_state:
        sf_ref = refs[pos]
        pos += 1
    o_scr, st_scr = refs[pos:pos + 2]
    gather_scr = refs[pos + 2:pos + 4]
    c = SCAN_C
    n = seq // c
    cps = n // nseg
    g = pl.program_id(2) if nseg > 1 else 0
    work = [(hh, reverse) for hh in range(hpb) for reverse in (False, True)]

    def init():
        st_scr[...] = s0_ref[...] if with_s0 else jnp.zeros(st_scr.shape, F32)

    def step(i, finalize):
        loc = {False: pl.multiple_of(i * c, c), True: pl.multiple_of((cps - 1 - i) * c, c)}
        pair = g * cps + i
        glob = {False: pl.multiple_of(pair * c, c), True: pl.multiple_of((n - 1 - pair) * c, c)}
        probs = [features(dir_refs[rev][0], loc[rev], hh, rev) + (st_scr[1 if rev else 0, hh], rev)
                 for hh, rev in work]
        for (hh, rev), (o, st) in zip(work, _chunk_scans(probs, consts, gather_scr)):
            st_scr[1 if rev else 0, hh] = st
            rows, cols = pl.ds(glob[rev], c), slice(hh * dv, (hh + 1) * dv)
            if finalize:
                gate = dir_refs[rev][1][pl.ds(loc[rev], c), cols]
                y = _rms(o + o_scr[rows, cols], gain_ref[...]) * _silu(gate)
                o_ref[rows, cols] = y.astype(o_ref.dtype)
            else:
                o_scr[rows, cols] = o

    def loop(lo, hi, finalize):
        lax.fori_loop(lo, hi, lambda i, carry: (step(i, finalize), carry)[1], 0)

    if nseg == 1:
        init()
        loop(0, n // 2, False)
        loop(n // 2, n, True)
    else:
        pl.when(g == 0)(init)
        pl.when(g < nseg // 2)(lambda: loop(0, cps, False))
        pl.when(g >= nseg // 2)(lambda: loop(0, cps, True))
    if emit_state:
        assert nseg == 1
        for d in range(2):
            for hh in range(hpb):
                sf_ref[d, hh] = st_scr[d, hh].T


def _hgrn_features(in_refs, r0, head, reverse):
    hq_ref, hf_ref, hb_ref, hi_ref, lb_ref = in_refs
    rows, cols = pl.ds(r0, SCAN_C), slice(head * HG_DK, (head + 1) * HG_DK)
    q = _silu(hq_ref[rows, cols]) * (HG_DK ** -0.5)
    lb = lb_ref[1 if reverse else 0][:, cols]
    f = lb + (1.0 - lb) * _sigmoid((hb_ref if reverse else hf_ref)[rows, cols])
    return q, 1.0 - f, hi_ref[rows, head * HG_DV:(head + 1) * HG_DV], jnp.log(f) * LOG2E


def _gla_features(in_refs, r0, head, reverse):
    gq_ref, gk_ref, gv_ref, low_ref, wgk_ref, bgk_ref = in_refs
    rows, cols = pl.ds(r0, SCAN_C), slice(head * GLA_DK, (head + 1) * GLA_DK)
    d = 1 if reverse else 0
    g = _dot(low_ref[rows, :].astype(BF16), wgk_ref[d][:, cols]) + bgk_ref[d][:, cols]
    la = (jnp.minimum(g, 0.0) - jnp.log(1.0 + jnp.exp(-jnp.abs(g)))) * (LOG2E / GLA_NORMALIZER)
    return (gq_ref[rows, cols] * (GLA_DK ** -0.5), gk_ref[rows, cols],
            gv_ref[rows, head * GLA_DV:(head + 1) * GLA_DV], la)


def _seq_geom(latent):
    seq = DEC_SEQ if latent else SEQ
    return seq, (DEC_BATCH if latent else BATCH), (TP // seq if latent else 0)


def _zspec(latent, off, width, per_head=True):
    seq, _, blk0 = _seq_geom(latent)
    assert off % width == 0
    if per_head:
        return pl.BlockSpec((seq, width), lambda b, h, *_: (blk0 + b, off // width + h))
    return pl.BlockSpec((seq, width), lambda b, h, *_: (blk0 + b, off // width))


def _const_spec(a):
    return pl.BlockSpec(a.shape, lambda *_, nd=a.ndim: (0,) * nd)


def _scan_geom(latent, heads=2):
    return (min(4, heads // 2), 8 if latent else 1)


def _zseg_spec(latent, off, width, reverse, per_head=True):
    seq, _, blk0 = _seq_geom(latent)
    nseg = _scan_geom(latent)[1]
    assert off % width == 0
    col0 = off // width

    def index(b, h, g=0):
        seg = (nseg - 1 - g) if reverse else g
        return ((blk0 + b) * nseg + seg, col0 + (h if per_head else 0))

    return pl.BlockSpec((seq // nseg, width), index)


def _scan_call(name, features, feat_fn, z, gate_off, gain, heads, dk, dv,
               latent, s0t, prev_out, prev_state, layer):
    seq, nb, row_blk0 = _seq_geom(latent)
    hpb, nseg = _scan_geom(latent, heads)
    consts = _scan_consts()
    in_specs = [_const_spec(a) for a in consts]
    args = list(consts)
    for reverse in ((False,) if nseg == 1 else (False, True)):
        specs, fargs = feat_fn(reverse)
        n_in = len(specs)
        in_specs += list(specs) + [_zseg_spec(latent, gate_off, hpb * dv, reverse)]
        args += list(fargs) + [z]
    in_specs.append(pl.BlockSpec((1, dv), lambda *_: (0, 0)))
    args.append(gain.reshape(1, dv))
    if latent:
        in_specs.append(pl.BlockSpec((None, None, 2, hpb, dv, dk), lambda b, h, *_: (b, layer, 0, h, 0, 0)))
        args.append(s0t)
    out_specs = [pl.BlockSpec((seq, hpb * dv), lambda b, h, *_: (row_blk0 + b, h))]
    out_shape = [jax.ShapeDtypeStruct((T, heads * dv), BF16)]
    aliases = {}
    if prev_out is not None:
        in_specs.append(pl.BlockSpec(memory_space=pl.ANY))
        aliases[len(args)] = 0
        args.append(prev_out)
    emit_state = not latent
    if emit_state:
        out_specs.append(pl.BlockSpec((None, None, 2, hpb, dk, dv), lambda b, h, *_: (b, layer, 0, h, 0, 0)))
        out_shape.append(jax.ShapeDtypeStruct((BATCH, DEPTH, 2, heads, dk, dv), F32))
        if prev_state is not None:
            in_specs.append(pl.BlockSpec(memory_space=pl.ANY))
            aliases[len(args)] = 1
            args.append(prev_state)
    n_alias = len(aliases)

    def kern(*refs):
        n_inputs = len(args) - n_alias
        keep = refs[:n_inputs] + refs[n_inputs + n_alias:]
        _scan_kernel(*keep, features=features, n_in=n_in, seq=seq, nseg=nseg, hpb=hpb, dv=dv, with_s0=latent,
                     emit_state=emit_state)

    grid = (nb, heads // hpb) + ((nseg,) if nseg > 1 else ())
    sem = ("parallel", "parallel") + (("arbitrary",) if nseg > 1 else ())
    res = pl.pallas_call(
        kern,
        grid=grid,
        in_specs=in_specs,
        out_specs=out_specs,
        out_shape=out_shape,
        input_output_aliases=aliases,
        scratch_shapes=[pltpu.VMEM((seq, hpb * dv), F32), pltpu.VMEM((2, hpb, dv, dk), F32),
                        pltpu.VMEM((2 * hpb, 3, SCAN_C, dk), F32),
                        pltpu.VMEM((2 * hpb, 2, dv // LANE, SCAN_C, LANE), F32)],
        compiler_params=_cparams(sem),
        name=name,
    )(*args)
    return (res[0], res[1]) if emit_state else (res[0], None)


def _hgrn(z, lb, gain, s0t, prev_state, layer):
    lb3 = lb.reshape(2, 1, HG_K)
    o = state = None
    for latent in (False, True):
        hpb = _scan_geom(latent, HG_HEADS)[0]

        def feat_fn(reverse, latent=latent, hpb=hpb):
            zs = functools.partial(_zseg_spec, latent, reverse=reverse)
            return ([zs(OFF_HQ, hpb * HG_DK), zs(OFF_HFF, hpb * HG_DK), zs(OFF_HFB, hpb * HG_DK),
                     zs(OFF_HI, hpb * HG_DV), pl.BlockSpec((2, 1, hpb * HG_DK), lambda b, h, *_: (0, 0, h))],
                    [z, z, z, z, lb3])

        o, st = _scan_call("hgrn_latent" if latent else "hgrn_prompt", _hgrn_features, feat_fn,
                           z, OFF_HG, gain, HG_HEADS, HG_DK, HG_DV, latent, s0t, o, prev_state, layer)
        state = st if st is not None else state
    return o, state


def _gla(z, w_gk, b_gk, gain, s0t, prev_state, layer):
    wpad = jnp.zeros((2, LANE, GLA_HEADS * GLA_DK), F32)
    wpad = wpad.at[0, 0:GLA_RANK].set(w_gk[0]).at[1, GLA_RANK:2 * GLA_RANK].set(w_gk[1]).astype(BF16)
    b3 = b_gk.reshape(2, 1, GLA_HEADS * GLA_DK)
    o = state = None
    for latent in (False, True):
        hpb = _scan_geom(latent, GLA_HEADS)[0]

        def feat_fn(reverse, latent=latent, hpb=hpb):
            zs = functools.partial(_zseg_spec, latent, reverse=reverse)
            return ([zs(OFF_GQ, hpb * GLA_DK), zs(OFF_GK, hpb * GLA_DK), zs(OFF_GV, hpb * GLA_DV),
                     zs(OFF_GLF, LANE, per_head=False),
                     pl.BlockSpec((2, LANE, hpb * GLA_DK), lambda b, h, *_: (0, 0, h)),
                     pl.BlockSpec((2, 1, hpb * GLA_DK), lambda b, h, *_: (0, 0, h))],
                    [z, z, z, z, wpad, b3])

        o, st = _scan_call("gla_latent" if latent else "gla_prompt", _gla_features, feat_fn,
                           z, OFF_GG, gain, GLA_HEADS, GLA_DK, GLA_DV, latent, s0t, o, prev_state, layer)
        state = st if st is not None else state
    return o, state


GQA = N_HEADS // N_KV_HEADS


def _ctx_attn_kernel(q_ref, k_ref, v_ref, sink_ref, o_ref):
    hd = HEAD_DIM
    k = k_ref[...].astype(BF16)
    v = v_ref[...].astype(BF16)
    heads = range(GQA)
    ss = [_dot_nt((q_ref[:, g * hd:(g + 1) * hd] * (hd ** -0.5)).astype(BF16), k) for g in heads]
    ps, dens = [], []
    for g in heads:
        sink = sink_ref[g][:, 0:1]
        m = jnp.maximum(jnp.max(ss[g], axis=-1, keepdims=True), sink)
        p = jnp.exp(ss[g] - m)
        dens.append(jnp.sum(p, axis=-1, keepdims=True) + jnp.exp(sink - m))
        ps.append(p.astype(BF16))
    for g in heads:
        o_ref[:, g * hd:(g + 1) * hd] = (_dot(ps[g], v) / dens[g]).astype(o_ref.dtype)


def _ctx_attention(z, sink3):
    hd = HEAD_DIM
    return pl.pallas_call(
        _ctx_attn_kernel,
        grid=(BATCH, N_KV_HEADS),
        in_specs=[_zspec(False, OFF_AQ, GQA * hd),
                  pl.BlockSpec((SEQ, hd), lambda b, kh: (b, OFF_AK // hd + kh)),
                  pl.BlockSpec((SEQ, hd), lambda b, kh: (b, OFF_AV // hd + kh)),
                  pl.BlockSpec((GQA, 1, LANE), lambda b, kh: (kh, 0, 0))],
        out_specs=pl.BlockSpec((SEQ, GQA * hd), lambda b, kh: (b, kh)),
        out_shape=jax.ShapeDtypeStruct((T, N_HEADS * hd), BF16),
        compiler_params=_cparams(("parallel", "parallel")),
        name="ctx_attention",
    )(z, z, z, sink3)


def _rope(x, cos_f, sin_f):
    return x * cos_f + pltpu.roll(x, HEAD_DIM // 2, 1) * sin_f


def _lat_attn_kernel(q_ref, k_ref, v_ref, ck_ref, cv_ref, cos_ref, sin_ref, sink_ref, o_ref, kr_scr, vr_scr):
    blk, hd, seq = ATT_BLOCK, HEAD_DIM, DEC_SEQ
    nb = seq // blk

    zeros = jnp.zeros((blk, hd), BF16)
    for scr in (kr_scr, vr_scr):
        scr[0:blk, :] = zeros
        scr[blk + seq:2 * blk + seq, :] = zeros

    def fill(n, carry):
        r0 = pl.multiple_of(n * blk, blk)
        rows = pl.ds(r0, blk)
        kr_scr[pl.ds(r0 + blk, blk), :] = _rope(k_ref[rows, :], cos_ref[rows, :], sin_ref[rows, :]).astype(BF16)
        vr_scr[pl.ds(r0 + blk, blk), :] = v_ref[rows, :].astype(BF16)
        return carry

    lax.fori_loop(0, nb, fill, 0)

    ck = ck_ref[...].astype(BF16)
    cv = cv_ref[...].astype(BF16)
    sink = jnp.concatenate([jnp.broadcast_to(sink_ref[g][:, 0:1], (blk, 1)) for g in range(GQA)], axis=0)
    qi = lax.broadcasted_iota(jnp.int32, (GQA * blk, 3 * blk), 0) & (blk - 1)
    kj = lax.broadcasted_iota(jnp.int32, (GQA * blk, 3 * blk), 1)
    window_bias = jnp.where(jnp.abs(kj - qi - blk) <= WINDOW, 0.0, -jnp.inf)

    per_iter = 2

    def qblocks(it, carry):
        ns = [it * per_iter + u for u in range(per_iter)]
        r0s = [pl.multiple_of(n * blk, blk) for n in ns]
        qs = []
        for r0 in r0s:
            rows = pl.ds(r0, blk)
            cos_f, sin_f = cos_ref[rows, :], sin_ref[rows, :]
            qs.append(jnp.concatenate(
                [(_rope(q_ref[rows, g * hd:(g + 1) * hd], cos_f, sin_f) * (hd ** -0.5)).astype(BF16)
                 for g in range(GQA)], axis=0))
        bands = [pl.ds(r0, 3 * blk) for r0 in r0s]
        s_locs = [_dot_nt(q, kr_scr[band, :]) for q, band in zip(qs, bands)]
        s_ctxs = [_dot_nt(q, ck) for q in qs]
        ps = []
        for n, s_loc, s_ctx in zip(ns, s_locs, s_ctxs):
            key_pos = kj + (n - 1) * blk
            s_loc = s_loc + window_bias
            s_loc = jnp.where(key_pos >= 0, s_loc, -jnp.inf)
            s_loc = jnp.where(key_pos < seq, s_loc, -jnp.inf)
            m = jnp.maximum(jnp.maximum(jnp.max(s_loc, axis=-1, keepdims=True),
                                        jnp.max(s_ctx, axis=-1, keepdims=True)), sink)
            p_loc = jnp.exp(s_loc - m)
            p_ctx = jnp.exp(s_ctx - m)
            den = (jnp.sum(p_loc, axis=-1, keepdims=True) + jnp.sum(p_ctx, axis=-1, keepdims=True)
                   + jnp.exp(sink - m))
            ps.append((p_ctx.astype(BF16), p_loc.astype(BF16), den))
        for r0, band, (p_ctx, p_loc, den) in zip(r0s, bands, ps):
            o = (_dot(p_ctx, cv) + _dot(p_loc, vr_scr[band, :])) / den
            for g in range(GQA):
                o_ref[pl.ds(r0, blk), g * hd:(g + 1) * hd] = o[g * blk:(g + 1) * blk].astype(o_ref.dtype)
        return carry

    lax.fori_loop(0, nb // per_iter, qblocks, 0)


def _rope_tables():
    n = jnp.arange(DEC_SEQ)
    row = (n // GRID_W).astype(F32)
    col = (n % GRID_W).astype(F32)
    n_freq = HEAD_DIM // 4
    inv = ROPE_THETA ** (-jnp.arange(n_freq, dtype=F32) / n_freq)
    ang = jnp.concatenate([row[:, None] * inv, col[:, None] * inv], axis=-1)
    cos, sin = jnp.cos(ang), jnp.sin(ang)
    return jnp.concatenate([cos, cos], axis=-1), jnp.concatenate([-sin, sin], axis=-1)


def _lat_attention(z, ck, cv, sink3, prev_out):
    hd = HEAD_DIM
    seq, _, blk0 = _seq_geom(True)
    cos_f, sin_f = _rope_tables()
    tab = pl.BlockSpec((seq, hd), lambda b, kh: (0, 0))
    cache = pl.BlockSpec((None, PAST_LEN, hd), lambda b, kh: (b, 0, kh))
    return pl.pallas_call(
        lambda *refs: _lat_attn_kernel(*refs[:8], *refs[9:]),
        grid=(DEC_BATCH, N_KV_HEADS),
        in_specs=[_zspec(True, OFF_AQ, GQA * hd),
                  pl.BlockSpec((seq, hd), lambda b, kh: (blk0 + b, OFF_AK // hd + kh)),
                  pl.BlockSpec((seq, hd), lambda b, kh: (blk0 + b, OFF_AV // hd + kh)),
                  cache, cache, tab, tab,
                  pl.BlockSpec((GQA, 1, LANE), lambda b, kh: (kh, 0, 0)),
                  pl.BlockSpec(memory_space=pl.ANY)],
        out_specs=pl.BlockSpec((seq, GQA * hd), lambda b, kh: (blk0 + b, kh)),
        out_shape=jax.ShapeDtypeStruct((T, N_HEADS * hd), BF16),
        input_output_aliases={8: 0},
        scratch_shapes=[pltpu.VMEM((seq + 2 * ATT_BLOCK, hd), BF16), pltpu.VMEM((seq + 2 * ATT_BLOCK, hd), BF16)],
        compiler_params=_cparams(("parallel", "parallel")),
        name="lat_attention",
    )(z, z, z, ck, cv, cos_f, sin_f, sink3, prev_out)


def kernel(x_prompt, x_sample, cache_k, cache_v, state_hgrn, state_gla, c, c_ctx, w_ada, b_ada, norm_g, w_in,
           w_out, sgu_ln_g, sgu_ln_b, sgu_w, sgu_b, hgrn_lb_logits, hgrn_norm_g, attn_sink, gla_w_gk, gla_b_gk,
           gla_norm_g, ffn_w_up, ffn_conv_w, ffn_conv_b, ffn_w_down):
    assert TP % DEC_SEQ == 0 and DEC_BATCH + 1 <= MOD_ROWS
    lb_all = jnp.cumsum(jax.nn.softmax(hgrn_lb_logits.astype(F32), axis=0), axis=0)
    lb_all = lb_all - lb_all[0:1]

    cvec = jnp.zeros((MOD_ROWS, D_MODEL), F32).at[0].set(c_ctx).at[1:1 + DEC_BATCH].set(c)
    mod = _ada_mod(cvec, w_ada, b_ada).reshape(DEPTH, MOD_ROWS, 1, 6 * D_MODEL)
    SH1, SC1, G1, SH2, SC2, G2 = range(6)

    x = [x_prompt.reshape(TP, D_MODEL), x_sample.reshape(TL, D_MODEL)]
    hg_s0t = jnp.swapaxes(state_hgrn, -1, -2)
    gl_s0t = jnp.swapaxes(state_gla, -1, -2)
    ck_all = cache_k.reshape(DEC_BATCH, DEPTH, PAST_LEN, N_KV_HEADS * HEAD_DIM)
    cv_all = cache_v.reshape(DEC_BATCH, DEPTH, PAST_LEN, N_KV_HEADS * HEAD_DIM)

    h = _norm_mod(x, norm_g[0, 0], mod, 0, SC1, SH1)
    ks_new, vs_new = [], []
    hg_state = gl_state = None
    w_in_b, w_up_b = w_in.astype(BF16), ffn_w_up
    w_out_b = w_out.astype(BF16)
    for l in range(DEPTH):
        sink3 = jnp.broadcast_to(attn_sink[l].astype(F32)[:, None, None], (N_HEADS, 1, LANE))

        z = _matmul(h, w_in_b, l, Z_COLS, F32, Z_TILE_N, "in_proj")
        o_sgu = _sgu(z, sgu_ln_g[l], sgu_ln_b[l], sgu_w[l], sgu_b[l])
        o_hg, hg_state = _hgrn(z, lb_all[l], hgrn_norm_g[l], hg_s0t, hg_state, l)
        o_att = _ctx_attention(z, sink3)
        o_att = _lat_attention(z, ck_all[:, l], cv_all[:, l], sink3, o_att)
        o_gl, gl_state = _gla(z, gla_w_gk[l], gla_b_gk[l], gla_norm_g[l], gl_s0t, gl_state, l)
        m = _out_proj([o_sgu, o_hg, o_att, o_gl], w_out_b, l)
        x, h = _resid(x, m, norm_g[l, 1], mod, l, G1, nxt=(norm_g[l, 2], l, SC2, SH2))
        ks_new.append(z[:TP, OFF_AK:OFF_AK + N_KV_HEADS * HEAD_DIM].reshape(BATCH, SEQ, N_KV_HEADS, HEAD_DIM))
        vs_new.append(z[:TP, OFF_AV:OFF_AV + N_KV_HEADS * HEAD_DIM].reshape(BATCH, SEQ, N_KV_HEADS, HEAD_DIM))

        act, w_down_b = _ffn_up(h, _conv_halo(h), w_up_b, ffn_conv_w, ffn_conv_b, ffn_w_down, l)
        f = _ffn_down(act, w_down_b, 0)
        if l + 1 < DEPTH:
            x, h = _resid(x, f, norm_g[l, 3], mod, l, G2, nxt=(norm_g[l + 1, 0], l + 1, SC1, SH1))
        else:
            x, h = _resid(x, f, norm_g[l, 3], mod, l, G2, split_out=True)

    y_prompt = x[0].reshape(BATCH, SEQ, D_MODEL)
    y_sample = x[1].reshape(DEC_BATCH, DEC_SEQ, D_MODEL)
    return (y_prompt, y_sample, jnp.stack(ks_new, axis=1), jnp.stack(vs_new, axis=1), hg_state, gl_state)
```

```python
import functools

import numpy as np
import jax
import jax.numpy as jnp
from jax import lax
from jax.experimental import pallas as pl
from jax.experimental.pallas import tpu as pltpu

D_MODEL = 4096
BATCH = 32
SEQ = 256
DEPTH = 2
DEC_BATCH = 2
DEC_SEQ = 4096
PAST_LEN = 512
GRID_W = 64
GROUP_W = D_MODEL // 4
SGU_CHUNK = 128
SGU_GROUPS = 4
SGU_DIM = GROUP_W
SGU_GCH = SGU_DIM // SGU_GROUPS
HG_HEADS = 8
HG_DK = 128
HG_DV = GROUP_W // HG_HEADS
HG_K = HG_HEADS * HG_DK
N_HEADS = 8
N_KV_HEADS = 2
HEAD_DIM = GROUP_W // N_HEADS
WINDOW = 128
ATT_BLOCK = 128
ROPE_THETA = 10000.0
GLA_HEADS = 4
GLA_DK = 128
GLA_DV = GROUP_W // GLA_HEADS
GLA_RANK = 16
GLA_NORMALIZER = 16.0
D_FF = 11008
CONV_W = 3
EPS = 1e-6

F32 = jnp.float32
BF16 = jnp.bfloat16
LOG2E = 1.4426950408889634

IN_SIZES = (2 * SGU_DIM,
            HG_K, HG_K, HG_K, HG_HEADS * HG_DV, HG_HEADS * HG_DV,
            N_HEADS * HEAD_DIM, N_KV_HEADS * HEAD_DIM, N_KV_HEADS * HEAD_DIM,
            GLA_HEADS * GLA_DK, GLA_HEADS * GLA_DK, GLA_HEADS * GLA_DV, GLA_HEADS * GLA_DV,
            GLA_RANK, GLA_RANK)
D_IN = sum(IN_SIZES)
_OFF = [0] + [int(c) for c in np.cumsum(IN_SIZES)]
(OFF_SGU, OFF_HQ, OFF_HFF, OFF_HFB, OFF_HI, OFF_HG, OFF_AQ, OFF_AK, OFF_AV,
 OFF_GQ, OFF_GK, OFF_GV, OFF_GG, OFF_GLF, OFF_GLB) = _OFF[:-1]

LANE = 128
SUBLANE = 8
VMEM_LIMIT = 56 * 1024 * 1024

Z_TILE_N = 1024
Z_COLS = -(-D_IN // Z_TILE_N) * Z_TILE_N
SCAN_C = 128
MOD_ROWS = 16

TP = BATCH * SEQ
TL = DEC_BATCH * DEC_SEQ
T = TP + TL
PIECES = ((0, TP), (TP, TL))


def _cparams(sem):
    return pltpu.CompilerParams(dimension_semantics=sem, vmem_limit_bytes=VMEM_LIMIT)


def _row_group(row0):
    return jnp.where(row0 < TP, 0, 1 + (row0 - TP) // DEC_SEQ)


def _dot(a, b):
    return jnp.dot(a, b, preferred_element_type=F32)


def _dot_nt(a, b):
    return lax.dot_general(a, b, (((1,), (1,)), ((), ())), preferred_element_type=F32)


def _dot_tn(a, b):
    return lax.dot_general(a, b, (((0,), (0,)), ((), ())), preferred_element_type=F32)


def _sigmoid(x):
    return 1.0 / (1.0 + jnp.exp(-x))


def _silu(x):
    return x * _sigmoid(x)


def _gelu(x):
    return 0.5 * x * (1.0 + lax.erf(x * (2.0 ** -0.5)))


def _rms(x, g):
    return x * lax.rsqrt(jnp.mean(x * x, axis=-1, keepdims=True) + EPS) * g


def _ada_kernel(c_ref, w_ref, b_ref, o_ref):
    c = _silu(c_ref[...]).astype(BF16)
    o_ref[...] = _dot(c, w_ref[...].astype(BF16)) + b_ref[...]


def _ada_mod(cvec, w_ada, b_ada):
    tn = 512
    n = w_ada.shape[-1]
    return pl.pallas_call(
        _ada_kernel,
        grid=(DEPTH, n // tn),
        in_specs=[pl.BlockSpec((MOD_ROWS, D_MODEL), lambda l, j: (0, 0)),
                  pl.BlockSpec((None, D_MODEL, tn), lambda l, j: (l, 0, j)),
                  pl.BlockSpec((None, 1, tn), lambda l, j: (l, 0, j))],
        out_specs=pl.BlockSpec((None, MOD_ROWS, tn), lambda l, j: (l, 0, j)),
        out_shape=jax.ShapeDtypeStruct((DEPTH, MOD_ROWS, n), F32),
        compiler_params=_cparams(("parallel", "parallel")),
        name="ada_mod",
    )(cvec, w_ada, b_ada.reshape(DEPTH, 1, n))


def _mod_spec(layer, chunk, tr, blk0=0):
    return pl.BlockSpec((None, None, 1, D_MODEL),
                        lambda i, *_: (layer, _row_group((blk0 + i) * tr), 0, chunk))


NORM_ROWS = 256
_VEC_SPEC = pl.BlockSpec((1, D_MODEL), lambda i: (0, 0))


def _row_spec(blk0=0):
    return pl.BlockSpec((NORM_ROWS, D_MODEL), lambda i: (blk0 + i, 0))


def _norm_mod_kernel(x_ref, g_ref, sc_ref, sh_ref, h_ref):
    y = _rms(x_ref[...], g_ref[...])
    h_ref[...] = (y * (1.0 + sc_ref[...]) + sh_ref[...]).astype(h_ref.dtype)


def _norm_mod(x_parts, g, mod, layer, sc_chunk, sh_chunk):
    tr = NORM_ROWS
    h = None
    for x, (row0, rows) in zip(x_parts, PIECES):
        blk0 = row0 // tr
        in_specs = [_row_spec(), _VEC_SPEC, _mod_spec(layer, sc_chunk, tr, blk0), _mod_spec(layer, sh_chunk, tr, blk0)]
        args = [x, g.reshape(1, D_MODEL), mod, mod]
        aliases = {}
        if h is not None:
            in_specs.append(pl.BlockSpec(memory_space=pl.ANY))
            aliases[len(args)] = 0
            args.append(h)
        h = pl.pallas_call(
            lambda x_ref, g_ref, sc_ref, sh_ref, *rest: _norm_mod_kernel(x_ref, g_ref, sc_ref, sh_ref, rest[-1]),
            grid=(rows // tr,),
            in_specs=in_specs,
            out_specs=_row_spec(blk0),
            out_shape=jax.ShapeDtypeStruct((T, D_MODEL), BF16),
            input_output_aliases=aliases,
            compiler_params=_cparams(("parallel",)),
            name="norm_mod",
        )(*args)
    return h


def _resid_kernel(x_ref, m_ref, ga_ref, gate_ref, *rest, emit_h):
    x = x_ref[...] + gate_ref[...] * _rms(m_ref[...].astype(F32), ga_ref[...])
    if emit_h:
        gb_ref, sc_ref, sh_ref, xo_ref, h_ref = rest
        xo_ref[...] = x
        y = _rms(x, gb_ref[...])
        h_ref[...] = (y * (1.0 + sc_ref[...]) + sh_ref[...]).astype(h_ref.dtype)
    else:
        (xo_ref,) = rest
        xo_ref[...] = x


def _resid(x, m, ga, mod, layer, gate_chunk, nxt=None, split_out=False):
    tr = NORM_ROWS
    x_parts = x if isinstance(x, (list, tuple)) else None
    pieces = PIECES if (x_parts is not None or split_out) else ((0, T),)
    emit_h = nxt is not None
    assert not (emit_h and split_out)
    x_new, h, outs = None, None, []
    for p, (row0, rows) in enumerate(pieces):
        blk0 = row0 // tr
        in_specs = [_row_spec(0 if x_parts is not None else blk0), _row_spec(blk0), _VEC_SPEC,
                    _mod_spec(layer, gate_chunk, tr, blk0)]
        args = [x_parts[p] if x_parts is not None else x, m, ga.reshape(1, D_MODEL), mod]
        if emit_h:
            gb, layer_b, sc_chunk, sh_chunk = nxt
            in_specs += [_VEC_SPEC, _mod_spec(layer_b, sc_chunk, tr, blk0), _mod_spec(layer_b, sh_chunk, tr, blk0)]
            args += [gb.reshape(1, D_MODEL), mod, mod]
        n_in = len(args)
        if split_out:
            out_specs = [_row_spec()]
            out_shape = [jax.ShapeDtypeStruct((rows, D_MODEL), F32)]
        else:
            out_specs = [_row_spec(blk0)]
            out_shape = [jax.ShapeDtypeStruct((T, D_MODEL), F32)]
        if emit_h:
            out_specs.append(_row_spec(blk0))
            out_shape.append(jax.ShapeDtypeStruct((T, D_MODEL), BF16))
        aliases = {}
        if x_new is not None and not split_out:
            for k, prev in enumerate((x_new, h) if emit_h else (x_new,)):
                in_specs.append(pl.BlockSpec(memory_space=pl.ANY))
                aliases[len(args)] = k
                args.append(prev)
        n_alias = len(aliases)

        def kern(*refs, n_in=n_in, n_alias=n_alias):
            _resid_kernel(*refs[:n_in], *refs[n_in + n_alias:], emit_h=emit_h)

        res = pl.pallas_call(
            kern,
            grid=(rows // tr,),
            in_specs=in_specs,
            out_specs=out_specs,
            out_shape=out_shape,
            input_output_aliases=aliases,
            compiler_params=_cparams(("parallel",)),
            name="resid_norm",
        )(*args)
        x_new = res[0]
        h = res[1] if emit_h else None
        outs.append(res[0])
    return (outs if split_out else x_new), h


MM_TILE_M = 1024


def _mm_kernel(a_ref, b_ref, o_ref, *, n_valid):
    acc = _dot(a_ref[...], b_ref[...])
    tn = o_ref.shape[1]
    if n_valid % tn:
        col = lax.broadcasted_iota(jnp.int32, acc.shape, 1)
        acc = jnp.where(col < n_valid - pl.program_id(1) * tn, acc, 0.0)
    o_ref[...] = acc.astype(o_ref.dtype)


def _matmul(a, w, layer, n_out, out_dtype, tn, name):
    m, k = a.shape
    tm = MM_TILE_M
    return pl.pallas_call(
        functools.partial(_mm_kernel, n_valid=w.shape[2]),
        grid=(m // tm, n_out // tn),
        in_specs=[pl.BlockSpec((tm, k), lambda i, j: (i, 0)),
                  pl.BlockSpec((None, k, tn), lambda i, j: (layer, 0, j))],
        out_specs=pl.BlockSpec((tm, tn), lambda i, j: (i, j)),
        out_shape=jax.ShapeDtypeStruct((m, n_out), out_dtype),
        compiler_params=_cparams(("parallel", "parallel")),
        name=name,
    )(a, w)


def _mm4_kernel(a0_ref, a1_ref, a2_ref, a3_ref, b_ref, o_ref):
    acc = _dot(a0_ref[...], b_ref[0 * GROUP_W:1 * GROUP_W, :])
    acc += _dot(a1_ref[...], b_ref[1 * GROUP_W:2 * GROUP_W, :])
    acc += _dot(a2_ref[...], b_ref[2 * GROUP_W:3 * GROUP_W, :])
    acc += _dot(a3_ref[...], b_ref[3 * GROUP_W:4 * GROUP_W, :])
    o_ref[...] = acc.astype(o_ref.dtype)


def _out_proj(parts, w, layer):
    tm, tn = MM_TILE_M, 1024
    a_spec = pl.BlockSpec((tm, GROUP_W), lambda i, j: (i, 0))
    return pl.pallas_call(
        _mm4_kernel,
        grid=(T // tm, D_MODEL // tn),
        in_specs=[a_spec, a_spec, a_spec, a_spec,
                  pl.BlockSpec((None, 4 * GROUP_W, tn), lambda i, j: (layer, 0, j))],
        out_specs=pl.BlockSpec((tm, tn), lambda i, j: (i, j)),
        out_shape=jax.ShapeDtypeStruct((T, D_MODEL), BF16),
        compiler_params=_cparams(("parallel", "parallel")),
        name="out_proj",
    )(*parts, w)


def _ffn_down(a, w, layer):
    m, kk = a.shape
    n = w.shape[2]
    tm, tn = 512, 512
    return pl.pallas_call(
        functools.partial(_mm_kernel, n_valid=n),
        grid=(m // tm, n // tn),
        in_specs=[pl.BlockSpec((tm, kk), lambda i, j: (i, 0)),
                  pl.BlockSpec((None, kk, tn), lambda i, j: (layer, 0, j))],
        out_specs=pl.BlockSpec((tm, tn), lambda i, j: (i, j)),
        out_shape=jax.ShapeDtypeStruct((m, n), BF16),
        compiler_params=_cparams(("parallel", "parallel")),
        name="ffn_down",
    )(a, w)


FFN_TILE_M = 2048
FFN_TILE_N = 256
FFN_SUB_M = 256
HALO_ROWS = 16
assert D_FF % FFN_TILE_N == 0


def _ffn_up_kernel(h_ref, halo_ref, wg_ref, wv_ref, cw_ref, cb_ref, wd_ref, o_ref, wdb_ref):
    wdb_ref[...] = wd_ref[...].astype(wdb_ref.dtype)
    tm, tn = o_ref.shape
    sub = FFN_SUB_M
    ns = tm // sub
    wg, wv = wg_ref[...].astype(BF16), wv_ref[...].astype(BF16)
    row0 = pl.program_id(0) * tm
    seq_len = jnp.where(row0 < TP, SEQ, DEC_SEQ)
    r = lax.broadcasted_iota(jnp.int32, (sub, tn), 0)
    g0 = _dot(jnp.concatenate([halo_ref[...], h_ref[0:sub, :]], axis=0), wg)
    gh = g0[0:HALO_ROWS]
    gs = [g0[HALO_ROWS:]] + [_dot(h_ref[s * sub:(s + 1) * sub, :], wg) for s in range(1, ns)]
    for s in range(ns):
        val = _dot(h_ref[s * sub:(s + 1) * sub, :], wv)
        g = gs[s]
        before = gh[0:1, :] if s == 0 else gs[s - 1][sub - 1:sub, :]
        after = gh[1:2, :] if s == ns - 1 else gs[s + 1][0:1, :]
        pos = (row0 + s * sub + r) & (seq_len - 1)
        prev = jnp.where(r == 0, before, pltpu.roll(g, 1, 0))
        prev = jnp.where(pos == 0, 0.0, prev)
        nxt = jnp.where(r == sub - 1, after, pltpu.roll(g, sub - 1, 0))
        nxt = jnp.where(pos == seq_len - 1, 0.0, nxt)
        conv = prev * cw_ref[0:1, :] + g * cw_ref[1:2, :] + nxt * cw_ref[2:3, :] + cb_ref[...]
        o_ref[s * sub:(s + 1) * sub, :] = (_gelu(conv) * val).astype(o_ref.dtype)


def _ffn_up(h, halo, w_up, conv_w, conv_b, w_down, layer):
    tm, tn = FFN_TILE_M, FFN_TILE_N
    nj = D_FF // tn
    steps = (T // tm) * nj
    wd_rows = D_FF // steps
    assert D_FF % steps == 0 and wd_rows % 16 == 0
    wd_in = pl.BlockSpec((None, wd_rows, D_MODEL), lambda i, j: (layer, i * nj + j, 0))
    wd_out = pl.BlockSpec((None, wd_rows, D_MODEL), lambda i, j: (0, i * nj + j, 0))
    return pl.pallas_call(
        _ffn_up_kernel,
        grid=(T // tm, nj),
        in_specs=[pl.BlockSpec((tm, D_MODEL), lambda i, j: (i, 0), pipeline_mode=pl.Buffered(1)),
                  pl.BlockSpec((None, HALO_ROWS, D_MODEL), lambda i, j: (i, 0, 0)),
                  pl.BlockSpec((None, D_MODEL, tn), lambda i, j: (layer, 0, j)),
                  pl.BlockSpec((None, D_MODEL, tn), lambda i, j: (layer, 0, nj + j)),
                  pl.BlockSpec((None, CONV_W, tn), lambda i, j: (layer, 0, j)),
                  pl.BlockSpec((None, 1, tn), lambda i, j: (layer, 0, j)),
                  wd_in],
        out_specs=[pl.BlockSpec((tm, tn), lambda i, j: (i, j)), wd_out],
        out_shape=[jax.ShapeDtypeStruct((T, D_FF), BF16), jax.ShapeDtypeStruct((1, D_FF, D_MODEL), BF16)],
        compiler_params=_cparams(("parallel", "parallel")),
        name="ffn_up",
    )(h, halo, w_up, w_up, conv_w, conv_b.reshape(DEPTH, 1, D_FF), w_down)


def _conv_halo(h):
    tm = FFN_TILE_M
    nt = T // tm
    ht = h.reshape(nt, tm, D_MODEL)
    zero = jnp.zeros((1, D_MODEL), h.dtype)
    before = jnp.concatenate([zero, ht[:-1, tm - 1]], axis=0)
    after = jnp.concatenate([ht[1:, 0], zero], axis=0)
    pad = jnp.zeros((nt, HALO_ROWS - 2, D_MODEL), h.dtype)
    return jnp.concatenate([before[:, None], after[:, None], pad], axis=1)


SGU_CHUNKS_PER_STEP = 4


def _sgu_kernel(u_ref, v_ref, g_ref, b_ref, w_ref, bias_ref, o_ref):
    c = SGU_CHUNK
    for ci in range(SGU_CHUNKS_PER_STEP):
        rows = slice(ci * c, (ci + 1) * c)
        u = _gelu(u_ref[rows, :])
        v = _gelu(v_ref[rows, :])
        vc = v - jnp.mean(v, axis=-1, keepdims=True)
        vn = vc * lax.rsqrt(jnp.mean(vc * vc, axis=-1, keepdims=True) + EPS) * g_ref[...] + b_ref[...]
        vb = vn.astype(BF16)
        for g in range(SGU_GROUPS):
            sl = slice(g * SGU_GCH, (g + 1) * SGU_GCH)
            mixed = _dot(w_ref[g], vb[:, sl]) + bias_ref[:, sl]
            o_ref[rows, sl] = (u[:, sl] * mixed).astype(o_ref.dtype)


def _sgu(z, ln_g, ln_b, w_s, b_s):
    c = SGU_CHUNK
    rows = SGU_CHUNKS_PER_STEP * c
    bias = jnp.repeat(b_s.T, SGU_GCH, axis=1)
    vec = pl.BlockSpec((1, SGU_DIM), lambda i: (0, 0))
    return pl.pallas_call(
        _sgu_kernel,
        grid=(T // rows,),
        in_specs=[pl.BlockSpec((rows, SGU_DIM), lambda i: (i, OFF_SGU // SGU_DIM)),
                  pl.BlockSpec((rows, SGU_DIM), lambda i: (i, OFF_SGU // SGU_DIM + 1)),
                  vec, vec,
                  pl.BlockSpec((SGU_GROUPS, c, c), lambda i: (0, 0, 0)),
                  pl.BlockSpec((c, SGU_DIM), lambda i: (0, 0))],
        out_specs=pl.BlockSpec((rows, SGU_DIM), lambda i: (i, 0)),
        out_shape=jax.ShapeDtypeStruct((T, SGU_DIM), BF16),
        compiler_params=_cparams(("parallel",)),
        name="sgu",
    )(z, z, ln_g.reshape(1, SGU_DIM), ln_b.reshape(1, SGU_DIM), w_s.astype(BF16), bias)


SMALL_LEVELS = tuple(m for m in (1, 2, 4) if m < SUBLANE)
BIG_LEVELS = tuple(m for m in (8, 16, 32, 64) if m < SCAN_C)


def _scan_consts():
    c = SCAN_C
    t = np.arange(c)[:, None]
    s = np.arange(c)[None, :]
    x = t ^ s
    tri = np.stack([s <= t, s >= t]).astype(np.float32)
    pair = [np.stack([x == 0, x == 0])]
    for m in SMALL_LEVELS:
        lvl = (x >= m) & (x < 2 * m)
        pair.append(np.stack([lvl & (t > s), lvl & (t < s)]))
    pair = np.stack(pair).astype(np.float32)
    rowq = np.stack([np.broadcast_to(((np.arange(c) & m) != 0)[:, None], (c, LANE)) for m in SMALL_LEVELS])
    hb = np.arange(c // 2)
    blk = np.stack([(hb[:, None] // m) == (hb[None, :] // m) for m in BIG_LEVELS])
    return (jnp.asarray(tri, BF16), jnp.asarray(pair, F32), jnp.asarray(rowq.astype(np.float32), F32),
            jnp.asarray(blk.astype(np.float32), F32))


def _split2(x):
    hi = x.astype(BF16)
    lo = (x - hi.astype(F32)).astype(BF16)
    return hi, lo


def _halves(a, m, second):
    off = m if second else 0
    return jnp.concatenate([a[j + off:j + off + m] for j in range(0, a.shape[0], 2 * m)], axis=0)


def _chunk_scans(probs, consts):
    tri_ref, pair_ref, rowq_ref, blk_ref = consts
    c, dk = probs[0][0].shape
    dirs = [1 if p[5] else 0 for p in probs]
    vbs = [p[2].astype(BF16) for p in probs]
    diag = [pair_ref[0, d] * _dot_nt(p[0].astype(BF16), p[1].astype(BF16)) for p, d in zip(probs, dirs)]
    bs = []
    for p, d in zip(probs, dirs):
        hi, lo = _split2(p[3])
        tri = tri_ref[d]
        bs.append(_dot(tri, hi) + _dot(tri, lo))
    inter = [_dot_nt((p[0] * jnp.exp2(b)).astype(BF16), p[4].astype(BF16)) for p, b in zip(probs, bs)]

    big = [[] for _ in probs]
    for li, m in enumerate(BIG_LEVELS):
        for pi, ((q, k, _, _, _, reverse), b) in enumerate(zip(probs, bs)):
            q_second = not reverse
            refs = [b[j + m:j + m + 1] if reverse else b[j + m - 1:j + m] for j in range(0, c, 2 * m)]
            ref = jnp.concatenate([jnp.broadcast_to(r, (m, dk)) for r in refs], axis=0)
            qf = _halves(q, m, q_second) * jnp.exp2(_halves(b, m, q_second) - ref)
            kf = _halves(k, m, not q_second) * jnp.exp2(ref - _halves(b, m, not q_second))
            s_l = _dot_nt(qf.astype(BF16), kf.astype(BF16))
            if 2 * m < c:
                s_l = s_l * blk_ref[li]
            big[pi].append(s_l.astype(BF16))

    scores = diag
    es = list(bs)
    for li, m in enumerate(SMALL_LEVELS):
        for pi, ((q, k, _, _, _, reverse), b) in enumerate(zip(probs, bs)):
            is_q = (rowq_ref[li] < 0.5) if reverse else (rowq_ref[li] > 0.5)
            sh_q, sh_e = (c - m, m) if reverse else (m, c - m)
            fac = jnp.exp2(jnp.where(is_q, b - pltpu.roll(es[pi], sh_q, 0), es[pi] - b))
            u = (jnp.where(is_q, q, k) * fac).astype(BF16)
            scores[pi] = scores[pi] + pair_ref[1 + li, dirs[pi]] * _dot_nt(u, u)
            if li + 1 < len(SMALL_LEVELS):
                es[pi] = jnp.where(is_q, es[pi], pltpu.roll(es[pi], sh_e, 0))

    outs = []
    for pi, ((q, k, _, _, st, reverse), b) in enumerate(zip(probs, bs)):
        o = inter[pi] + _dot(scores[pi].astype(BF16), vbs[pi])
        parts = [o[g:g + SUBLANE] for g in range(0, c, SUBLANE)]
        q_second = not reverse
        for li, m in enumerate(BIG_LEVELS):
            oc = _dot(big[pi][li], _halves(probs[pi][2], m, not q_second).astype(BF16))
            for jj, j in enumerate(range(0, c, 2 * m)):
                for g in range(0, m, SUBLANE):
                    dst = (j + (m if q_second else 0) + g) // SUBLANE
                    parts[dst] = parts[dst] + oc[jj * m + g:jj * m + g + SUBLANE]
        bl = b[0:1, :] if reverse else b[c - 1:c, :]
        kd = (k * jnp.exp2(bl - b)).astype(BF16)
        st_new = jnp.exp2(bl) * st + _dot_tn(vbs[pi], kd)
        outs.append((jnp.concatenate(parts, axis=0), st_new))
    return outs


def _scan_kernel(*refs, features, n_in, seq, nseg, hpb, dv, with_s0, emit_state):
    consts = refs[:4]
    pos = 4
    dir_refs = []
    for _ in range(1 if nseg == 1 else 2):
        dir_refs.append((refs[pos:pos + n_in], refs[pos + n_in]))
        pos += n_in + 1
    dir_refs = dir_refs * 2 if nseg == 1 else dir_refs
    gain_ref = refs[pos]
    pos += 1
    s0_ref = None
    if with_s0:
        s0_ref = refs[pos]
        pos += 1
    o_ref = refs[pos]
    pos += 1
    sf_ref = None
    if emit_state:
        sf_ref = refs[pos]
        pos += 1
    o_scr, st_scr = refs[pos:pos + 2]
    c = SCAN_C
    n = seq // c
    cps = n // nseg
    g = pl.program_id(2) if nseg > 1 else 0
    work = [(hh, reverse) for hh in range(hpb) for reverse in (False, True)]

    def init():
        st_scr[...] = s0_ref[...] if with_s0 else jnp.zeros(st_scr.shape, F32)

    def step(i, finalize):
        loc = {False: pl.multiple_of(i * c, c), True: pl.multiple_of((cps - 1 - i) * c, c)}
        pair = g * cps + i
        glob = {False: pl.multiple_of(pair * c, c), True: pl.multiple_of((n - 1 - pair) * c, c)}
        probs = [features(dir_refs[rev][0], loc[rev], hh, rev) + (st_scr[1 if rev else 0, hh], rev)
                 for hh, rev in work]
        for (hh, rev), (o, st) in zip(work, _chunk_scans(probs, consts)):
            st_scr[1 if rev else 0, hh] = st
            rows, cols = pl.ds(glob[rev], c), slice(hh * dv, (hh + 1) * dv)
            if finalize:
                gate = dir_refs[rev][1][pl.ds(loc[rev], c), cols]
                y = _rms(o + o_scr[rows, cols], gain_ref[...]) * _silu(gate)
                o_ref[rows, cols] = y.astype(o_ref.dtype)
            else:
                o_scr[rows, cols] = o

    def loop(lo, hi, finalize):
        lax.fori_loop(lo, hi, lambda i, carry: (step(i, finalize), carry)[1], 0)

    if nseg == 1:
        init()
        loop(0, n // 2, False)
        loop(n // 2, n, True)
    else:
        pl.when(g == 0)(init)
        pl.when(g < nseg // 2)(lambda: loop(0, cps, False))
        pl.when(g >= nseg // 2)(lambda: loop(0, cps, True))
    if emit_state:
        assert nseg == 1
        for d in range(2):
            for hh in range(hpb):
                sf_ref[d, hh] = st_scr[d, hh].T


def _hgrn_features(in_refs, r0, head, reverse):
    hq_ref, hf_ref, hb_ref, hi_ref, lb_ref = in_refs
    rows, cols = pl.ds(r0, SCAN_C), slice(head * HG_DK, (head + 1) * HG_DK)
    q = _silu(hq_ref[rows, cols]) * (HG_DK ** -0.5)
    lb = lb_ref[1 if reverse else 0][:, cols]
    f = lb + (1.0 - lb) * _sigmoid((hb_ref if reverse else hf_ref)[rows, cols])
    return q, 1.0 - f, hi_ref[rows, head * HG_DV:(head + 1) * HG_DV], jnp.log(f) * LOG2E


def _gla_features(in_refs, r0, head, reverse):
    gq_ref, gk_ref, gv_ref, low_ref, wgk_ref, bgk_ref = in_refs
    rows, cols = pl.ds(r0, SCAN_C), slice(head * GLA_DK, (head + 1) * GLA_DK)
    d = 1 if reverse else 0
    g = _dot(low_ref[rows, :].astype(BF16), wgk_ref[d][:, cols]) + bgk_ref[d][:, cols]
    la = (jnp.minimum(g, 0.0) - jnp.log(1.0 + jnp.exp(-jnp.abs(g)))) * (LOG2E / GLA_NORMALIZER)
    return (gq_ref[rows, cols] * (GLA_DK ** -0.5), gk_ref[rows, cols],
            gv_ref[rows, head * GLA_DV:(head + 1) * GLA_DV], la)


def _seq_geom(latent):
    seq = DEC_SEQ if latent else SEQ
    return seq, (DEC_BATCH if latent else BATCH), (TP // seq if latent else 0)


def _zspec(latent, off, width, per_head=True):
    seq, _, blk0 = _seq_geom(latent)
    assert off % width == 0
    if per_head:
        return pl.BlockSpec((seq, width), lambda b, h, *_: (blk0 + b, off // width + h))
    return pl.BlockSpec((seq, width), lambda b, h, *_: (blk0 + b, off // width))


def _const_spec(a):
    return pl.BlockSpec(a.shape, lambda *_, nd=a.ndim: (0,) * nd)


def _scan_geom(latent, heads=2):
    return (min(4, heads // 2), 8 if latent else 1)


def _zseg_spec(latent, off, width, reverse, per_head=True):
    seq, _, blk0 = _seq_geom(latent)
    nseg = _scan_geom(latent)[1]
    assert off % width == 0
    col0 = off // width

    def index(b, h, g=0):
        seg = (nseg - 1 - g) if reverse else g
        return ((blk0 + b) * nseg + seg, col0 + (h if per_head else 0))

    return pl.BlockSpec((seq // nseg, width), index)


def _scan_call(name, features, feat_fn, z, gate_off, gain, heads, dk, dv,
               latent, s0t, prev_out, prev_state, layer):
    seq, nb, row_blk0 = _seq_geom(latent)
    hpb, nseg = _scan_geom(latent, heads)
    consts = _scan_consts()
    in_specs = [_const_spec(a) for a in consts]
    args = list(consts)
    for reverse in ((False,) if nseg == 1 else (False, True)):
        specs, fargs = feat_fn(reverse)
        n_in = len(specs)
        in_specs += list(specs) + [_zseg_spec(latent, gate_off, hpb * dv, reverse)]
        args += list(fargs) + [z]
    in_specs.append(pl.BlockSpec((1, dv), lambda *_: (0, 0)))
    args.append(gain.reshape(1, dv))
    if latent:
        in_specs.append(pl.BlockSpec((None, None, 2, hpb, dv, dk), lambda b, h, *_: (b, layer, 0, h, 0, 0)))
        args.append(s0t)
    out_specs = [pl.BlockSpec((seq, hpb * dv), lambda b, h, *_: (row_blk0 + b, h))]
    out_shape = [jax.ShapeDtypeStruct((T, heads * dv), BF16)]
    aliases = {}
    if prev_out is not None:
        in_specs.append(pl.BlockSpec(memory_space=pl.ANY))
        aliases[len(args)] = 0
        args.append(prev_out)
    emit_state = not latent
    if emit_state:
        out_specs.append(pl.BlockSpec((None, None, 2, hpb, dk, dv), lambda b, h, *_: (b, layer, 0, h, 0, 0)))
        out_shape.append(jax.ShapeDtypeStruct((BATCH, DEPTH, 2, heads, dk, dv), F32))
        if prev_state is not None:
            in_specs.append(pl.BlockSpec(memory_space=pl.ANY))
            aliases[len(args)] = 1
            args.append(prev_state)
    n_alias = len(aliases)

    def kern(*refs):
        n_inputs = len(args) - n_alias
        keep = refs[:n_inputs] + refs[n_inputs + n_alias:]
        _scan_kernel(*keep, features=features, n_in=n_in, seq=seq, nseg=nseg, hpb=hpb, dv=dv, with_s0=latent,
                     emit_state=emit_state)

    grid = (nb, heads // hpb) + ((nseg,) if nseg > 1 else ())
    sem = ("parallel", "parallel") + (("arbitrary",) if nseg > 1 else ())
    res = pl.pallas_call(
        kern,
        grid=grid,
        in_specs=in_specs,
        out_specs=out_specs,
        out_shape=out_shape,
        input_output_aliases=aliases,
        scratch_shapes=[pltpu.VMEM((seq, hpb * dv), F32), pltpu.VMEM((2, hpb, dv, dk), F32)],
        compiler_params=_cparams(sem),
        name=name,
    )(*args)
    return (res[0], res[1]) if emit_state else (res[0], None)


def _hgrn(z, lb, gain, s0t, prev_state, layer):
    lb3 = lb.reshape(2, 1, HG_K)
    o = state = None
    for latent in (False, True):
        hpb = _scan_geom(latent, HG_HEADS)[0]

        def feat_fn(reverse, latent=latent, hpb=hpb):
            zs = functools.partial(_zseg_spec, latent, reverse=reverse)
            return ([zs(OFF_HQ, hpb * HG_DK), zs(OFF_HFF, hpb * HG_DK), zs(OFF_HFB, hpb * HG_DK),
                     zs(OFF_HI, hpb * HG_DV), pl.BlockSpec((2, 1, hpb * HG_DK), lambda b, h, *_: (0, 0, h))],
                    [z, z, z, z, lb3])

        o, st = _scan_call("hgrn_latent" if latent else "hgrn_prompt", _hgrn_features, feat_fn,
                           z, OFF_HG, gain, HG_HEADS, HG_DK, HG_DV, latent, s0t, o, prev_state, layer)
        state = st if st is not None else state
    return o, state


def _gla(z, w_gk, b_gk, gain, s0t, prev_state, layer):
    wpad = jnp.zeros((2, LANE, GLA_HEADS * GLA_DK), F32)
    wpad = wpad.at[0, 0:GLA_RANK].set(w_gk[0]).at[1, GLA_RANK:2 * GLA_RANK].set(w_gk[1]).astype(BF16)
    b3 = b_gk.reshape(2, 1, GLA_HEADS * GLA_DK)
    o = state = None
    for latent in (False, True):
        hpb = _scan_geom(latent, GLA_HEADS)[0]

        def feat_fn(reverse, latent=latent, hpb=hpb):
            zs = functools.partial(_zseg_spec, latent, reverse=reverse)
            return ([zs(OFF_GQ, hpb * GLA_DK), zs(OFF_GK, hpb * GLA_DK), zs(OFF_GV, hpb * GLA_DV),
                     zs(OFF_GLF, LANE, per_head=False),
                     pl.BlockSpec((2, LANE, hpb * GLA_DK), lambda b, h, *_: (0, 0, h)),
                     pl.BlockSpec((2, 1, hpb * GLA_DK), lambda b, h, *_: (0, 0, h))],
                    [z, z, z, z, wpad, b3])

        o, st = _scan_call("gla_latent" if latent else "gla_prompt", _gla_features, feat_fn,
                           z, OFF_GG, gain, GLA_HEADS, GLA_DK, GLA_DV, latent, s0t, o, prev_state, layer)
        state = st if st is not None else state
    return o, state


GQA = N_HEADS // N_KV_HEADS


def _ctx_attn_kernel(q_ref, k_ref, v_ref, sink_ref, o_ref):
    hd = HEAD_DIM
    ks = [k_ref[:, kh * hd:(kh + 1) * hd].astype(BF16) for kh in range(N_KV_HEADS)]
    vs = [v_ref[:, kh * hd:(kh + 1) * hd].astype(BF16) for kh in range(N_KV_HEADS)]
    heads = range(N_HEADS)
    ss = [_dot_nt((q_ref[:, h * hd:(h + 1) * hd] * (hd ** -0.5)).astype(BF16), ks[h // GQA]) for h in heads]
    ps, dens = [], []
    for h in heads:
        sink = sink_ref[h][:, 0:1]
        m = jnp.maximum(jnp.max(ss[h], axis=-1, keepdims=True), sink)
        p = jnp.exp(ss[h] - m)
        dens.append(jnp.sum(p, axis=-1, keepdims=True) + jnp.exp(sink - m))
        ps.append(p.astype(BF16))
    for h in heads:
        o_ref[:, h * hd:(h + 1) * hd] = (_dot(ps[h], vs[h // GQA]) / dens[h]).astype(o_ref.dtype)


def _ctx_attention(z, sink3):
    hd = HEAD_DIM
    kv_w = N_KV_HEADS * hd
    assert OFF_AK % kv_w == 0 and OFF_AV % kv_w == 0
    return pl.pallas_call(
        _ctx_attn_kernel,
        grid=(BATCH,),
        in_specs=[pl.BlockSpec((SEQ, N_HEADS * hd), lambda b: (b, OFF_AQ // (N_HEADS * hd))),
                  pl.BlockSpec((SEQ, kv_w), lambda b: (b, OFF_AK // kv_w)),
                  pl.BlockSpec((SEQ, kv_w), lambda b: (b, OFF_AV // kv_w)),
                  pl.BlockSpec((N_HEADS, 1, LANE), lambda b: (0, 0, 0))],
        out_specs=pl.BlockSpec((SEQ, N_HEADS * hd), lambda b: (b, 0)),
        out_shape=jax.ShapeDtypeStruct((T, N_HEADS * hd), BF16),
        compiler_params=_cparams(("parallel",)),
        name="ctx_attention",
    )(z, z, z, sink3)


def _rope(x, cos_f, sin_f):
    return x * cos_f + pltpu.roll(x, HEAD_DIM // 2, 1) * sin_f


def _lat_attn_kernel(q_ref, k_ref, v_ref, ck_ref, cv_ref, cos_ref, sin_ref, sink_ref, o_ref, kr_scr, vr_scr):
    blk, hd, seq = ATT_BLOCK, HEAD_DIM, DEC_SEQ
    nb = seq // blk

    zeros = jnp.zeros((blk, hd), BF16)
    for scr in (kr_scr, vr_scr):
        scr[0:blk, :] = zeros
        scr[blk + seq:2 * blk + seq, :] = zeros

    def fill(n, carry):
        r0 = pl.multiple_of(n * blk, blk)
        rows = pl.ds(r0, blk)
        kr_scr[pl.ds(r0 + blk, blk), :] = _rope(k_ref[rows, :], cos_ref[rows, :], sin_ref[rows, :]).astype(BF16)
        vr_scr[pl.ds(r0 + blk, blk), :] = v_ref[rows, :].astype(BF16)
        return carry

    lax.fori_loop(0, nb, fill, 0)

    ck = ck_ref[...].astype(BF16)
    cv = cv_ref[...].astype(BF16)
    sink = jnp.concatenate([jnp.broadcast_to(sink_ref[g][:, 0:1], (blk, 1)) for g in range(GQA)], axis=0)
    qi = lax.broadcasted_iota(jnp.int32, (GQA * blk, 3 * blk), 0) & (blk - 1)
    kj = lax.broadcasted_iota(jnp.int32, (GQA * blk, 3 * blk), 1)
    window_bias = jnp.where(jnp.abs(kj - qi - blk) <= WINDOW, 0.0, -jnp.inf)

    per_iter = 4

    def qblocks(it, carry):
        ns = [it * per_iter + u for u in range(per_iter)]
        r0s = [pl.multiple_of(n * blk, blk) for n in ns]
        qs = []
        for r0 in r0s:
            rows = pl.ds(r0, blk)
            cos_f, sin_f = cos_ref[rows, :], sin_ref[rows, :]
            qs.append(jnp.concatenate(
                [(_rope(q_ref[rows, g * hd:(g + 1) * hd], cos_f, sin_f) * (hd ** -0.5)).astype(BF16)
                 for g in range(GQA)], axis=0))
        bands = [pl.ds(r0, 3 * blk) for r0 in r0s]
        s_locs = [_dot_nt(q, kr_scr[band, :]) for q, band in zip(qs, bands)]
        s_ctxs = [_dot_nt(q, ck) for q in qs]
        ps = []
        for n, s_loc, s_ctx in zip(ns, s_locs, s_ctxs):
            key_pos = kj + (n - 1) * blk
            s_loc = s_loc + window_bias
            s_loc = jnp.where(key_pos >= 0, s_loc, -jnp.inf)
            s_loc = jnp.where(key_pos < seq, s_loc, -jnp.inf)
            m = jnp.maximum(jnp.maximum(jnp.max(s_loc, axis=-1, keepdims=True),
                                        jnp.max(s_ctx, axis=-1, keepdims=True)), sink)
            p_loc = jnp.exp(s_loc - m)
            p_ctx = jnp.exp(s_ctx - m)
            den = (jnp.sum(p_loc, axis=-1, keepdims=True) + jnp.sum(p_ctx, axis=-1, keepdims=True)
                   + jnp.exp(sink - m))
            ps.append((p_ctx.astype(BF16), p_loc.astype(BF16), den))
        for r0, band, (p_ctx, p_loc, den) in zip(r0s, bands, ps):
            o = (_dot(p_ctx, cv) + _dot(p_loc, vr_scr[band, :])) / den
            for g in range(GQA):
                o_ref[pl.ds(r0, blk), g * hd:(g + 1) * hd] = o[g * blk:(g + 1) * blk].astype(o_ref.dtype)
        return carry

    lax.fori_loop(0, nb // per_iter, qblocks, 0)


def _rope_tables():
    n = jnp.arange(DEC_SEQ)
    row = (n // GRID_W).astype(F32)
    col = (n % GRID_W).astype(F32)
    n_freq = HEAD_DIM // 4
    inv = ROPE_THETA ** (-jnp.arange(n_freq, dtype=F32) / n_freq)
    ang = jnp.concatenate([row[:, None] * inv, col[:, None] * inv], axis=-1)
    cos, sin = jnp.cos(ang), jnp.sin(ang)
    return jnp.concatenate([cos, cos], axis=-1), jnp.concatenate([-sin, sin], axis=-1)


def _lat_attention(z, ck, cv, sink3, prev_out):
    hd = HEAD_DIM
    seq, _, blk0 = _seq_geom(True)
    cos_f, sin_f = _rope_tables()
    tab = pl.BlockSpec((seq, hd), lambda b, kh: (0, 0))
    cache = pl.BlockSpec((None, PAST_LEN, hd), lambda b, kh: (b, 0, kh))
    return pl.pallas_call(
        lambda *refs: _lat_attn_kernel(*refs[:8], *refs[9:]),
        grid=(DEC_BATCH, N_KV_HEADS),
        in_specs=[_zspec(True, OFF_AQ, GQA * hd),
                  pl.BlockSpec((seq, hd), lambda b, kh: (blk0 + b, OFF_AK // hd + kh)),
                  pl.BlockSpec((seq, hd), lambda b, kh: (blk0 + b, OFF_AV // hd + kh)),
                  cache, cache, tab, tab,
                  pl.BlockSpec((GQA, 1, LANE), lambda b, kh: (kh, 0, 0)),
                  pl.BlockSpec(memory_space=pl.ANY)],
        out_specs=pl.BlockSpec((seq, GQA * hd), lambda b, kh: (blk0 + b, kh)),
        out_shape=jax.ShapeDtypeStruct((T, N_HEADS * hd), BF16),
        input_output_aliases={8: 0},
        scratch_shapes=[pltpu.VMEM((seq + 2 * ATT_BLOCK, hd), BF16), pltpu.VMEM((seq + 2 * ATT_BLOCK, hd), BF16)],
        compiler_params=_cparams(("parallel", "parallel")),
        name="lat_attention",
    )(z, z, z, ck, cv, cos_f, sin_f, sink3, prev_out)


def kernel(x_prompt, x_sample, cache_k, cache_v, state_hgrn, state_gla, c, c_ctx, w_ada, b_ada, norm_g, w_in,
           w_out, sgu_ln_g, sgu_ln_b, sgu_w, sgu_b, hgrn_lb_logits, hgrn_norm_g, attn_sink, gla_w_gk, gla_b_gk,
           gla_norm_g, ffn_w_up, ffn_conv_w, ffn_conv_b, ffn_w_down):
    assert TP % DEC_SEQ == 0 and DEC_BATCH + 1 <= MOD_ROWS
    lb_all = jnp.cumsum(jax.nn.softmax(hgrn_lb_logits.astype(F32), axis=0), axis=0)
    lb_all = lb_all - lb_all[0:1]

    cvec = jnp.zeros((MOD_ROWS, D_MODEL), F32).at[0].set(c_ctx).at[1:1 + DEC_BATCH].set(c)
    mod = _ada_mod(cvec, w_ada, b_ada).reshape(DEPTH, MOD_ROWS, 1, 6 * D_MODEL)
    SH1, SC1, G1, SH2, SC2, G2 = range(6)

    x = [x_prompt.reshape(TP, D_MODEL), x_sample.reshape(TL, D_MODEL)]
    hg_s0t = jnp.swapaxes(state_hgrn, -1, -2)
    gl_s0t = jnp.swapaxes(state_gla, -1, -2)
    ck_all = cache_k.reshape(DEC_BATCH, DEPTH, PAST_LEN, N_KV_HEADS * HEAD_DIM)
    cv_all = cache_v.reshape(DEC_BATCH, DEPTH, PAST_LEN, N_KV_HEADS * HEAD_DIM)

    h = _norm_mod(x, norm_g[0, 0], mod, 0, SC1, SH1)
    ks_new, vs_new = [], []
    hg_state = gl_state = None
    w_in_b, w_up_b = w_in.astype(BF16), ffn_w_up
    w_out_b = w_out.astype(BF16)
    for l in range(DEPTH):
        sink3 = jnp.broadcast_to(attn_sink[l].astype(F32)[:, None, None], (N_HEADS, 1, LANE))

        z = _matmul(h, w_in_b, l, Z_COLS, F32, Z_TILE_N, "in_proj")
        o_sgu = _sgu(z, sgu_ln_g[l], sgu_ln_b[l], sgu_w[l], sgu_b[l])
        o_hg, hg_state = _hgrn(z, lb_all[l], hgrn_norm_g[l], hg_s0t, hg_state, l)
        o_att = _ctx_attention(z, sink3)
        o_att = _lat_attention(z, ck_all[:, l], cv_all[:, l], sink3, o_att)
        o_gl, gl_state = _gla(z, gla_w_gk[l], gla_b_gk[l], gla_norm_g[l], gl_s0t, gl_state, l)
        m = _out_proj([o_sgu, o_hg, o_att, o_gl], w_out_b, l)
        x, h = _resid(x, m, norm_g[l, 1], mod, l, G1, nxt=(norm_g[l, 2], l, SC2, SH2))
        ks_new.append(z[:TP, OFF_AK:OFF_AK + N_KV_HEADS * HEAD_DIM].reshape(BATCH, SEQ, N_KV_HEADS, HEAD_DIM))
        vs_new.append(z[:TP, OFF_AV:OFF_AV + N_KV_HEADS * HEAD_DIM].reshape(BATCH, SEQ, N_KV_HEADS, HEAD_DIM))

        act, w_down_b = _ffn_up(h, _conv_halo(h), w_up_b, ffn_conv_w, ffn_conv_b, ffn_w_down, l)
        f = _ffn_down(act, w_down_b, 0)
        if l + 1 < DEPTH:
            x, h = _resid(x, f, norm_g[l, 3], mod, l, G2, nxt=(norm_g[l + 1, 0], l + 1, SC1, SH1))
        else:
            x, h = _resid(x, f, norm_g[l, 3], mod, l, G2, split_out=True)

    y_prompt = x[0].reshape(BATCH, SEQ, D_MODEL)
    y_sample = x[1].reshape(DEC_BATCH, DEC_SEQ, D_MODEL)
    return (y_prompt, y_sample, jnp.stack(ks_new, axis=1), jnp.stack(vs_new, axis=1), hg_state, gl_state)
```

```python
import functools

import numpy as np
import jax
import jax.numpy as jnp
from jax import lax
from jax.experimental import pallas as pl
from jax.experimental.pallas import tpu as pltpu

D_MODEL = 4096
BATCH = 32
SEQ = 256
DEPTH = 2
DEC_BATCH = 2
DEC_SEQ = 4096
PAST_LEN = 512
GRID_W = 64
GROUP_W = D_MODEL // 4
SGU_CHUNK = 128
SGU_GROUPS = 4
SGU_DIM = GROUP_W
SGU_GCH = SGU_DIM // SGU_GROUPS
HG_HEADS = 8
HG_DK = 128
HG_DV = GROUP_W // HG_HEADS
HG_K = HG_HEADS * HG_DK
N_HEADS = 8
N_KV_HEADS = 2
HEAD_DIM = GROUP_W // N_HEADS
WINDOW = 128
ATT_BLOCK = 128
ROPE_THETA = 10000.0
GLA_HEADS = 4
GLA_DK = 128
GLA_DV = GROUP_W // GLA_HEADS
GLA_RANK = 16
GLA_NORMALIZER = 16.0
D_FF = 11008
CONV_W = 3
EPS = 1e-6

F32 = jnp.float32
BF16 = jnp.bfloat16
LOG2E = 1.4426950408889634

IN_SIZES = (2 * SGU_DIM,
            HG_K, HG_K, HG_K, HG_HEADS * HG_DV, HG_HEADS * HG_DV,
            N_HEADS * HEAD_DIM, N_KV_HEADS * HEAD_DIM, N_KV_HEADS * HEAD_DIM,
            GLA_HEADS * GLA_DK, GLA_HEADS * GLA_DK, GLA_HEADS * GLA_DV, GLA_HEADS * GLA_DV,
            GLA_RANK, GLA_RANK)
D_IN = sum(IN_SIZES)
_OFF = [0] + [int(c) for c in np.cumsum(IN_SIZES)]
(OFF_SGU, OFF_HQ, OFF_HFF, OFF_HFB, OFF_HI, OFF_HG, OFF_AQ, OFF_AK, OFF_AV,
 OFF_GQ, OFF_GK, OFF_GV, OFF_GG, OFF_GLF, OFF_GLB) = _OFF[:-1]

LANE = 128
SUBLANE = 8
VMEM_LIMIT = 56 * 1024 * 1024

Z_TILE_N = 1024
Z_COLS = -(-D_IN // Z_TILE_N) * Z_TILE_N
SCAN_C = 128
MOD_ROWS = 16

TP = BATCH * SEQ
TL = DEC_BATCH * DEC_SEQ
T = TP + TL
PIECES = ((0, TP), (TP, TL))


def _cparams(sem):
    return pltpu.CompilerParams(dimension_semantics=sem, vmem_limit_bytes=VMEM_LIMIT)


def _row_group(row0):
    return jnp.where(row0 < TP, 0, 1 + (row0 - TP) // DEC_SEQ)


def _dot(a, b):
    return jnp.dot(a, b, preferred_element_type=F32)


def _dot_nt(a, b):
    return lax.dot_general(a, b, (((1,), (1,)), ((), ())), preferred_element_type=F32)


def _dot_tn(a, b):
    return lax.dot_general(a, b, (((0,), (0,)), ((), ())), preferred_element_type=F32)


def _sigmoid(x):
    return 1.0 / (1.0 + jnp.exp(-x))


def _silu(x):
    return x * _sigmoid(x)


def _gelu(x):
    return 0.5 * x * (1.0 + lax.erf(x * (2.0 ** -0.5)))


def _rms(x, g):
    return x * lax.rsqrt(jnp.mean(x * x, axis=-1, keepdims=True) + EPS) * g


def _ada_kernel(c_ref, w_ref, b_ref, o_ref):
    c = _silu(c_ref[...]).astype(BF16)
    o_ref[...] = _dot(c, w_ref[...].astype(BF16)) + b_ref[...]


def _ada_mod(cvec, w_ada, b_ada):
    tn = 512
    n = w_ada.shape[-1]
    return pl.pallas_call(
        _ada_kernel,
        grid=(DEPTH, n // tn),
        in_specs=[pl.BlockSpec((MOD_ROWS, D_MODEL), lambda l, j: (0, 0)),
                  pl.BlockSpec((None, D_MODEL, tn), lambda l, j: (l, 0, j)),
                  pl.BlockSpec((None, 1, tn), lambda l, j: (l, 0, j))],
        out_specs=pl.BlockSpec((None, MOD_ROWS, tn), lambda l, j: (l, 0, j)),
        out_shape=jax.ShapeDtypeStruct((DEPTH, MOD_ROWS, n), F32),
        compiler_params=_cparams(("parallel", "parallel")),
        name="ada_mod",
    )(cvec, w_ada, b_ada.reshape(DEPTH, 1, n))


def _mod_spec(layer, chunk, tr, blk0=0):
    return pl.BlockSpec((None, None, 1, D_MODEL),
                        lambda i, *_: (layer, _row_group((blk0 + i) * tr), 0, chunk))


NORM_ROWS = 256
_VEC_SPEC = pl.BlockSpec((1, D_MODEL), lambda i: (0, 0))


def _row_spec(blk0=0):
    return pl.BlockSpec((NORM_ROWS, D_MODEL), lambda i: (blk0 + i, 0))


def _norm_mod_kernel(x_ref, g_ref, sc_ref, sh_ref, h_ref):
    y = _rms(x_ref[...], g_ref[...])
    h_ref[...] = (y * (1.0 + sc_ref[...]) + sh_ref[...]).astype(h_ref.dtype)


def _norm_mod(x_parts, g, mod, layer, sc_chunk, sh_chunk):
    tr = NORM_ROWS
    h = None
    for x, (row0, rows) in zip(x_parts, PIECES):
        blk0 = row0 // tr
        in_specs = [_row_spec(), _VEC_SPEC, _mod_spec(layer, sc_chunk, tr, blk0), _mod_spec(layer, sh_chunk, tr, blk0)]
        args = [x, g.reshape(1, D_MODEL), mod, mod]
        aliases = {}
        if h is not None:
            in_specs.append(pl.BlockSpec(memory_space=pl.ANY))
            aliases[len(args)] = 0
            args.append(h)
        h = pl.pallas_call(
            lambda x_ref, g_ref, sc_ref, sh_ref, *rest: _norm_mod_kernel(x_ref, g_ref, sc_ref, sh_ref, rest[-1]),
            grid=(rows // tr,),
            in_specs=in_specs,
            out_specs=_row_spec(blk0),
            out_shape=jax.ShapeDtypeStruct((T, D_MODEL), BF16),
            input_output_aliases=aliases,
            compiler_params=_cparams(("parallel",)),
            name="norm_mod",
        )(*args)
    return h


def _resid_kernel(x_ref, m_ref, ga_ref, gate_ref, *rest, emit_h):
    x = x_ref[...] + gate_ref[...] * _rms(m_ref[...].astype(F32), ga_ref[...])
    if emit_h:
        gb_ref, sc_ref, sh_ref, xo_ref, h_ref = rest
        xo_ref[...] = x
        y = _rms(x, gb_ref[...])
        h_ref[...] = (y * (1.0 + sc_ref[...]) + sh_ref[...]).astype(h_ref.dtype)
    else:
        (xo_ref,) = rest
        xo_ref[...] = x


def _resid(x, m, ga, mod, layer, gate_chunk, nxt=None, split_out=False):
    tr = NORM_ROWS
    x_parts = x if isinstance(x, (list, tuple)) else None
    pieces = PIECES if (x_parts is not None or split_out) else ((0, T),)
    emit_h = nxt is not None
    assert not (emit_h and split_out)
    x_new, h, outs = None, None, []
    for p, (row0, rows) in enumerate(pieces):
        blk0 = row0 // tr
        in_specs = [_row_spec(0 if x_parts is not None else blk0), _row_spec(blk0), _VEC_SPEC,
                    _mod_spec(layer, gate_chunk, tr, blk0)]
        args = [x_parts[p] if x_parts is not None else x, m, ga.reshape(1, D_MODEL), mod]
        if emit_h:
            gb, layer_b, sc_chunk, sh_chunk = nxt
            in_specs += [_VEC_SPEC, _mod_spec(layer_b, sc_chunk, tr, blk0), _mod_spec(layer_b, sh_chunk, tr, blk0)]
            args += [gb.reshape(1, D_MODEL), mod, mod]
        n_in = len(args)
        if split_out:
            out_specs = [_row_spec()]
            out_shape = [jax.ShapeDtypeStruct((rows, D_MODEL), F32)]
        else:
            out_specs = [_row_spec(blk0)]
            out_shape = [jax.ShapeDtypeStruct((T, D_MODEL), F32)]
        if emit_h:
            out_specs.append(_row_spec(blk0))
            out_shape.append(jax.ShapeDtypeStruct((T, D_MODEL), BF16))
        aliases = {}
        if x_new is not None and not split_out:
            for k, prev in enumerate((x_new, h) if emit_h else (x_new,)):
                in_specs.append(pl.BlockSpec(memory_space=pl.ANY))
                aliases[len(args)] = k
                args.append(prev)
        n_alias = len(aliases)

        def kern(*refs, n_in=n_in, n_alias=n_alias):
            _resid_kernel(*refs[:n_in], *refs[n_in + n_alias:], emit_h=emit_h)

        res = pl.pallas_call(
            kern,
            grid=(rows // tr,),
            in_specs=in_specs,
            out_specs=out_specs,
            out_shape=out_shape,
            input_output_aliases=aliases,
            compiler_params=_cparams(("parallel",)),
            name="resid_norm",
        )(*args)
        x_new = res[0]
        h = res[1] if emit_h else None
        outs.append(res[0])
    return (outs if split_out else x_new), h


MM_TILE_M = 1024


def _mm_kernel(a_ref, b_ref, o_ref, *, n_valid):
    acc = _dot(a_ref[...], b_ref[...])
    tn = o_ref.shape[1]
    if n_valid % tn:
        col = lax.broadcasted_iota(jnp.int32, acc.shape, 1)
        acc = jnp.where(col < n_valid - pl.program_id(1) * tn, acc, 0.0)
    o_ref[...] = acc.astype(o_ref.dtype)


def _matmul(a, w, layer, n_out, out_dtype, tn, name):
    m, k = a.shape
    tm = MM_TILE_M
    return pl.pallas_call(
        functools.partial(_mm_kernel, n_valid=w.shape[2]),
        grid=(m // tm, n_out // tn),
        in_specs=[pl.BlockSpec((tm, k), lambda i, j: (i, 0)),
                  pl.BlockSpec((None, k, tn), lambda i, j: (layer, 0, j))],
        out_specs=pl.BlockSpec((tm, tn), lambda i, j: (i, j)),
        out_shape=jax.ShapeDtypeStruct((m, n_out), out_dtype),
        compiler_params=_cparams(("parallel", "parallel")),
        name=name,
    )(a, w)


def _mm4_kernel(a0_ref, a1_ref, a2_ref, a3_ref, b_ref, o_ref):
    acc = _dot(a0_ref[...], b_ref[0 * GROUP_W:1 * GROUP_W, :])
    acc += _dot(a1_ref[...], b_ref[1 * GROUP_W:2 * GROUP_W, :])
    acc += _dot(a2_ref[...], b_ref[2 * GROUP_W:3 * GROUP_W, :])
    acc += _dot(a3_ref[...], b_ref[3 * GROUP_W:4 * GROUP_W, :])
    o_ref[...] = acc.astype(o_ref.dtype)


def _out_proj(parts, w, layer):
    tm, tn = MM_TILE_M, 1024
    a_spec = pl.BlockSpec((tm, GROUP_W), lambda i, j: (i, 0))
    return pl.pallas_call(
        _mm4_kernel,
        grid=(T // tm, D_MODEL // tn),
        in_specs=[a_spec, a_spec, a_spec, a_spec,
                  pl.BlockSpec((None, 4 * GROUP_W, tn), lambda i, j: (layer, 0, j))],
        out_specs=pl.BlockSpec((tm, tn), lambda i, j: (i, j)),
        out_shape=jax.ShapeDtypeStruct((T, D_MODEL), BF16),
        compiler_params=_cparams(("parallel", "parallel")),
        name="out_proj",
    )(*parts, w)


def _ffn_down(a, w, layer):
    m, kk = a.shape
    n = w.shape[2]
    tm, tn = 512, 512
    return pl.pallas_call(
        functools.partial(_mm_kernel, n_valid=n),
        grid=(m // tm, n // tn),
        in_specs=[pl.BlockSpec((tm, kk), lambda i, j: (i, 0)),
                  pl.BlockSpec((None, kk, tn), lambda i, j: (layer, 0, j))],
        out_specs=pl.BlockSpec((tm, tn), lambda i, j: (i, j)),
        out_shape=jax.ShapeDtypeStruct((m, n), BF16),
        compiler_params=_cparams(("parallel", "parallel")),
        name="ffn_down",
    )(a, w)


FFN_TILE_M = 2048
FFN_TILE_N = 256
FFN_SUB_M = 256
HALO_ROWS = 16
assert D_FF % FFN_TILE_N == 0


def _ffn_up_kernel(h_ref, halo_ref, wg_ref, wv_ref, cw_ref, cb_ref, wd_ref, o_ref, wdb_ref):
    wdb_ref[...] = wd_ref[...].astype(wdb_ref.dtype)
    tm, tn = o_ref.shape
    sub = FFN_SUB_M
    ns = tm // sub
    wg, wv = wg_ref[...].astype(BF16), wv_ref[...].astype(BF16)
    row0 = pl.program_id(0) * tm
    seq_len = jnp.where(row0 < TP, SEQ, DEC_SEQ)
    r = lax.broadcasted_iota(jnp.int32, (sub, tn), 0)
    g0 = _dot(jnp.concatenate([halo_ref[...], h_ref[0:sub, :]], axis=0), wg)
    gh = g0[0:HALO_ROWS]
    gs = [g0[HALO_ROWS:]] + [_dot(h_ref[s * sub:(s + 1) * sub, :], wg) for s in range(1, ns)]
    for s in range(ns):
        val = _dot(h_ref[s * sub:(s + 1) * sub, :], wv)
        g = gs[s]
        before = gh[0:1, :] if s == 0 else gs[s - 1][sub - 1:sub, :]
        after = gh[1:2, :] if s == ns - 1 else gs[s + 1][0:1, :]
        pos = (row0 + s * sub + r) & (seq_len - 1)
        prev = jnp.where(r == 0, before, pltpu.roll(g, 1, 0))
        prev = jnp.where(pos == 0, 0.0, prev)
        nxt = jnp.where(r == sub - 1, after, pltpu.roll(g, sub - 1, 0))
        nxt = jnp.where(pos == seq_len - 1, 0.0, nxt)
        conv = prev * cw_ref[0:1, :] + g * cw_ref[1:2, :] + nxt * cw_ref[2:3, :] + cb_ref[...]
        o_ref[s * sub:(s + 1) * sub, :] = (_gelu(conv) * val).astype(o_ref.dtype)


def _ffn_up(h, halo, w_up, conv_w, conv_b, w_down, layer):
    tm, tn = FFN_TILE_M, FFN_TILE_N
    nj = D_FF // tn
    steps = (T // tm) * nj
    wd_rows = D_FF // steps
    assert D_FF % steps == 0 and wd_rows % 16 == 0
    wd_in = pl.BlockSpec((None, wd_rows, D_MODEL), lambda i, j: (layer, i * nj + j, 0))
    wd_out = pl.BlockSpec((None, wd_rows, D_MODEL), lambda i, j: (0, i * nj + j, 0))
    return pl.pallas_call(
        _ffn_up_kernel,
        grid=(T // tm, nj),
        in_specs=[pl.BlockSpec((tm, D_MODEL), lambda i, j: (i, 0), pipeline_mode=pl.Buffered(1)),
                  pl.BlockSpec((None, HALO_ROWS, D_MODEL), lambda i, j: (i, 0, 0)),
                  pl.BlockSpec((None, D_MODEL, tn), lambda i, j: (layer, 0, j)),
                  pl.BlockSpec((None, D_MODEL, tn), lambda i, j: (layer, 0, nj + j)),
                  pl.BlockSpec((None, CONV_W, tn), lambda i, j: (layer, 0, j)),
                  pl.BlockSpec((None, 1, tn), lambda i, j: (layer, 0, j)),
                  wd_in],
        out_specs=[pl.BlockSpec((tm, tn), lambda i, j: (i, j)), wd_out],
        out_shape=[jax.ShapeDtypeStruct((T, D_FF), BF16), jax.ShapeDtypeStruct((1, D_FF, D_MODEL), BF16)],
        compiler_params=_cparams(("parallel", "parallel")),
        name="ffn_up",
    )(h, halo, w_up, w_up, conv_w, conv_b.reshape(DEPTH, 1, D_FF), w_down)


def _conv_halo(h):
    tm = FFN_TILE_M
    nt = T // tm
    ht = h.reshape(nt, tm, D_MODEL)
    zero = jnp.zeros((1, D_MODEL), h.dtype)
    before = jnp.concatenate([zero, ht[:-1, tm - 1]], axis=0)
    after = jnp.concatenate([ht[1:, 0], zero], axis=0)
    pad = jnp.zeros((nt, HALO_ROWS - 2, D_MODEL), h.dtype)
    return jnp.concatenate([before[:, None], after[:, None], pad], axis=1)


SGU_CHUNKS_PER_STEP = 4


def _sgu_kernel(u_ref, v_ref, g_ref, b_ref, w_ref, bias_ref, o_ref):
    c = SGU_CHUNK
    for ci in range(SGU_CHUNKS_PER_STEP):
        rows = slice(ci * c, (ci + 1) * c)
        u = _gelu(u_ref[rows, :])
        v = _gelu(v_ref[rows, :])
        vc = v - jnp.mean(v, axis=-1, keepdims=True)
        vn = vc * lax.rsqrt(jnp.mean(vc * vc, axis=-1, keepdims=True) + EPS) * g_ref[...] + b_ref[...]
        vb = vn.astype(BF16)
        for g in range(SGU_GROUPS):
            sl = slice(g * SGU_GCH, (g + 1) * SGU_GCH)
            mixed = _dot(w_ref[g], vb[:, sl]) + bias_ref[:, sl]
            o_ref[rows, sl] = (u[:, sl] * mixed).astype(o_ref.dtype)


def _sgu(z, ln_g, ln_b, w_s, b_s):
    c = SGU_CHUNK
    rows = SGU_CHUNKS_PER_STEP * c
    bias = jnp.repeat(b_s.T, SGU_GCH, axis=1)
    vec = pl.BlockSpec((1, SGU_DIM), lambda i: (0, 0))
    return pl.pallas_call(
        _sgu_kernel,
        grid=(T // rows,),
        in_specs=[pl.BlockSpec((rows, SGU_DIM), lambda i: (i, OFF_SGU // SGU_DIM)),
                  pl.BlockSpec((rows, SGU_DIM), lambda i: (i, OFF_SGU // SGU_DIM + 1)),
                  vec, vec,
                  pl.BlockSpec((SGU_GROUPS, c, c), lambda i: (0, 0, 0)),
                  pl.BlockSpec((c, SGU_DIM), lambda i: (0, 0))],
        out_specs=pl.BlockSpec((rows, SGU_DIM), lambda i: (i, 0)),
        out_shape=jax.ShapeDtypeStruct((T, SGU_DIM), BF16),
        compiler_params=_cparams(("parallel",)),
        name="sgu",
    )(z, z, ln_g.reshape(1, SGU_DIM), ln_b.reshape(1, SGU_DIM), w_s.astype(BF16), bias)


SMALL_LEVELS = tuple(m for m in (1, 2, 4) if m < SUBLANE)
BIG_LEVELS = tuple(m for m in (8, 16, 32, 64) if m < SCAN_C)


def _scan_consts():
    c = SCAN_C
    t = np.arange(c)[:, None]
    s = np.arange(c)[None, :]
    x = t ^ s
    tri = np.stack([s <= t, s >= t]).astype(np.float32)
    pair = [np.stack([x == 0, x == 0])]
    for m in SMALL_LEVELS:
        lvl = (x >= m) & (x < 2 * m)
        pair.append(np.stack([lvl & (t > s), lvl & (t < s)]))
    pair = np.stack(pair).astype(np.float32)
    rowq = np.stack([np.broadcast_to(((np.arange(c) & m) != 0)[:, None], (c, LANE)) for m in SMALL_LEVELS])
    hb = np.arange(c // 2)
    blk = np.stack([(hb[:, None] // m) == (hb[None, :] // m) for m in BIG_LEVELS])
    return (jnp.asarray(tri, BF16), jnp.asarray(pair, F32), jnp.asarray(rowq.astype(np.float32), F32),
            jnp.asarray(blk.astype(np.float32), F32))


def _split2(x):
    hi = x.astype(BF16)
    lo = (x - hi.astype(F32)).astype(BF16)
    return hi, lo


def _halves(a, m, second):
    off = m if second else 0
    return jnp.concatenate([a[j + off:j + off + m] for j in range(0, a.shape[0], 2 * m)], axis=0)


def _chunk_scans(probs, consts):
    tri_ref, pair_ref, rowq_ref, blk_ref = consts
    c, dk = probs[0][0].shape
    dirs = [1 if p[5] else 0 for p in probs]
    vbs = [p[2].astype(BF16) for p in probs]
    diag = [pair_ref[0, d] * _dot_nt(p[0].astype(BF16), p[1].astype(BF16)) for p, d in zip(probs, dirs)]
    bs = []
    for p, d in zip(probs, dirs):
        hi, lo = _split2(p[3])
        tri = tri_ref[d]
        bs.append(_dot(tri, hi) + _dot(tri, lo))
    inter = [_dot_nt((p[0] * jnp.exp2(b)).astype(BF16), p[4].astype(BF16)) for p, b in zip(probs, bs)]

    big = [[] for _ in probs]
    for li, m in enumerate(BIG_LEVELS):
        for pi, ((q, k, _, _, _, reverse), b) in enumerate(zip(probs, bs)):
            q_second = not reverse
            refs = [b[j + m:j + m + 1] if reverse else b[j + m - 1:j + m] for j in range(0, c, 2 * m)]
            ref = jnp.concatenate([jnp.broadcast_to(r, (m, dk)) for r in refs], axis=0)
            qf = _halves(q, m, q_second) * jnp.exp2(_halves(b, m, q_second) - ref)
            kf = _halves(k, m, not q_second) * jnp.exp2(ref - _halves(b, m, not q_second))
            s_l = _dot_nt(qf.astype(BF16), kf.astype(BF16))
            if 2 * m < c:
                s_l = s_l * blk_ref[li]
            big[pi].append(s_l.astype(BF16))

    scores = diag
    es = list(bs)
    for li, m in enumerate(SMALL_LEVELS):
        for pi, ((q, k, _, _, _, reverse), b) in enumerate(zip(probs, bs)):
            is_q = (rowq_ref[li] < 0.5) if reverse else (rowq_ref[li] > 0.5)
            sh_q, sh_e = (c - m, m) if reverse else (m, c - m)
            fac = jnp.exp2(jnp.where(is_q, b - pltpu.roll(es[pi], sh_q, 0), es[pi] - b))
            u = (jnp.where(is_q, q, k) * fac).astype(BF16)
            scores[pi] = scores[pi] + pair_ref[1 + li, dirs[pi]] * _dot_nt(u, u)
            if li + 1 < len(SMALL_LEVELS):
                es[pi] = jnp.where(is_q, es[pi], pltpu.roll(es[pi], sh_e, 0))

    outs = []
    for pi, ((q, k, _, _, st, reverse), b) in enumerate(zip(probs, bs)):
        o = inter[pi] + _dot(scores[pi].astype(BF16), vbs[pi])
        parts = [o[g:g + SUBLANE] for g in range(0, c, SUBLANE)]
        q_second = not reverse
        for li, m in enumerate(BIG_LEVELS):
            oc = _dot(big[pi][li], _halves(probs[pi][2], m, not q_second).astype(BF16))
            for jj, j in enumerate(range(0, c, 2 * m)):
                for g in range(0, m, SUBLANE):
                    dst = (j + (m if q_second else 0) + g) // SUBLANE
                    parts[dst] = parts[dst] + oc[jj * m + g:jj * m + g + SUBLANE]
        bl = b[0:1, :] if reverse else b[c - 1:c, :]
        kd = (k * jnp.exp2(bl - b)).astype(BF16)
        st_new = jnp.exp2(bl) * st + _dot_tn(vbs[pi], kd)
        outs.append((jnp.concatenate(parts, axis=0), st_new))
    return outs


def _scan_kernel(*refs, features, n_in, seq, nseg, hpb, dv, with_s0, emit_state):
    consts = refs[:4]
    pos = 4
    dir_refs = []
    for _ in range(1 if nseg == 1 else 2):
        dir_refs.append((refs[pos:pos + n_in], refs[pos + n_in]))
        pos += n_in + 1
    dir_refs = dir_refs * 2 if nseg == 1 else dir_refs
    gain_ref = refs[pos]
    pos += 1
    s0_ref = None
    if with_s0:
        s0_ref = refs[pos]
        pos += 1
    o_ref = refs[pos]
    pos += 1
    sf_ref = None
    if emit_state:
        sf_ref = refs[pos]
        pos += 1
    o_scr, st_scr = refs[pos:pos + 2]
    c = SCAN_C
    n = seq // c
    cps = n // nseg
    g = pl.program_id(2) if nseg > 1 else 0
    work = [(hh, reverse) for hh in range(hpb) for reverse in (False, True)]

    def init():
        st_scr[...] = s0_ref[...] if with_s0 else jnp.zeros(st_scr.shape, F32)

    def step(i, finalize):
        loc = {False: pl.multiple_of(i * c, c), True: pl.multiple_of((cps - 1 - i) * c, c)}
        pair = g * cps + i
        glob = {False: pl.multiple_of(pair * c, c), True: pl.multiple_of((n - 1 - pair) * c, c)}
        probs = [features(dir_refs[rev][0], loc[rev], hh, rev) + (st_scr[1 if rev else 0, hh], rev)
                 for hh, rev in work]
        for (hh, rev), (o, st) in zip(work, _chunk_scans(probs, consts)):
            st_scr[1 if rev else 0, hh] = st
            rows, cols = pl.ds(glob[rev], c), slice(hh * dv, (hh + 1) * dv)
            if finalize:
                gate = dir_refs[rev][1][pl.ds(loc[rev], c), cols]
                y = _rms(o + o_scr[rows, cols], gain_ref[...]) * _silu(gate)
                o_ref[rows, cols] = y.astype(o_ref.dtype)
            else:
                o_scr[rows, cols] = o

    def loop(lo, hi, finalize):
        lax.fori_loop(lo, hi, lambda i, carry: (step(i, finalize), carry)[1], 0)

    if nseg == 1:
        init()
        loop(0, n // 2, False)
        loop(n // 2, n, True)
    else:
        pl.when(g == 0)(init)
        pl.when(g < nseg // 2)(lambda: loop(0, cps, False))
        pl.when(g >= nseg // 2)(lambda: loop(0, cps, True))
    if emit_state:
        assert nseg == 1
        for d in range(2):
            for hh in range(hpb):
                sf_ref[d, hh] = st_scr[d, hh].T


def _hgrn_features(in_refs, r0, head, reverse):
    hq_ref, hf_ref, hb_ref, hi_ref, lb_ref = in_refs
    rows, cols = pl.ds(r0, SCAN_C), slice(head * HG_DK, (head + 1) * HG_DK)
    q = _silu(hq_ref[rows, cols]) * (HG_DK ** -0.5)
    lb = lb_ref[1 if reverse else 0][:, cols]
    f = lb + (1.0 - lb) * _sigmoid((hb_ref if reverse else hf_ref)[rows, cols])
    return q, 1.0 - f, hi_ref[rows, head * HG_DV:(head + 1) * HG_DV], jnp.log(f) * LOG2E


def _gla_features(in_refs, r0, head, reverse):
    gq_ref, gk_ref, gv_ref, low_ref, wgk_ref, bgk_ref = in_refs
    rows, cols = pl.ds(r0, SCAN_C), slice(head * GLA_DK, (head + 1) * GLA_DK)
    d = 1 if reverse else 0
    g = _dot(low_ref[rows, :].astype(BF16), wgk_ref[d][:, cols]) + bgk_ref[d][:, cols]
    la = (jnp.minimum(g, 0.0) - jnp.log(1.0 + jnp.exp(-jnp.abs(g)))) * (LOG2E / GLA_NORMALIZER)
    return (gq_ref[rows, cols] * (GLA_DK ** -0.5), gk_ref[rows, cols],
            gv_ref[rows, head * GLA_DV:(head + 1) * GLA_DV], la)


def _seq_geom(latent):
    seq = DEC_SEQ if latent else SEQ
    return seq, (DEC_BATCH if latent else BATCH), (TP // seq if latent else 0)


def _zspec(latent, off, width, per_head=True):
    seq, _, blk0 = _seq_geom(latent)
    assert off % width == 0
    if per_head:
        return pl.BlockSpec((seq, width), lambda b, h, *_: (blk0 + b, off // width + h))
    return pl.BlockSpec((seq, width), lambda b, h, *_: (blk0 + b, off // width))


def _const_spec(a):
    return pl.BlockSpec(a.shape, lambda *_, nd=a.ndim: (0,) * nd)


SCAN_HEAD_COLS = {
    HG_HEADS: ((OFF_HQ, HG_DK), (OFF_HFF, HG_DK), (OFF_HFB, HG_DK), (OFF_HI, HG_DV), (OFF_HG, HG_DV)),
    GLA_HEADS: ((OFF_GQ, GLA_DK), (OFF_GK, GLA_DK), (OFF_GV, GLA_DV), (OFF_GG, GLA_DV)),
}


def _scan_geom(latent, heads=HG_HEADS):
    hpb = min(heads, 4 if latent else 8)
    while any(off % (hpb * width) for off, width in SCAN_HEAD_COLS[heads]):
        hpb //= 2
    return hpb, (8 if latent else 1)


def _zseg_spec(latent, off, width, reverse, per_head=True):
    seq, _, blk0 = _seq_geom(latent)
    nseg = _scan_geom(latent)[1]
    assert off % width == 0
    col0 = off // width

    def index(b, h, g=0):
        seg = (nseg - 1 - g) if reverse else g
        return ((blk0 + b) * nseg + seg, col0 + (h if per_head else 0))

    return pl.BlockSpec((seq // nseg, width), index)


def _scan_call(name, features, feat_fn, z, gate_off, gain, heads, dk, dv,
               latent, s0t, prev_out, prev_state, layer):
    seq, nb, row_blk0 = _seq_geom(latent)
    hpb, nseg = _scan_geom(latent, heads)
    consts = _scan_consts()
    in_specs = [_const_spec(a) for a in consts]
    args = list(consts)
    for reverse in ((False,) if nseg == 1 else (False, True)):
        specs, fargs = feat_fn(reverse)
        n_in = len(specs)
        in_specs += list(specs) + [_zseg_spec(latent, gate_off, hpb * dv, reverse)]
        args += list(fargs) + [z]
    in_specs.append(pl.BlockSpec((1, dv), lambda *_: (0, 0)))
    args.append(gain.reshape(1, dv))
    if latent:
        in_specs.append(pl.BlockSpec((None, None, 2, hpb, dv, dk), lambda b, h, *_: (b, layer, 0, h, 0, 0)))
        args.append(s0t)
    out_specs = [pl.BlockSpec((seq, hpb * dv), lambda b, h, *_: (row_blk0 + b, h))]
    out_shape = [jax.ShapeDtypeStruct((T, heads * dv), BF16)]
    aliases = {}
    if prev_out is not None:
        in_specs.append(pl.BlockSpec(memory_space=pl.ANY))
        aliases[len(args)] = 0
        args.append(prev_out)
    emit_state = not latent
    if emit_state:
        out_specs.append(pl.BlockSpec((None, None, 2, hpb, dk, dv), lambda b, h, *_: (b, layer, 0, h, 0, 0)))
        out_shape.append(jax.ShapeDtypeStruct((BATCH, DEPTH, 2, heads, dk, dv), F32))
        if prev_state is not None:
            in_specs.append(pl.BlockSpec(memory_space=pl.ANY))
            aliases[len(args)] = 1
            args.append(prev_state)
    n_alias = len(aliases)

    def kern(*refs):
        n_inputs = len(args) - n_alias
        keep = refs[:n_inputs] + refs[n_inputs + n_alias:]
        _scan_kernel(*keep, features=features, n_in=n_in, seq=seq, nseg=nseg, hpb=hpb, dv=dv, with_s0=latent,
                     emit_state=emit_state)

    grid = (nb, heads // hpb) + ((nseg,) if nseg > 1 else ())
    sem = ("parallel", "parallel") + (("arbitrary",) if nseg > 1 else ())
    res = pl.pallas_call(
        kern,
        grid=grid,
        in_specs=in_specs,
        out_specs=out_specs,
        out_shape=out_shape,
        input_output_aliases=aliases,
        scratch_shapes=[pltpu.VMEM((seq, hpb * dv), F32), pltpu.VMEM((2, hpb, dv, dk), F32)],
        compiler_params=_cparams(sem),
        name=name,
    )(*args)
    return (res[0], res[1]) if emit_state else (res[0], None)


def _hgrn(z, lb, gain, s0t, prev_state, layer):
    lb3 = lb.reshape(2, 1, HG_K)
    o = state = None
    for latent in (False, True):
        hpb = _scan_geom(latent, HG_HEADS)[0]

        def feat_fn(reverse, latent=latent, hpb=hpb):
            zs = functools.partial(_zseg_spec, latent, reverse=reverse)
            return ([zs(OFF_HQ, hpb * HG_DK), zs(OFF_HFF, hpb * HG_DK), zs(OFF_HFB, hpb * HG_DK),
                     zs(OFF_HI, hpb * HG_DV), pl.BlockSpec((2, 1, hpb * HG_DK), lambda b, h, *_: (0, 0, h))],
                    [z, z, z, z, lb3])

        o, st = _scan_call("hgrn_latent" if latent else "hgrn_prompt", _hgrn_features, feat_fn,
                           z, OFF_HG, gain, HG_HEADS, HG_DK, HG_DV, latent, s0t, o, prev_state, layer)
        state = st if st is not None else state
    return o, state


def _gla(z, w_gk, b_gk, gain, s0t, prev_state, layer):
    wpad = jnp.zeros((2, LANE, GLA_HEADS * GLA_DK), F32)
    wpad = wpad.at[0, 0:GLA_RANK].set(w_gk[0]).at[1, GLA_RANK:2 * GLA_RANK].set(w_gk[1]).astype(BF16)
    b3 = b_gk.reshape(2, 1, GLA_HEADS * GLA_DK)
    o = state = None
    for latent in (False, True):
        hpb = _scan_geom(latent, GLA_HEADS)[0]

        def feat_fn(reverse, latent=latent, hpb=hpb):
            zs = functools.partial(_zseg_spec, latent, reverse=reverse)
            return ([zs(OFF_GQ, hpb * GLA_DK), zs(OFF_GK, hpb * GLA_DK), zs(OFF_GV, hpb * GLA_DV),
                     zs(OFF_GLF, LANE, per_head=False),
                     pl.BlockSpec((2, LANE, hpb * GLA_DK), lambda b, h, *_: (0, 0, h)),
                     pl.BlockSpec((2, 1, hpb * GLA_DK), lambda b, h, *_: (0, 0, h))],
                    [z, z, z, z, wpad, b3])

        o, st = _scan_call("gla_latent" if latent else "gla_prompt", _gla_features, feat_fn,
                           z, OFF_GG, gain, GLA_HEADS, GLA_DK, GLA_DV, latent, s0t, o, prev_state, layer)
        state = st if st is not None else state
    return o, state


GQA = N_HEADS // N_KV_HEADS


def _ctx_attn_kernel(q_ref, k_ref, v_ref, sink_ref, o_ref):
    hd = HEAD_DIM
    ks = [k_ref[:, kh * hd:(kh + 1) * hd].astype(BF16) for kh in range(N_KV_HEADS)]
    vs = [v_ref[:, kh * hd:(kh + 1) * hd].astype(BF16) for kh in range(N_KV_HEADS)]
    heads = range(N_HEADS)
    ss = [_dot_nt((q_ref[:, h * hd:(h + 1) * hd] * (hd ** -0.5)).astype(BF16), ks[h // GQA]) for h in heads]
    ps, dens = [], []
    for h in heads:
        sink = sink_ref[h][:, 0:1]
        m = jnp.maximum(jnp.max(ss[h], axis=-1, keepdims=True), sink)
        p = jnp.exp(ss[h] - m)
        dens.append(jnp.sum(p, axis=-1, keepdims=True) + jnp.exp(sink - m))
        ps.append(p.astype(BF16))
    for h in heads:
        o_ref[:, h * hd:(h + 1) * hd] = (_dot(ps[h], vs[h // GQA]) / dens[h]).astype(o_ref.dtype)


def _ctx_attention(z, sink3):
    hd = HEAD_DIM
    kv_w = N_KV_HEADS * hd
    assert OFF_AK % kv_w == 0 and OFF_AV % kv_w == 0
    return pl.pallas_call(
        _ctx_attn_kernel,
        grid=(BATCH,),
        in_specs=[pl.BlockSpec((SEQ, N_HEADS * hd), lambda b: (b, OFF_AQ // (N_HEADS * hd))),
                  pl.BlockSpec((SEQ, kv_w), lambda b: (b, OFF_AK // kv_w)),
                  pl.BlockSpec((SEQ, kv_w), lambda b: (b, OFF_AV // kv_w)),
                  pl.BlockSpec((N_HEADS, 1, LANE), lambda b: (0, 0, 0))],
        out_specs=pl.BlockSpec((SEQ, N_HEADS * hd), lambda b: (b, 0)),
        out_shape=jax.ShapeDtypeStruct((T, N_HEADS * hd), BF16),
        compiler_params=_cparams(("parallel",)),
        name="ctx_attention",
    )(z, z, z, sink3)


def _rope(x, cos_f, sin_f):
    return x * cos_f + pltpu.roll(x, HEAD_DIM // 2, 1) * sin_f


def _lat_attn_kernel(q_ref, k_ref, v_ref, ck_ref, cv_ref, cos_ref, sin_ref, sink_ref, o_ref, kr_scr, vr_scr):
    blk, hd, seq = ATT_BLOCK, HEAD_DIM, DEC_SEQ
    nb = seq // blk

    zeros = jnp.zeros((blk, hd), BF16)
    for scr in (kr_scr, vr_scr):
        scr[0:blk, :] = zeros
        scr[blk + seq:2 * blk + seq, :] = zeros

    def fill(n, carry):
        r0 = pl.multiple_of(n * blk, blk)
        rows = pl.ds(r0, blk)
        kr_scr[pl.ds(r0 + blk, blk), :] = _rope(k_ref[rows, :], cos_ref[rows, :], sin_ref[rows, :]).astype(BF16)
        vr_scr[pl.ds(r0 + blk, blk), :] = v_ref[rows, :].astype(BF16)
        return carry

    lax.fori_loop(0, nb, fill, 0)

    ck = ck_ref[...].astype(BF16)
    cv = cv_ref[...].astype(BF16)
    sink = jnp.concatenate([jnp.broadcast_to(sink_ref[g][:, 0:1], (blk, 1)) for g in range(GQA)], axis=0)
    qi = lax.broadcasted_iota(jnp.int32, (GQA * blk, 3 * blk), 0) & (blk - 1)
    kj = lax.broadcasted_iota(jnp.int32, (GQA * blk, 3 * blk), 1)
    window_bias = jnp.where(jnp.abs(kj - qi - blk) <= WINDOW, 0.0, -jnp.inf)

    per_iter = 4

    def qblocks(it, carry):
        ns = [it * per_iter + u for u in range(per_iter)]
        r0s = [pl.multiple_of(n * blk, blk) for n in ns]
        qs = []
        for r0 in r0s:
            rows = pl.ds(r0, blk)
            cos_f, sin_f = cos_ref[rows, :], sin_ref[rows, :]
            qs.append(jnp.concatenate(
                [(_rope(q_ref[rows, g * hd:(g + 1) * hd], cos_f, sin_f) * (hd ** -0.5)).astype(BF16)
                 for g in range(GQA)], axis=0))
        bands = [pl.ds(r0, 3 * blk) for r0 in r0s]
        s_locs = [_dot_nt(q, kr_scr[band, :]) for q, band in zip(qs, bands)]
        s_ctxs = [_dot_nt(q, ck) for q in qs]
        ps = []
        for n, s_loc, s_ctx in zip(ns, s_locs, s_ctxs):
            key_pos = kj + (n - 1) * blk
            s_loc = s_loc + window_bias
            s_loc = jnp.where(key_pos >= 0, s_loc, -jnp.inf)
            s_loc = jnp.where(key_pos < seq, s_loc, -jnp.inf)
            m = jnp.maximum(jnp.maximum(jnp.max(s_loc, axis=-1, keepdims=True),
                                        jnp.max(s_ctx, axis=-1, keepdims=True)), sink)
            p_loc = jnp.exp(s_loc - m)
            p_ctx = jnp.exp(s_ctx - m)
            den = (jnp.sum(p_loc, axis=-1, keepdims=True) + jnp.sum(p_ctx, axis=-1, keepdims=True)
                   + jnp.exp(sink - m))
            ps.append((p_ctx.astype(BF16), p_loc.astype(BF16), den))
        for r0, band, (p_ctx, p_loc, den) in zip(r0s, bands, ps):
            o = (_dot(p_ctx, cv) + _dot(p_loc, vr_scr[band, :])) / den
            for g in range(GQA):
                o_ref[pl.ds(r0, blk), g * hd:(g + 1) * hd] = o[g * blk:(g + 1) * blk].astype(o_ref.dtype)
        return carry

    lax.fori_loop(0, nb // per_iter, qblocks, 0)


def _rope_tables():
    n = jnp.arange(DEC_SEQ)
    row = (n // GRID_W).astype(F32)
    col = (n % GRID_W).astype(F32)
    n_freq = HEAD_DIM // 4
    inv = ROPE_THETA ** (-jnp.arange(n_freq, dtype=F32) / n_freq)
    ang = jnp.concatenate([row[:, None] * inv, col[:, None] * inv], axis=-1)
    cos, sin = jnp.cos(ang), jnp.sin(ang)
    return jnp.concatenate([cos, cos], axis=-1), jnp.concatenate([-sin, sin], axis=-1)


def _lat_attention(z, ck, cv, sink3, prev_out):
    hd = HEAD_DIM
    seq, _, blk0 = _seq_geom(True)
    cos_f, sin_f = _rope_tables()
    tab = pl.BlockSpec((seq, hd), lambda b, kh: (0, 0))
    cache = pl.BlockSpec((None, PAST_LEN, hd), lambda b, kh: (b, 0, kh))
    return pl.pallas_call(
        lambda *refs: _lat_attn_kernel(*refs[:8], *refs[9:]),
        grid=(DEC_BATCH, N_KV_HEADS),
        in_specs=[_zspec(True, OFF_AQ, GQA * hd),
                  pl.BlockSpec((seq, hd), lambda b, kh: (blk0 + b, OFF_AK // hd + kh)),
                  pl.BlockSpec((seq, hd), lambda b, kh: (blk0 + b, OFF_AV // hd + kh)),
                  cache, cache, tab, tab,
                  pl.BlockSpec((GQA, 1, LANE), lambda b, kh: (kh, 0, 0)),
                  pl.BlockSpec(memory_space=pl.ANY)],
        out_specs=pl.BlockSpec((seq, GQA * hd), lambda b, kh: (blk0 + b, kh)),
        out_shape=jax.ShapeDtypeStruct((T, N_HEADS * hd), BF16),
        input_output_aliases={8: 0},
        scratch_shapes=[pltpu.VMEM((seq + 2 * ATT_BLOCK, hd), BF16), pltpu.VMEM((seq + 2 * ATT_BLOCK, hd), BF16)],
        compiler_params=_cparams(("parallel", "parallel")),
        name="lat_attention",
    )(z, z, z, ck, cv, cos_f, sin_f, sink3, prev_out)


def kernel(x_prompt, x_sample, cache_k, cache_v, state_hgrn, state_gla, c, c_ctx, w_ada, b_ada, norm_g, w_in,
           w_out, sgu_ln_g, sgu_ln_b, sgu_w, sgu_b, hgrn_lb_logits, hgrn_norm_g, attn_sink, gla_w_gk, gla_b_gk,
           gla_norm_g, ffn_w_up, ffn_conv_w, ffn_conv_b, ffn_w_down):
    assert TP % DEC_SEQ == 0 and DEC_BATCH + 1 <= MOD_ROWS
    lb_all = jnp.cumsum(jax.nn.softmax(hgrn_lb_logits.astype(F32), axis=0), axis=0)
    lb_all = lb_all - lb_all[0:1]

    cvec = jnp.zeros((MOD_ROWS, D_MODEL), F32).at[0].set(c_ctx).at[1:1 + DEC_BATCH].set(c)
    mod = _ada_mod(cvec, w_ada, b_ada).reshape(DEPTH, MOD_ROWS, 1, 6 * D_MODEL)
    SH1, SC1, G1, SH2, SC2, G2 = range(6)

    x = [x_prompt.reshape(TP, D_MODEL), x_sample.reshape(TL, D_MODEL)]
    hg_s0t = jnp.swapaxes(state_hgrn, -1, -2)
    gl_s0t = jnp.swapaxes(state_gla, -1, -2)
    ck_all = cache_k.reshape(DEC_BATCH, DEPTH, PAST_LEN, N_KV_HEADS * HEAD_DIM)
    cv_all = cache_v.reshape(DEC_BATCH, DEPTH, PAST_LEN, N_KV_HEADS * HEAD_DIM)

    h = _norm_mod(x, norm_g[0, 0], mod, 0, SC1, SH1)
    ks_new, vs_new = [], []
    hg_state = gl_state = None
    w_in_b, w_up_b = w_in.astype(BF16), ffn_w_up
    w_out_b = w_out.astype(BF16)
    for l in range(DEPTH):
        sink3 = jnp.broadcast_to(attn_sink[l].astype(F32)[:, None, None], (N_HEADS, 1, LANE))

        z = _matmul(h, w_in_b, l, Z_COLS, F32, Z_TILE_N, "in_proj")
        o_sgu = _sgu(z, sgu_ln_g[l], sgu_ln_b[l], sgu_w[l], sgu_b[l])
        o_hg, hg_state = _hgrn(z, lb_all[l], hgrn_norm_g[l], hg_s0t, hg_state, l)
        o_att = _ctx_attention(z, sink3)
        o_att = _lat_attention(z, ck_all[:, l], cv_all[:, l], sink3, o_att)
        o_gl, gl_state = _gla(z, gla_w_gk[l], gla_b_gk[l], gla_norm_g[l], gl_s0t, gl_state, l)
        m = _out_proj([o_sgu, o_hg, o_att, o_gl], w_out_b, l)
        x, h = _resid(x, m, norm_g[l, 1], mod, l, G1, nxt=(norm_g[l, 2], l, SC2, SH2))
        ks_new.append(z[:TP, OFF_AK:OFF_AK + N_KV_HEADS * HEAD_DIM].reshape(BATCH, SEQ, N_KV_HEADS, HEAD_DIM))
        vs_new.append(z[:TP, OFF_AV:OFF_AV + N_KV_HEADS * HEAD_DIM].reshape(BATCH, SEQ, N_KV_HEADS, HEAD_DIM))

        act, w_down_b = _ffn_up(h, _conv_halo(h), w_up_b, ffn_conv_w, ffn_conv_b, ffn_w_down, l)
        f = _ffn_down(act, w_down_b, 0)
        if l + 1 < DEPTH:
            x, h = _resid(x, f, norm_g[l, 3], mod, l, G2, nxt=(norm_g[l + 1, 0], l + 1, SC1, SH1))
        else:
            x, h = _resid(x, f, norm_g[l, 3], mod, l, G2, split_out=True)

    y_prompt = x[0].reshape(BATCH, SEQ, D_MODEL)
    y_sample = x[1].reshape(DEC_BATCH, DEC_SEQ, D_MODEL)
    return (y_prompt, y_sample, jnp.stack(ks_new, axis=1), jnp.stack(vs_new, axis=1), hg_state, gl_state)
```

```python
import functools

import numpy as np
import jax
import jax.numpy as jnp
from jax import lax
from jax.experimental import pallas as pl
from jax.experimental.pallas import tpu as pltpu

D_MODEL = 4096
BATCH = 32
SEQ = 256
DEPTH = 2
DEC_BATCH = 2
DEC_SEQ = 4096
PAST_LEN = 512
GRID_W = 64
GROUP_W = D_MODEL // 4
SGU_CHUNK = 128
SGU_GROUPS = 4
SGU_DIM = GROUP_W
SGU_GCH = SGU_DIM // SGU_GROUPS
HG_HEADS = 8
HG_DK = 128
HG_DV = GROUP_W // HG_HEADS
HG_K = HG_HEADS * HG_DK
N_HEADS = 8
N_KV_HEADS = 2
HEAD_DIM = GROUP_W // N_HEADS
WINDOW = 128
ATT_BLOCK = 128
ROPE_THETA = 10000.0
GLA_HEADS = 4
GLA_DK = 128
GLA_DV = GROUP_W // GLA_HEADS
GLA_RANK = 16
GLA_NORMALIZER = 16.0
D_FF = 11008
CONV_W = 3
EPS = 1e-6

F32 = jnp.float32
BF16 = jnp.bfloat16
LOG2E = 1.4426950408889634

IN_SIZES = (2 * SGU_DIM,
            HG_K, HG_K, HG_K, HG_HEADS * HG_DV, HG_HEADS * HG_DV,
            N_HEADS * HEAD_DIM, N_KV_HEADS * HEAD_DIM, N_KV_HEADS * HEAD_DIM,
            GLA_HEADS * GLA_DK, GLA_HEADS * GLA_DK, GLA_HEADS * GLA_DV, GLA_HEADS * GLA_DV,
            GLA_RANK, GLA_RANK)
D_IN = sum(IN_SIZES)
_OFF = [0] + [int(c) for c in np.cumsum(IN_SIZES)]
(OFF_SGU, OFF_HQ, OFF_HFF, OFF_HFB, OFF_HI, OFF_HG, OFF_AQ, OFF_AK, OFF_AV,
 OFF_GQ, OFF_GK, OFF_GV, OFF_GG, OFF_GLF, OFF_GLB) = _OFF[:-1]

LANE = 128
SUBLANE = 8
VMEM_LIMIT = 56 * 1024 * 1024

Z_TILE_N = 1024
Z_COLS = -(-D_IN // Z_TILE_N) * Z_TILE_N
SCAN_C = 128
MOD_ROWS = 16

TP = BATCH * SEQ
TL = DEC_BATCH * DEC_SEQ
T = TP + TL
PIECES = ((0, TP), (TP, TL))


def _cparams(sem):
    return pltpu.CompilerParams(dimension_semantics=sem, vmem_limit_bytes=VMEM_LIMIT)


def _row_group(row0):
    return jnp.where(row0 < TP, 0, 1 + (row0 - TP) // DEC_SEQ)


def _dot(a, b):
    return jnp.dot(a, b, preferred_element_type=F32)


def _dot_nt(a, b):
    return lax.dot_general(a, b, (((1,), (1,)), ((), ())), preferred_element_type=F32)


def _dot_tn(a, b):
    return lax.dot_general(a, b, (((0,), (0,)), ((), ())), preferred_element_type=F32)


def _sigmoid(x):
    return 1.0 / (1.0 + jnp.exp(-x))


def _silu(x):
    return x * _sigmoid(x)


def _gelu(x):
    return 0.5 * x * (1.0 + lax.erf(x * (2.0 ** -0.5)))


def _rms(x, g):
    return x * lax.rsqrt(jnp.mean(x * x, axis=-1, keepdims=True) + EPS) * g


def _ada_kernel(c_ref, w_ref, b_ref, o_ref):
    c = _silu(c_ref[...]).astype(BF16)
    o_ref[...] = _dot(c, w_ref[...].astype(BF16)) + b_ref[...]


def _ada_mod(cvec, w_ada, b_ada):
    tn = 512
    n = w_ada.shape[-1]
    return pl.pallas_call(
        _ada_kernel,
        grid=(DEPTH, n // tn),
        in_specs=[pl.BlockSpec((MOD_ROWS, D_MODEL), lambda l, j: (0, 0)),
                  pl.BlockSpec((None, D_MODEL, tn), lambda l, j: (l, 0, j)),
                  pl.BlockSpec((None, 1, tn), lambda l, j: (l, 0, j))],
        out_specs=pl.BlockSpec((None, MOD_ROWS, tn), lambda l, j: (l, 0, j)),
        out_shape=jax.ShapeDtypeStruct((DEPTH, MOD_ROWS, n), F32),
        compiler_params=_cparams(("parallel", "parallel")),
        name="ada_mod",
    )(cvec, w_ada, b_ada.reshape(DEPTH, 1, n))


def _mod_spec(layer, chunk, tr, blk0=0):
    return pl.BlockSpec((None, None, 1, D_MODEL),
                        lambda i, *_: (layer, _row_group((blk0 + i) * tr), 0, chunk))


NORM_ROWS = 256
_VEC_SPEC = pl.BlockSpec((1, D_MODEL), lambda i: (0, 0))


def _row_spec(blk0=0):
    return pl.BlockSpec((NORM_ROWS, D_MODEL), lambda i: (blk0 + i, 0))


def _norm_mod_kernel(x_ref, g_ref, sc_ref, sh_ref, h_ref):
    y = _rms(x_ref[...], g_ref[...])
    h_ref[...] = (y * (1.0 + sc_ref[...]) + sh_ref[...]).astype(h_ref.dtype)


def _norm_mod(x_parts, g, mod, layer, sc_chunk, sh_chunk):
    tr = NORM_ROWS
    h = None
    for x, (row0, rows) in zip(x_parts, PIECES):
        blk0 = row0 // tr
        in_specs = [_row_spec(), _VEC_SPEC, _mod_spec(layer, sc_chunk, tr, blk0), _mod_spec(layer, sh_chunk, tr, blk0)]
        args = [x, g.reshape(1, D_MODEL), mod, mod]
        aliases = {}
        if h is not None:
            in_specs.append(pl.BlockSpec(memory_space=pl.ANY))
            aliases[len(args)] = 0
            args.append(h)
        h = pl.pallas_call(
            lambda x_ref, g_ref, sc_ref, sh_ref, *rest: _norm_mod_kernel(x_ref, g_ref, sc_ref, sh_ref, rest[-1]),
            grid=(rows // tr,),
            in_specs=in_specs,
            out_specs=_row_spec(blk0),
            out_shape=jax.ShapeDtypeStruct((T, D_MODEL), BF16),
            input_output_aliases=aliases,
            compiler_params=_cparams(("parallel",)),
            name="norm_mod",
        )(*args)
    return h


def _resid_kernel(x_ref, m_ref, ga_ref, gate_ref, *rest, emit_h):
    x = x_ref[...] + gate_ref[...] * _rms(m_ref[...].astype(F32), ga_ref[...])
    if emit_h:
        gb_ref, sc_ref, sh_ref, xo_ref, h_ref = rest
        xo_ref[...] = x
        y = _rms(x, gb_ref[...])
        h_ref[...] = (y * (1.0 + sc_ref[...]) + sh_ref[...]).astype(h_ref.dtype)
    else:
        (xo_ref,) = rest
        xo_ref[...] = x


def _resid(x, m, ga, mod, layer, gate_chunk, nxt=None, split_out=False):
    tr = NORM_ROWS
    x_parts = x if isinstance(x, (list, tuple)) else None
    pieces = PIECES if (x_parts is not None or split_out) else ((0, T),)
    emit_h = nxt is not None
    assert not (emit_h and split_out)
    x_new, h, outs = None, None, []
    for p, (row0, rows) in enumerate(pieces):
        blk0 = row0 // tr
        in_specs = [_row_spec(0 if x_parts is not None else blk0), _row_spec(blk0), _VEC_SPEC,
                    _mod_spec(layer, gate_chunk, tr, blk0)]
        args = [x_parts[p] if x_parts is not None else x, m, ga.reshape(1, D_MODEL), mod]
        if emit_h:
            gb, layer_b, sc_chunk, sh_chunk = nxt
            in_specs += [_VEC_SPEC, _mod_spec(layer_b, sc_chunk, tr, blk0), _mod_spec(layer_b, sh_chunk, tr, blk0)]
            args += [gb.reshape(1, D_MODEL), mod, mod]
        n_in = len(args)
        if split_out:
            out_specs = [_row_spec()]
            out_shape = [jax.ShapeDtypeStruct((rows, D_MODEL), F32)]
        else:
            out_specs = [_row_spec(blk0)]
            out_shape = [jax.ShapeDtypeStruct((T, D_MODEL), F32)]
        if emit_h:
            out_specs.append(_row_spec(blk0))
            out_shape.append(jax.ShapeDtypeStruct((T, D_MODEL), BF16))
        aliases = {}
        if x_new is not None and not split_out:
            for k, prev in enumerate((x_new, h) if emit_h else (x_new,)):
                in_specs.append(pl.BlockSpec(memory_space=pl.ANY))
                aliases[len(args)] = k
                args.append(prev)
        n_alias = len(aliases)

        def kern(*refs, n_in=n_in, n_alias=n_alias):
            _resid_kernel(*refs[:n_in], *refs[n_in + n_alias:], emit_h=emit_h)

        res = pl.pallas_call(
            kern,
            grid=(rows // tr,),
            in_specs=in_specs,
            out_specs=out_specs,
            out_shape=out_shape,
            input_output_aliases=aliases,
            compiler_params=_cparams(("parallel",)),
            name="resid_norm",
        )(*args)
        x_new = res[0]
        h = res[1] if emit_h else None
        outs.append(res[0])
    return (outs if split_out else x_new), h


MM_TILE_M = 1024


def _mm_kernel(a_ref, b_ref, o_ref, *, n_valid):
    acc = _dot(a_ref[...], b_ref[...])
    tn = o_ref.shape[1]
    if n_valid % tn:
        col = lax.broadcasted_iota(jnp.int32, acc.shape, 1)
        acc = jnp.where(col < n_valid - pl.program_id(1) * tn, acc, 0.0)
    o_ref[...] = acc.astype(o_ref.dtype)


def _matmul(a, w, layer, n_out, out_dtype, tn, name):
    m, k = a.shape
    tm = MM_TILE_M
    return pl.pallas_call(
        functools.partial(_mm_kernel, n_valid=w.shape[2]),
        grid=(m // tm, n_out // tn),
        in_specs=[pl.BlockSpec((tm, k), lambda i, j: (i, 0)),
                  pl.BlockSpec((None, k, tn), lambda i, j: (layer, 0, j))],
        out_specs=pl.BlockSpec((tm, tn), lambda i, j: (i, j)),
        out_shape=jax.ShapeDtypeStruct((m, n_out), out_dtype),
        compiler_params=_cparams(("parallel", "parallel")),
        name=name,
    )(a, w)


def _mm4_kernel(a0_ref, a1_ref, a2_ref, a3_ref, b_ref, o_ref):
    acc = _dot(a0_ref[...], b_ref[0 * GROUP_W:1 * GROUP_W, :])
    acc += _dot(a1_ref[...], b_ref[1 * GROUP_W:2 * GROUP_W, :])
    acc += _dot(a2_ref[...], b_ref[2 * GROUP_W:3 * GROUP_W, :])
    acc += _dot(a3_ref[...], b_ref[3 * GROUP_W:4 * GROUP_W, :])
    o_ref[...] = acc.astype(o_ref.dtype)


def _out_proj(parts, w, layer):
    tm, tn = MM_TILE_M, 1024
    a_spec = pl.BlockSpec((tm, GROUP_W), lambda i, j: (i, 0))
    return pl.pallas_call(
        _mm4_kernel,
        grid=(T // tm, D_MODEL // tn),
        in_specs=[a_spec, a_spec, a_spec, a_spec,
                  pl.BlockSpec((None, 4 * GROUP_W, tn), lambda i, j: (layer, 0, j))],
        out_specs=pl.BlockSpec((tm, tn), lambda i, j: (i, j)),
        out_shape=jax.ShapeDtypeStruct((T, D_MODEL), BF16),
        compiler_params=_cparams(("parallel", "parallel")),
        name="out_proj",
    )(*parts, w)


def _ffn_down(a, w, layer):
    m, kk = a.shape
    n = w.shape[2]
    tm, tn = 512, 512
    return pl.pallas_call(
        functools.partial(_mm_kernel, n_valid=n),
        grid=(m // tm, n // tn),
        in_specs=[pl.BlockSpec((tm, kk), lambda i, j: (i, 0)),
                  pl.BlockSpec((None, kk, tn), lambda i, j: (layer, 0, j))],
        out_specs=pl.BlockSpec((tm, tn), lambda i, j: (i, j)),
        out_shape=jax.ShapeDtypeStruct((m, n), BF16),
        compiler_params=_cparams(("parallel", "parallel")),
        name="ffn_down",
    )(a, w)


FFN_TILE_M = 2048
FFN_TILE_N = 256
FFN_SUB_M = 256
HALO_ROWS = 16
assert D_FF % FFN_TILE_N == 0


def _ffn_up_kernel(h_ref, halo_ref, wg_ref, wv_ref, cw_ref, cb_ref, wd_ref, o_ref, wdb_ref):
    wdb_ref[...] = wd_ref[...].astype(wdb_ref.dtype)
    tm, tn = o_ref.shape
    sub = FFN_SUB_M
    ns = tm // sub
    wg, wv = wg_ref[...].astype(BF16), wv_ref[...].astype(BF16)
    row0 = pl.program_id(0) * tm
    seq_len = jnp.where(row0 < TP, SEQ, DEC_SEQ)
    r = lax.broadcasted_iota(jnp.int32, (sub, tn), 0)
    g0 = _dot(jnp.concatenate([halo_ref[...], h_ref[0:sub, :]], axis=0), wg)
    gh = g0[0:HALO_ROWS]
    gs = [g0[HALO_ROWS:]] + [_dot(h_ref[s * sub:(s + 1) * sub, :], wg) for s in range(1, ns)]
    for s in range(ns):
        val = _dot(h_ref[s * sub:(s + 1) * sub, :], wv)
        g = gs[s]
        before = gh[0:1, :] if s == 0 else gs[s - 1][sub - 1:sub, :]
        after = gh[1:2, :] if s == ns - 1 else gs[s + 1][0:1, :]
        pos = (row0 + s * sub + r) & (seq_len - 1)
        prev = jnp.where(r == 0, before, pltpu.roll(g, 1, 0))
        prev = jnp.where(pos == 0, 0.0, prev)
        nxt = jnp.where(r == sub - 1, after, pltpu.roll(g, sub - 1, 0))
        nxt = jnp.where(pos == seq_len - 1, 0.0, nxt)
        conv = prev * cw_ref[0:1, :] + g * cw_ref[1:2, :] + nxt * cw_ref[2:3, :] + cb_ref[...]
        o_ref[s * sub:(s + 1) * sub, :] = (_gelu(conv) * val).astype(o_ref.dtype)


def _ffn_up(h, halo, w_up, conv_w, conv_b, w_down, layer):
    tm, tn = FFN_TILE_M, FFN_TILE_N
    nj = D_FF // tn
    steps = (T // tm) * nj
    wd_rows = D_FF // steps
    assert D_FF % steps == 0 and wd_rows % 16 == 0
    wd_in = pl.BlockSpec((None, wd_rows, D_MODEL), lambda i, j: (layer, i * nj + j, 0))
    wd_out = pl.BlockSpec((None, wd_rows, D_MODEL), lambda i, j: (0, i * nj + j, 0))
    return pl.pallas_call(
        _ffn_up_kernel,
        grid=(T // tm, nj),
        in_specs=[pl.BlockSpec((tm, D_MODEL), lambda i, j: (i, 0), pipeline_mode=pl.Buffered(1)),
                  pl.BlockSpec((None, HALO_ROWS, D_MODEL), lambda i, j: (i, 0, 0)),
                  pl.BlockSpec((None, D_MODEL, tn), lambda i, j: (layer, 0, j)),
                  pl.BlockSpec((None, D_MODEL, tn), lambda i, j: (layer, 0, nj + j)),
                  pl.BlockSpec((None, CONV_W, tn), lambda i, j: (layer, 0, j)),
                  pl.BlockSpec((None, 1, tn), lambda i, j: (layer, 0, j)),
                  wd_in],
        out_specs=[pl.BlockSpec((tm, tn), lambda i, j: (i, j)), wd_out],
        out_shape=[jax.ShapeDtypeStruct((T, D_FF), BF16), jax.ShapeDtypeStruct((1, D_FF, D_MODEL), BF16)],
        compiler_params=_cparams(("parallel", "parallel")),
        name="ffn_up",
    )(h, halo, w_up, w_up, conv_w, conv_b.reshape(DEPTH, 1, D_FF), w_down)


def _conv_halo(h):
    tm = FFN_TILE_M
    nt = T // tm
    ht = h.reshape(nt, tm, D_MODEL)
    zero = jnp.zeros((1, D_MODEL), h.dtype)
    before = jnp.concatenate([zero, ht[:-1, tm - 1]], axis=0)
    after = jnp.concatenate([ht[1:, 0], zero], axis=0)
    pad = jnp.zeros((nt, HALO_ROWS - 2, D_MODEL), h.dtype)
    return jnp.concatenate([before[:, None], after[:, None], pad], axis=1)


SGU_CHUNKS_PER_STEP = 4


def _sgu_kernel(u_ref, v_ref, g_ref, b_ref, w_ref, bias_ref, o_ref):
    c = SGU_CHUNK
    for ci in range(SGU_CHUNKS_PER_STEP):
        rows = slice(ci * c, (ci + 1) * c)
        u = _gelu(u_ref[rows, :])
        v = _gelu(v_ref[rows, :])
        vc = v - jnp.mean(v, axis=-1, keepdims=True)
        vn = vc * lax.rsqrt(jnp.mean(vc * vc, axis=-1, keepdims=True) + EPS) * g_ref[...] + b_ref[...]
        vb = vn.astype(BF16)
        for g in range(SGU_GROUPS):
            sl = slice(g * SGU_GCH, (g + 1) * SGU_GCH)
            mixed = _dot(w_ref[g], vb[:, sl]) + bias_ref[:, sl]
            o_ref[rows, sl] = (u[:, sl] * mixed).astype(o_ref.dtype)


def _sgu(z, ln_g, ln_b, w_s, b_s):
    c = SGU_CHUNK
    rows = SGU_CHUNKS_PER_STEP * c
    bias = jnp.repeat(b_s.T, SGU_GCH, axis=1)
    vec = pl.BlockSpec((1, SGU_DIM), lambda i: (0, 0))
    return pl.pallas_call(
        _sgu_kernel,
        grid=(T // rows,),
        in_specs=[pl.BlockSpec((rows, SGU_DIM), lambda i: (i, OFF_SGU // SGU_DIM)),
                  pl.BlockSpec((rows, SGU_DIM), lambda i: (i, OFF_SGU // SGU_DIM + 1)),
                  vec, vec,
                  pl.BlockSpec((SGU_GROUPS, c, c), lambda i: (0, 0, 0)),
                  pl.BlockSpec((c, SGU_DIM), lambda i: (0, 0))],
        out_specs=pl.BlockSpec((rows, SGU_DIM), lambda i: (i, 0)),
        out_shape=jax.ShapeDtypeStruct((T, SGU_DIM), BF16),
        compiler_params=_cparams(("parallel",)),
        name="sgu",
    )(z, z, ln_g.reshape(1, SGU_DIM), ln_b.reshape(1, SGU_DIM), w_s.astype(BF16), bias)


SMALL_LEVELS = tuple(m for m in (1, 2, 4) if m < SUBLANE)
BIG_LEVELS = tuple(m for m in (8, 16, 32, 64) if m < SCAN_C)


def _scan_consts():
    c = SCAN_C
    t = np.arange(c)[:, None]
    s = np.arange(c)[None, :]
    x = t ^ s
    tri = np.stack([s <= t, s >= t]).astype(np.float32)
    pair = [np.stack([x == 0, x == 0])]
    for m in SMALL_LEVELS:
        lvl = (x >= m) & (x < 2 * m)
        pair.append(np.stack([lvl & (t > s), lvl & (t < s)]))
    pair = np.stack(pair).astype(np.float32)
    rowq = np.stack([np.broadcast_to(((np.arange(c) & m) != 0)[:, None], (c, LANE)) for m in SMALL_LEVELS])
    hb = np.arange(c // 2)
    blk = np.stack([(hb[:, None] // m) == (hb[None, :] // m) for m in BIG_LEVELS])
    return (jnp.asarray(tri, BF16), jnp.asarray(pair, F32), jnp.asarray(rowq.astype(np.float32), F32),
            jnp.asarray(blk.astype(np.float32), F32))


def _split2(x):
    hi = x.astype(BF16)
    lo = (x - hi.astype(F32)).astype(BF16)
    return hi, lo


def _halves(a, m, second):
    off = m if second else 0
    return jnp.concatenate([a[j + off:j + off + m] for j in range(0, a.shape[0], 2 * m)], axis=0)


def _chunk_scans(probs, consts):
    tri_ref, pair_ref, rowq_ref, blk_ref = consts
    c, dk = probs[0][0].shape
    dirs = [1 if p[5] else 0 for p in probs]
    vbs = [p[2].astype(BF16) for p in probs]
    diag = [pair_ref[0, d] * _dot_nt(p[0].astype(BF16), p[1].astype(BF16)) for p, d in zip(probs, dirs)]
    bs = []
    for p, d in zip(probs, dirs):
        hi, lo = _split2(p[3])
        tri = tri_ref[d]
        bs.append(_dot(tri, hi) + _dot(tri, lo))
    inter = [_dot_nt((p[0] * jnp.exp2(b)).astype(BF16), p[4].astype(BF16)) for p, b in zip(probs, bs)]

    big = [[] for _ in probs]
    for li, m in enumerate(BIG_LEVELS):
        for pi, ((q, k, _, _, _, reverse), b) in enumerate(zip(probs, bs)):
            q_second = not reverse
            refs = [b[j + m:j + m + 1] if reverse else b[j + m - 1:j + m] for j in range(0, c, 2 * m)]
            ref = jnp.concatenate([jnp.broadcast_to(r, (m, dk)) for r in refs], axis=0)
            qf = _halves(q, m, q_second) * jnp.exp2(_halves(b, m, q_second) - ref)
            kf = _halves(k, m, not q_second) * jnp.exp2(ref - _halves(b, m, not q_second))
            s_l = _dot_nt(qf.astype(BF16), kf.astype(BF16))
            if 2 * m < c:
                s_l = s_l * blk_ref[li]
            big[pi].append(s_l.astype(BF16))

    scores = diag
    es = list(bs)
    for li, m in enumerate(SMALL_LEVELS):
        for pi, ((q, k, _, _, _, reverse), b) in enumerate(zip(probs, bs)):
            is_q = (rowq_ref[li] < 0.5) if reverse else (rowq_ref[li] > 0.5)
            sh_q, sh_e = (c - m, m) if reverse else (m, c - m)
            fac = jnp.exp2(jnp.where(is_q, b - pltpu.roll(es[pi], sh_q, 0), es[pi] - b))
            u = (jnp.where(is_q, q, k) * fac).astype(BF16)
            scores[pi] = scores[pi] + pair_ref[1 + li, dirs[pi]] * _dot_nt(u, u)
            if li + 1 < len(SMALL_LEVELS):
                es[pi] = jnp.where(is_q, es[pi], pltpu.roll(es[pi], sh_e, 0))

    outs = []
    for pi, ((q, k, _, _, st, reverse), b) in enumerate(zip(probs, bs)):
        o = inter[pi] + _dot(scores[pi].astype(BF16), vbs[pi])
        parts = [o[g:g + SUBLANE] for g in range(0, c, SUBLANE)]
        q_second = not reverse
        for li, m in enumerate(BIG_LEVELS):
            oc = _dot(big[pi][li], _halves(probs[pi][2], m, not q_second).astype(BF16))
            for jj, j in enumerate(range(0, c, 2 * m)):
                for g in range(0, m, SUBLANE):
                    dst = (j + (m if q_second else 0) + g) // SUBLANE
                    parts[dst] = parts[dst] + oc[jj * m + g:jj * m + g + SUBLANE]
        bl = b[0:1, :] if reverse else b[c - 1:c, :]
        kd = (k * jnp.exp2(bl - b)).astype(BF16)
        st_new = jnp.exp2(bl) * st + _dot_tn(vbs[pi], kd)
        outs.append((jnp.concatenate(parts, axis=0), st_new))
    return outs


def _scan_kernel(*refs, features, n_in, seq, nseg, hpb, dv, with_s0, emit_state):
    consts = refs[:4]
    pos = 4
    dir_refs = []
    for _ in range(1 if nseg == 1 else 2):
        dir_refs.append((refs[pos:pos + n_in], refs[pos + n_in]))
        pos += n_in + 1
    dir_refs = dir_refs * 2 if nseg == 1 else dir_refs
    gain_ref = refs[pos]
    pos += 1
    s0_ref = None
    if with_s0:
        s0_ref = refs[pos]
        pos += 1
    o_ref = refs[pos]
    pos += 1
    sf_ref = None
    if emit_state:
        sf_ref = refs[pos]
        pos += 1
    o_scr, st_scr = refs[pos:pos + 2]
    c = SCAN_C
    n = seq // c
    cps = n // nseg
    g = pl.program_id(2) if nseg > 1 else 0
    work = [(hh, reverse) for hh in range(hpb) for reverse in (False, True)]

    def init():
        st_scr[...] = s0_ref[...] if with_s0 else jnp.zeros(st_scr.shape, F32)

    def step(i, finalize):
        loc = {False: pl.multiple_of(i * c, c), True: pl.multiple_of((cps - 1 - i) * c, c)}
        pair = g * cps + i
        glob = {False: pl.multiple_of(pair * c, c), True: pl.multiple_of((n - 1 - pair) * c, c)}
        probs = [features(dir_refs[rev][0], loc[rev], hh, rev) + (st_scr[1 if rev else 0, hh], rev)
                 for hh, rev in work]
        for (hh, rev), (o, st) in zip(work, _chunk_scans(probs, consts)):
            st_scr[1 if rev else 0, hh] = st
            rows, cols = pl.ds(glob[rev], c), slice(hh * dv, (hh + 1) * dv)
            if finalize:
                gate = dir_refs[rev][1][pl.ds(loc[rev], c), cols]
                y = _rms(o + o_scr[rows, cols], gain_ref[...]) * _silu(gate)
                o_ref[rows, cols] = y.astype(o_ref.dtype)
            else:
                o_scr[rows, cols] = o

    def loop(lo, hi, finalize):
        lax.fori_loop(lo, hi, lambda i, carry: (step(i, finalize), carry)[1], 0)

    if nseg == 1:
        init()
        loop(0, n // 2, False)
        loop(n // 2, n, True)
    else:
        pl.when(g == 0)(init)
        pl.when(g < nseg // 2)(lambda: loop(0, cps, False))
        pl.when(g >= nseg // 2)(lambda: loop(0, cps, True))
    if emit_state:
        assert nseg == 1
        for d in range(2):
            for hh in range(hpb):
                sf_ref[d, hh] = st_scr[d, hh].T


def _hgrn_features(in_refs, r0, head, reverse):
    hq_ref, hf_ref, hb_ref, hi_ref, lb_ref = in_refs
    rows, cols = pl.ds(r0, SCAN_C), slice(head * HG_DK, (head + 1) * HG_DK)
    q = _silu(hq_ref[rows, cols]) * (HG_DK ** -0.5)
    lb = lb_ref[1 if reverse else 0][:, cols]
    f = lb + (1.0 - lb) * _sigmoid((hb_ref if reverse else hf_ref)[rows, cols])
    return q, 1.0 - f, hi_ref[rows, head * HG_DV:(head + 1) * HG_DV], jnp.log(f) * LOG2E


def _gla_features(in_refs, r0, head, reverse):
    gq_ref, gk_ref, gv_ref, low_ref, wgk_ref, bgk_ref = in_refs
    rows, cols = pl.ds(r0, SCAN_C), slice(head * GLA_DK, (head + 1) * GLA_DK)
    d = 1 if reverse else 0
    g = _dot(low_ref[rows, :].astype(BF16), wgk_ref[d][:, cols]) + bgk_ref[d][:, cols]
    la = (jnp.minimum(g, 0.0) - jnp.log(1.0 + jnp.exp(-jnp.abs(g)))) * (LOG2E / GLA_NORMALIZER)
    return (gq_ref[rows, cols] * (GLA_DK ** -0.5), gk_ref[rows, cols],
            gv_ref[rows, head * GLA_DV:(head + 1) * GLA_DV], la)


def _seq_geom(latent):
    seq = DEC_SEQ if latent else SEQ
    return seq, (DEC_BATCH if latent else BATCH), (TP // seq if latent else 0)


def _zspec(latent, off, width, per_head=True):
    seq, _, blk0 = _seq_geom(latent)
    assert off % width == 0
    if per_head:
        return pl.BlockSpec((seq, width), lambda b, h, *_: (blk0 + b, off // width + h))
    return pl.BlockSpec((seq, width), lambda b, h, *_: (blk0 + b, off // width))


def _const_spec(a):
    return pl.BlockSpec(a.shape, lambda *_, nd=a.ndim: (0,) * nd)


SCAN_HEAD_COLS = {
    HG_HEADS: ((OFF_HQ, HG_DK), (OFF_HFF, HG_DK), (OFF_HFB, HG_DK), (OFF_HI, HG_DV), (OFF_HG, HG_DV)),
    GLA_HEADS: ((OFF_GQ, GLA_DK), (OFF_GK, GLA_DK), (OFF_GV, GLA_DV), (OFF_GG, GLA_DV)),
}


def _scan_geom(latent, heads=HG_HEADS):
    hpb = min(heads, 4 if latent else 8)
    while any(off % (hpb * width) for off, width in SCAN_HEAD_COLS[heads]):
        hpb //= 2
    return hpb, (8 if latent else 1)


def _zseg_spec(latent, off, width, reverse, per_head=True):
    seq, _, blk0 = _seq_geom(latent)
    nseg = _scan_geom(latent)[1]
    assert off % width == 0
    col0 = off // width

    def index(b, h, g=0):
        seg = (nseg - 1 - g) if reverse else g
        return ((blk0 + b) * nseg + seg, col0 + (h if per_head else 0))

    return pl.BlockSpec((seq // nseg, width), index)


def _scan_call(name, features, feat_fn, z, gate_off, gain, heads, dk, dv,
               latent, s0t, prev_out, prev_state, layer):
    seq, nb, row_blk0 = _seq_geom(latent)
    hpb, nseg = _scan_geom(latent, heads)
    consts = _scan_consts()
    in_specs = [_const_spec(a) for a in consts]
    args = list(consts)
    for reverse in ((False,) if nseg == 1 else (False, True)):
        specs, fargs = feat_fn(reverse)
        n_in = len(specs)
        in_specs += list(specs) + [_zseg_spec(latent, gate_off, hpb * dv, reverse)]
        args += list(fargs) + [z]
    in_specs.append(pl.BlockSpec((1, dv), lambda *_: (0, 0)))
    args.append(gain.reshape(1, dv))
    if latent:
        in_specs.append(pl.BlockSpec((None, None, 2, hpb, dv, dk), lambda b, h, *_: (b, layer, 0, h, 0, 0)))
        args.append(s0t)
    out_specs = [pl.BlockSpec((seq, hpb * dv), lambda b, h, *_: (row_blk0 + b, h))]
    out_shape = [jax.ShapeDtypeStruct((T, heads * dv), BF16)]
    aliases = {}
    if prev_out is not None:
        in_specs.append(pl.BlockSpec(memory_space=pl.ANY))
        aliases[len(args)] = 0
        args.append(prev_out)
    emit_state = not latent
    if emit_state:
        out_specs.append(pl.BlockSpec((None, None, 2, hpb, dk, dv), lambda b, h, *_: (b, layer, 0, h, 0, 0)))
        out_shape.append(jax.ShapeDtypeStruct((BATCH, DEPTH, 2, heads, dk, dv), F32))
        if prev_state is not None:
            in_specs.append(pl.BlockSpec(memory_space=pl.ANY))
            aliases[len(args)] = 1
            args.append(prev_state)
    n_alias = len(aliases)

    def kern(*refs):
        n_inputs = len(args) - n_alias
        keep = refs[:n_inputs] + refs[n_inputs + n_alias:]
        _scan_kernel(*keep, features=features, n_in=n_in, seq=seq, nseg=nseg, hpb=hpb, dv=dv, with_s0=latent,
                     emit_state=emit_state)

    grid = (nb, heads // hpb) + ((nseg,) if nseg > 1 else ())
    sem = ("parallel", "parallel") + (("arbitrary",) if nseg > 1 else ())
    res = pl.pallas_call(
        kern,
        grid=grid,
        in_specs=in_specs,
        out_specs=out_specs,
        out_shape=out_shape,
        input_output_aliases=aliases,
        scratch_shapes=[pltpu.VMEM((seq, hpb * dv), F32), pltpu.VMEM((2, hpb, dv, dk), F32)],
        compiler_params=_cparams(sem),
        name=name,
    )(*args)
    return (res[0], res[1]) if emit_state else (res[0], None)


def _hgrn(z, lb, gain, s0t, prev_state, layer):
    lb3 = lb.reshape(2, 1, HG_K)
    o = state = None
    for latent in (False, True):
        hpb = _scan_geom(latent, HG_HEADS)[0]

        def feat_fn(reverse, latent=latent, hpb=hpb):
            zs = functools.partial(_zseg_spec, latent, reverse=reverse)
            return ([zs(OFF_HQ, hpb * HG_DK), zs(OFF_HFF, hpb * HG_DK), zs(OFF_HFB, hpb * HG_DK),
                     zs(OFF_HI, hpb * HG_DV), pl.BlockSpec((2, 1, hpb * HG_DK), lambda b, h, *_: (0, 0, h))],
                    [z, z, z, z, lb3])

        o, st = _scan_call("hgrn_latent" if latent else "hgrn_prompt", _hgrn_features, feat_fn,
                           z, OFF_HG, gain, HG_HEADS, HG_DK, HG_DV, latent, s0t, o, prev_state, layer)
        state = st if st is not None else state
    return o, state


def _gla(z, w_gk, b_gk, gain, s0t, prev_state, layer):
    wpad = jnp.zeros((2, LANE, GLA_HEADS * GLA_DK), F32)
    wpad = wpad.at[0, 0:GLA_RANK].set(w_gk[0]).at[1, GLA_RANK:2 * GLA_RANK].set(w_gk[1]).astype(BF16)
    b3 = b_gk.reshape(2, 1, GLA_HEADS * GLA_DK)
    o = state = None
    for latent in (False, True):
        hpb = _scan_geom(latent, GLA_HEADS)[0]

        def feat_fn(reverse, latent=latent, hpb=hpb):
            zs = functools.partial(_zseg_spec, latent, reverse=reverse)
            return ([zs(OFF_GQ, hpb * GLA_DK), zs(OFF_GK, hpb * GLA_DK), zs(OFF_GV, hpb * GLA_DV),
                     zs(OFF_GLF, LANE, per_head=False),
                     pl.BlockSpec((2, LANE, hpb * GLA_DK), lambda b, h, *_: (0, 0, h)),
                     pl.BlockSpec((2, 1, hpb * GLA_DK), lambda b, h, *_: (0, 0, h))],
                    [z, z, z, z, wpad, b3])

        o, st = _scan_call("gla_latent" if latent else "gla_prompt", _gla_features, feat_fn,
                           z, OFF_GG, gain, GLA_HEADS, GLA_DK, GLA_DV, latent, s0t, o, prev_state, layer)
        state = st if st is not None else state
    return o, state


GQA = N_HEADS // N_KV_HEADS


def _ctx_attn_kernel(q_ref, k_ref, v_ref, sink_ref, wo_ref, o_ref, wob_ref):
    wob_ref[...] = wo_ref[...].astype(wob_ref.dtype)
    hd = HEAD_DIM
    ks = [k_ref[:, kh * hd:(kh + 1) * hd].astype(BF16) for kh in range(N_KV_HEADS)]
    vs = [v_ref[:, kh * hd:(kh + 1) * hd].astype(BF16) for kh in range(N_KV_HEADS)]
    heads = range(N_HEADS)
    ss = [_dot_nt((q_ref[:, h * hd:(h + 1) * hd] * (hd ** -0.5)).astype(BF16), ks[h // GQA]) for h in heads]
    ps, dens = [], []
    for h in heads:
        sink = sink_ref[h][:, 0:1]
        m = jnp.maximum(jnp.max(ss[h], axis=-1, keepdims=True), sink)
        p = jnp.exp(ss[h] - m)
        dens.append(jnp.sum(p, axis=-1, keepdims=True) + jnp.exp(sink - m))
        ps.append(p.astype(BF16))
    for h in heads:
        o_ref[:, h * hd:(h + 1) * hd] = (_dot(ps[h], vs[h // GQA]) / dens[h]).astype(o_ref.dtype)


def _ctx_attention(z, sink3, w_out, layer):
    hd = HEAD_DIM
    kv_w = N_KV_HEADS * hd
    assert OFF_AK % kv_w == 0 and OFF_AV % kv_w == 0
    k_out, n_out = w_out.shape[1:]
    wo_rows = k_out // BATCH
    assert k_out % BATCH == 0 and wo_rows % 16 == 0
    return pl.pallas_call(
        _ctx_attn_kernel,
        grid=(BATCH,),
        in_specs=[pl.BlockSpec((SEQ, N_HEADS * hd), lambda b: (b, OFF_AQ // (N_HEADS * hd))),
                  pl.BlockSpec((SEQ, kv_w), lambda b: (b, OFF_AK // kv_w)),
                  pl.BlockSpec((SEQ, kv_w), lambda b: (b, OFF_AV // kv_w)),
                  pl.BlockSpec((N_HEADS, 1, LANE), lambda b: (0, 0, 0)),
                  pl.BlockSpec((None, wo_rows, n_out), lambda b: (layer, b, 0))],
        out_specs=[pl.BlockSpec((SEQ, N_HEADS * hd), lambda b: (b, 0)),
                   pl.BlockSpec((None, wo_rows, n_out), lambda b: (0, b, 0))],
        out_shape=[jax.ShapeDtypeStruct((T, N_HEADS * hd), BF16), jax.ShapeDtypeStruct((1, k_out, n_out), BF16)],
        compiler_params=_cparams(("parallel",)),
        name="ctx_attention",
    )(z, z, z, sink3, w_out)


def _rope(x, cos_f, sin_f):
    return x * cos_f + pltpu.roll(x, HEAD_DIM // 2, 1) * sin_f


def _lat_attn_kernel(q_ref, k_ref, v_ref, ck_ref, cv_ref, cos_ref, sin_ref, sink_ref, o_ref, kr_scr, vr_scr):
    blk, hd, seq = ATT_BLOCK, HEAD_DIM, DEC_SEQ
    nb = seq // blk

    zeros = jnp.zeros((blk, hd), BF16)
    for scr in (kr_scr, vr_scr):
        scr[0:blk, :] = zeros
        scr[blk + seq:2 * blk + seq, :] = zeros

    def fill(n, carry):
        r0 = pl.multiple_of(n * blk, blk)
        rows = pl.ds(r0, blk)
        kr_scr[pl.ds(r0 + blk, blk), :] = _rope(k_ref[rows, :], cos_ref[rows, :], sin_ref[rows, :]).astype(BF16)
        vr_scr[pl.ds(r0 + blk, blk), :] = v_ref[rows, :].astype(BF16)
        return carry

    lax.fori_loop(0, nb, fill, 0)

    ck = ck_ref[...].astype(BF16)
    cv = cv_ref[...].astype(BF16)
    sink = jnp.concatenate([jnp.broadcast_to(sink_ref[g][:, 0:1], (blk, 1)) for g in range(GQA)], axis=0)
    qi = lax.broadcasted_iota(jnp.int32, (GQA * blk, 3 * blk), 0) & (blk - 1)
    kj = lax.broadcasted_iota(jnp.int32, (GQA * blk, 3 * blk), 1)
    window_bias = jnp.where(jnp.abs(kj - qi - blk) <= WINDOW, 0.0, -jnp.inf)

    per_iter = 4

    def qblocks(it, carry):
        ns = [it * per_iter + u for u in range(per_iter)]
        r0s = [pl.multiple_of(n * blk, blk) for n in ns]
        qs = []
        for r0 in r0s:
            rows = pl.ds(r0, blk)
            cos_f, sin_f = cos_ref[rows, :], sin_ref[rows, :]
            qs.append(jnp.concatenate(
                [(_rope(q_ref[rows, g * hd:(g + 1) * hd], cos_f, sin_f) * (hd ** -0.5)).astype(BF16)
                 for g in range(GQA)], axis=0))
        bands = [pl.ds(r0, 3 * blk) for r0 in r0s]
        s_locs = [_dot_nt(q, kr_scr[band, :]) for q, band in zip(qs, bands)]
        s_ctxs = [_dot_nt(q, ck) for q in qs]
        ps = []
        for n, s_loc, s_ctx in zip(ns, s_locs, s_ctxs):
            key_pos = kj + (n - 1) * blk
            s_loc = s_loc + window_bias
            s_loc = jnp.where(key_pos >= 0, s_loc, -jnp.inf)
            s_loc = jnp.where(key_pos < seq, s_loc, -jnp.inf)
            m = jnp.maximum(jnp.maximum(jnp.max(s_loc, axis=-1, keepdims=True),
                                        jnp.max(s_ctx, axis=-1, keepdims=True)), sink)
            p_loc = jnp.exp(s_loc - m)
            p_ctx = jnp.exp(s_ctx - m)
            den = (jnp.sum(p_loc, axis=-1, keepdims=True) + jnp.sum(p_ctx, axis=-1, keepdims=True)
                   + jnp.exp(sink - m))
            ps.append((p_ctx.astype(BF16), p_loc.astype(BF16), den))
        for r0, band, (p_ctx, p_loc, den) in zip(r0s, bands, ps):
            o = (_dot(p_ctx, cv) + _dot(p_loc, vr_scr[band, :])) / den
            for g in range(GQA):
                o_ref[pl.ds(r0, blk), g * hd:(g + 1) * hd] = o[g * blk:(g + 1) * blk].astype(o_ref.dtype)
        return carry

    lax.fori_loop(0, nb // per_iter, qblocks, 0)


def _rope_tables():
    n = jnp.arange(DEC_SEQ)
    row = (n // GRID_W).astype(F32)
    col = (n % GRID_W).astype(F32)
    n_freq = HEAD_DIM // 4
    inv = ROPE_THETA ** (-jnp.arange(n_freq, dtype=F32) / n_freq)
    ang = jnp.concatenate([row[:, None] * inv, col[:, None] * inv], axis=-1)
    cos, sin = jnp.cos(ang), jnp.sin(ang)
    return jnp.concatenate([cos, cos], axis=-1), jnp.concatenate([-sin, sin], axis=-1)


def _lat_attention(z, ck, cv, sink3, prev_out):
    hd = HEAD_DIM
    seq, _, blk0 = _seq_geom(True)
    cos_f, sin_f = _rope_tables()
    tab = pl.BlockSpec((seq, hd), lambda b, kh: (0, 0))
    cache = pl.BlockSpec((None, PAST_LEN, hd), lambda b, kh: (b, 0, kh))
    return pl.pallas_call(
        lambda *refs: _lat_attn_kernel(*refs[:8], *refs[9:]),
        grid=(DEC_BATCH, N_KV_HEADS),
        in_specs=[_zspec(True, OFF_AQ, GQA * hd),
                  pl.BlockSpec((seq, hd), lambda b, kh: (blk0 + b, OFF_AK // hd + kh)),
                  pl.BlockSpec((seq, hd), lambda b, kh: (blk0 + b, OFF_AV // hd + kh)),
                  cache, cache, tab, tab,
                  pl.BlockSpec((GQA, 1, LANE), lambda b, kh: (kh, 0, 0)),
                  pl.BlockSpec(memory_space=pl.ANY)],
        out_specs=pl.BlockSpec((seq, GQA * hd), lambda b, kh: (blk0 + b, kh)),
        out_shape=jax.ShapeDtypeStruct((T, N_HEADS * hd), BF16),
        input_output_aliases={8: 0},
        scratch_shapes=[pltpu.VMEM((seq + 2 * ATT_BLOCK, hd), BF16), pltpu.VMEM((seq + 2 * ATT_BLOCK, hd), BF16)],
        compiler_params=_cparams(("parallel", "parallel")),
        name="lat_attention",
    )(z, z, z, ck, cv, cos_f, sin_f, sink3, prev_out)


def kernel(x_prompt, x_sample, cache_k, cache_v, state_hgrn, state_gla, c, c_ctx, w_ada, b_ada, norm_g, w_in,
           w_out, sgu_ln_g, sgu_ln_b, sgu_w, sgu_b, hgrn_lb_logits, hgrn_norm_g, attn_sink, gla_w_gk, gla_b_gk,
           gla_norm_g, ffn_w_up, ffn_conv_w, ffn_conv_b, ffn_w_down):
    assert TP % DEC_SEQ == 0 and DEC_BATCH + 1 <= MOD_ROWS
    lb_all = jnp.cumsum(jax.nn.softmax(hgrn_lb_logits.astype(F32), axis=0), axis=0)
    lb_all = lb_all - lb_all[0:1]

    cvec = jnp.zeros((MOD_ROWS, D_MODEL), F32).at[0].set(c_ctx).at[1:1 + DEC_BATCH].set(c)
    mod = _ada_mod(cvec, w_ada, b_ada).reshape(DEPTH, MOD_ROWS, 1, 6 * D_MODEL)
    SH1, SC1, G1, SH2, SC2, G2 = range(6)

    x = [x_prompt.reshape(TP, D_MODEL), x_sample.reshape(TL, D_MODEL)]
    hg_s0t = jnp.swapaxes(state_hgrn, -1, -2)
    gl_s0t = jnp.swapaxes(state_gla, -1, -2)
    ck_all = cache_k.reshape(DEC_BATCH, DEPTH, PAST_LEN, N_KV_HEADS * HEAD_DIM)
    cv_all = cache_v.reshape(DEC_BATCH, DEPTH, PAST_LEN, N_KV_HEADS * HEAD_DIM)

    h = _norm_mod(x, norm_g[0, 0], mod, 0, SC1, SH1)
    ks_new, vs_new = [], []
    hg_state = gl_state = None
    w_in_b, w_up_b = w_in.astype(BF16), ffn_w_up
    for l in range(DEPTH):
        sink3 = jnp.broadcast_to(attn_sink[l].astype(F32)[:, None, None], (N_HEADS, 1, LANE))

        z = _matmul(h, w_in_b, l, Z_COLS, F32, Z_TILE_N, "in_proj")
        o_sgu = _sgu(z, sgu_ln_g[l], sgu_ln_b[l], sgu_w[l], sgu_b[l])
        o_hg, hg_state = _hgrn(z, lb_all[l], hgrn_norm_g[l], hg_s0t, hg_state, l)
        o_att, w_out_b = _ctx_attention(z, sink3, w_out, l)
        o_att = _lat_attention(z, ck_all[:, l], cv_all[:, l], sink3, o_att)
        o_gl, gl_state = _gla(z, gla_w_gk[l], gla_b_gk[l], gla_norm_g[l], gl_s0t, gl_state, l)
        m = _out_proj([o_sgu, o_hg, o_att, o_gl], w_out_b, 0)
        x, h = _resid(x, m, norm_g[l, 1], mod, l, G1, nxt=(norm_g[l, 2], l, SC2, SH2))
        ks_new.append(z[:TP, OFF_AK:OFF_AK + N_KV_HEADS * HEAD_DIM].reshape(BATCH, SEQ, N_KV_HEADS, HEAD_DIM))
        vs_new.append(z[:TP, OFF_AV:OFF_AV + N_KV_HEADS * HEAD_DIM].reshape(BATCH, SEQ, N_KV_HEADS, HEAD_DIM))

        act, w_down_b = _ffn_up(h, _conv_halo(h), w_up_b, ffn_conv_w, ffn_conv_b, ffn_w_down, l)
        f = _ffn_down(act, w_down_b, 0)
        if l + 1 < DEPTH:
            x, h = _resid(x, f, norm_g[l, 3], mod, l, G2, nxt=(norm_g[l + 1, 0], l + 1, SC1, SH1))
        else:
            x, h = _resid(x, f, norm_g[l, 3], mod, l, G2, split_out=True)

    y_prompt = x[0].reshape(BATCH, SEQ, D_MODEL)
    y_sample = x[1].reshape(DEC_BATCH, DEC_SEQ, D_MODEL)
    return (y_prompt, y_sample, jnp.stack(ks_new, axis=1), jnp.stack(vs_new, axis=1), hg_state, gl_state)
```
